```python
import math
import jax, jax.numpy as jnp
from jax import lax
import numpy as np

D_MODEL = 1024
BATCH = 1
SEQ = 16384
DEPTH = 4
DEC_BATCH = 16
DEC_SEQ = 32
PAST_LEN = 4096

CHUNK = 64
N_MIXERS = 3
N_GDN = (DEPTH + 2) // 3
N_SB = (DEPTH + 1) // 3
N_RET = DEPTH // 3
GDN_HEADS = 8
GDN_DK = 128
GDN_DV = 128
GDN_CONV = 4
GDN_QKV = 2 * GDN_HEADS * GDN_DK + GDN_HEADS * GDN_DV
SB_HEADS = 16
SB_DH = 64
SB_BLOCK = 128
RET_HEADS = 4
RET_DK = 256
RET_DV = 512
RET_ROPE_BASE = 10000.0
D_FF = 2816
FFN_CONV = 3
NORM_EPS = 1e-6
F32 = jnp.float32

kernel_name = 'hybrid_gdn_stickbreak_retention_stream_step'


def rms_norm(x, g):
    xf = x.astype(F32)
    y = xf * lax.rsqrt(jnp.mean(xf * xf, axis=-1, keepdims=True) + NORM_EPS)
    return (y * g.astype(F32)).astype(x.dtype)


def l2_normalize(x):
    xf = x.astype(F32)
    return xf * lax.rsqrt(jnp.sum(xf * xf, axis=-1, keepdims=True) + NORM_EPS)


def head_group_norm(o, g):
    mu = jnp.mean(o, axis=-1, keepdims=True)
    d = o - mu
    var = jnp.mean(d * d, axis=-1, keepdims=True)
    return d * lax.rsqrt(var + NORM_EPS) * g.astype(F32).reshape(o.shape[2:])


def causal_depthwise_conv(x, buf, w):
    width = w.shape[0]
    T = x.shape[1]
    xp = jnp.concatenate([buf.astype(x.dtype), x], axis=1)
    y = xp[:, 0:T] * w[0]
    for j in range(1, width):
        y = y + xp[:, j:j + T] * w[j]
    return y, xp[:, xp.shape[1] - (width - 1):]


def to_chunks(x, chunk):
    B, T, H = x.shape[:3]
    x = x.astype(F32).reshape((B, T // chunk, chunk, H) + x.shape[3:])
    return jnp.moveaxis(x, (1, 3), (0, 2))


def from_chunks(o):
    N, B, H, C = o.shape[:4]
    return jnp.moveaxis(o, (0, 2), (1, 3)).reshape((B, N * C, H) + o.shape[4:])


def rotary(x, pos):
    half = x.shape[-1] // 2
    inv_freq = RET_ROPE_BASE ** (-jnp.arange(half, dtype=F32) / half)
    ang = pos.astype(F32)[:, None] * inv_freq[None, :]
    cos = jnp.cos(ang)[None, :, None, :]
    sin = jnp.sin(ang)[None, :, None, :]
    xf = x.astype(F32)
    x1, x2 = xf[..., :half], xf[..., half:]
    return jnp.concatenate([x1 * cos - x2 * sin, x1 * sin + x2 * cos], axis=-1)


def gated_delta_rule(q, k, v, g, beta, S0, chunk):
    qc, kc, vc = to_chunks(q, chunk), to_chunks(k, chunk), to_chunks(v, chunk)
    gc, bc = to_chunks(g, chunk), to_chunks(beta, chunk)
    G = jnp.cumsum(gc, axis=-1)
    idx = jnp.arange(chunk)
    incl = idx[:, None] >= idx[None, :]
    strict = idx[:, None] > idx[None, :]
    decay = jnp.exp(jnp.where(incl, G[..., :, None] - G[..., None, :], -jnp.inf))
    kb = kc * bc[..., None]
    L = jnp.where(strict, jnp.einsum('nbhik,nbhjk->nbhij', kb, kc) * decay, 0.0)
    eye = jnp.eye(chunk, dtype=F32)
    Tinv = lax.linalg.triangular_solve(eye + L, jnp.broadcast_to(eye, L.shape), left_side=True, lower=True)
    u = jnp.einsum('nbhij,nbhjv->nbhiv', Tinv, vc * bc[..., None])
    w = jnp.einsum('nbhij,nbhjk->nbhik', Tinv, kb * jnp.exp(G)[..., None])
    qk = jnp.einsum('nbhik,nbhjk->nbhij', qc, kc) * decay
    q_dec = qc * jnp.exp(G)[..., None]
    k_dec = kc * jnp.exp(G[..., -1:] - G)[..., None]
    g_last = jnp.exp(G[..., -1])

    def step(S, inp):
        u_n, w_n, qk_n, qd_n, kd_n, gl_n = inp
        v_new = u_n - jnp.einsum('bhck,bhkv->bhcv', w_n, S)
        o = jnp.einsum('bhck,bhkv->bhcv', qd_n, S) + jnp.einsum('bhij,bhjv->bhiv', qk_n, v_new)
        S = S * gl_n[..., None, None] + jnp.einsum('bhck,bhcv->bhkv', kd_n, v_new)
        return S, o

    S, o = lax.scan(step, S0.astype(F32), (u, w, qk, q_dec, k_dec, g_last))
    return from_chunks(o), S


def gdn_mixer(h, S0, conv_buf, w_in, conv_w, A_log, dt_bias, norm_g, w_out):
    B, T, _ = h.shape
    chunk = min(CHUNK, T)
    hv = GDN_HEADS * GDN_DV
    qkv, z, a, b = jnp.split(h @ w_in, [GDN_QKV, GDN_QKV + hv, GDN_QKV + hv + GDN_HEADS], axis=-1)
    qkv, new_buf = causal_depthwise_conv(qkv, conv_buf, conv_w)
    qkv = jax.nn.silu(qkv)
    q, k, v = jnp.split(qkv, [GDN_HEADS * GDN_DK, 2 * GDN_HEADS * GDN_DK], axis=-1)
    q = l2_normalize(q.reshape(B, T, GDN_HEADS, GDN_DK)) * GDN_DK ** -0.5
    k = l2_normalize(k.reshape(B, T, GDN_HEADS, GDN_DK))
    v = v.reshape(B, T, GDN_HEADS, GDN_DV)
    beta = jax.nn.sigmoid(b.astype(F32))
    g = -jnp.exp(A_log.astype(F32)) * jax.nn.softplus(a.astype(F32) + dt_bias.astype(F32))
    o, S = gated_delta_rule(q, k, v, g, beta, S0, chunk)
    o = rms_norm(o, norm_g) * jax.nn.silu(z.astype(F32)).reshape(B, T, GDN_HEADS, GDN_DV)
    return o.reshape(B, T, hv).astype(h.dtype) @ w_out, S.astype(S0.dtype), new_buf


def stick_breaking_attend(q, k, v, q_start):
    B, Tq, H, Dh = q.shape
    Tk = k.shape[1]
    blk = min(SB_BLOCK, Tq)
    nb = Tq // blk
    kf = k.astype(F32)
    vf = v.astype(F32)
    key_pos = jnp.arange(Tk)
    scale = Dh ** -0.5

    def one_block(args):
        qb, start = args
        z = jnp.einsum('bqhd,bkhd->bhqk', qb.astype(F32), kf) * scale
        q_pos = start + jnp.arange(blk)
        mask = key_pos[None, :] < q_pos[:, None]
        log_not = jnp.where(mask, jax.nn.log_sigmoid(-z), 0.0)
        after = lax.cumsum(log_not, axis=3, reverse=True) - log_not
        wts = jnp.where(mask, jnp.exp(jax.nn.log_sigmoid(z) + after), 0.0)
        return jnp.einsum('bhqk,bkhd->bqhd', wts, vf)

    qb = jnp.moveaxis(q.reshape(B, nb, blk, H, Dh), 1, 0)
    starts = q_start + blk * jnp.arange(nb)
    o = lax.map(one_block, (qb, starts))
    return jnp.moveaxis(o, 0, 1).reshape(B, Tq, H, Dh)


def sb_mixer(h, past_k, past_v, w_in, q_norm_g, k_norm_g, w_out):
    B, T, _ = h.shape
    q, k, v = jnp.split(h @ w_in, 3, axis=-1)
    q = rms_norm(q.reshape(B, T, SB_HEADS, SB_DH), q_norm_g)
    k = rms_norm(k.reshape(B, T, SB_HEADS, SB_DH), k_norm_g)
    v = v.reshape(B, T, SB_HEADS, SB_DH)
    k_all = jnp.concatenate([past_k.astype(k.dtype), k], axis=1)
    v_all = jnp.concatenate([past_v.astype(v.dtype), v], axis=1)
    o = stick_breaking_attend(q, k_all, v_all, past_k.shape[1])
    return o.reshape(B, T, SB_HEADS * SB_DH).astype(h.dtype) @ w_out, k, v


def retention_chunked(q, k, v, log_gamma, R0, chunk):
    qc, kc, vc = to_chunks(q, chunk), to_chunks(k, chunk), to_chunks(v, chunk)
    idx = jnp.arange(chunk, dtype=F32)
    rel = idx[:, None] - idx[None, :]
    lg = log_gamma[:, None, None]
    dmask = jnp.where(rel >= 0, jnp.exp(lg * jnp.maximum(rel, 0.0)), 0.0)
    inner = jnp.einsum('nbhij,nbhjv->nbhiv', jnp.einsum('nbhik,nbhjk->nbhij', qc, kc) * dmask, vc)
    q_dec = qc * jnp.exp(log_gamma[:, None] * (idx + 1.0))[:, :, None]
    k_dec = kc * jnp.exp(log_gamma[:, None] * (chunk - 1.0 - idx))[:, :, None]
    g_chunk = jnp.exp(log_gamma * chunk)[:, None, None]

    def step(R, inp):
        qd, kd, vn = inp
        o = jnp.einsum('bhck,bhkv->bhcv', qd, R)
        R = R * g_chunk + jnp.einsum('bhck,bhcv->bhkv', kd, vn)
        return R, o

    R, cross = lax.scan(step, R0.astype(F32), (q_dec, k_dec, vc))
    return from_chunks(inner + cross), R


def ret_mixer(h, R0, w_in, norm_g, w_out, pos0):
    B, T, _ = h.shape
    chunk = min(CHUNK, T)
    hk = RET_HEADS * RET_DK
    hv = RET_HEADS * RET_DV
    q, k, v, gt = jnp.split(h @ w_in, [hk, 2 * hk, 2 * hk + hv], axis=-1)
    pos = pos0 + jnp.arange(T)
    q = rotary(q.reshape(B, T, RET_HEADS, RET_DK), pos)
    k = rotary(k.reshape(B, T, RET_HEADS, RET_DK), pos) * RET_DK ** -0.5
    v = v.reshape(B, T, RET_HEADS, RET_DV)
    log_gamma = jnp.log1p(-jnp.exp2(-5.0 - jnp.arange(RET_HEADS, dtype=F32)))
    o, R = retention_chunked(q, k, v, log_gamma, R0, chunk)
    o = head_group_norm(o, norm_g) * jax.nn.silu(gt.astype(F32)).reshape(B, T, RET_HEADS, RET_DV)
    return o.reshape(B, T, hv).astype(h.dtype) @ w_out, R.astype(R0.dtype)


def conv_ffn(h, buf, w_in, conv_w, conv_b, w_out):
    u, new_buf = causal_depthwise_conv(h @ w_in, buf, conv_w)
    gate, val = jnp.split(u + conv_b, 2, axis=-1)
    return (jax.nn.silu(gate) * val) @ w_out, new_buf


def setup_inputs(seed: int = 0) -> dict:
    key = jax.random.key(seed)
    ks = iter(jax.random.split(key, 48))
    D = D_MODEL
    gdn_proj = GDN_QKV + GDN_HEADS * GDN_DV + 2 * GDN_HEADS
    ret_proj = 2 * RET_HEADS * RET_DK + 2 * RET_HEADS * RET_DV

    def nrm(shape, scale):
        return jax.random.normal(next(ks), shape, F32) * scale

    def gain(shape):
        return 1.0 + 0.02 * jax.random.normal(next(ks), shape, F32)

    dt = jnp.exp(jax.random.uniform(next(ks), (N_GDN, GDN_HEADS), F32, math.log(1e-3), math.log(1e-1)))
    dt_bias = dt + jnp.log(-jnp.expm1(-dt))
    A_log = jnp.log(jax.random.uniform(next(ks), (N_GDN, GDN_HEADS), F32, 1.0, 16.0))
    return {
        'x_prompt': nrm((BATCH, SEQ, D), 1.0),
        'x_sample': nrm((DEC_BATCH, DEC_SEQ, D), 1.0),
        'state_l0_gdn_S': nrm((DEC_BATCH, GDN_HEADS, GDN_DK, GDN_DV), 0.1),
        'state_l0_gdn_conv': nrm((DEC_BATCH, GDN_CONV - 1, GDN_QKV), 1.0),
        'cache_l1_sb_k': nrm((DEC_BATCH, PAST_LEN, SB_HEADS, SB_DH), 1.0),
        'cache_l1_sb_v': nrm((DEC_BATCH, PAST_LEN, SB_HEADS, SB_DH), 1.0),
        'state_l2_ret': nrm((DEC_BATCH, RET_HEADS, RET_DK, RET_DV), 0.1),
        'state_l3_gdn_S': nrm((DEC_BATCH, GDN_HEADS, GDN_DK, GDN_DV), 0.1),
        'state_l3_gdn_conv': nrm((DEC_BATCH, GDN_CONV - 1, GDN_QKV), 1.0),
        'state_ffn_conv': nrm((DEPTH, DEC_BATCH, FFN_CONV - 1, 2 * D_FF), 1.0),
        'c_prompt': nrm((BATCH, D), 1.0),
        'c_sample': nrm((DEC_BATCH, D), 1.0),
        'ada_w': nrm((DEPTH, D, 6 * D), D ** -0.5),
        'ada_b': nrm((DEPTH, 6 * D), 0.02),
        'norm_mix_g': gain((DEPTH, D)),
        'norm_ffn_g': gain((DEPTH, D)),
        'gdn_w_in': nrm((N_GDN, D, gdn_proj), D ** -0.5),
        'gdn_conv_w': nrm((N_GDN, GDN_CONV, GDN_QKV), GDN_CONV ** -0.5),
        'gdn_A_log': A_log,
        'gdn_dt_bias': dt_bias,
        'gdn_norm_g': gain((N_GDN, GDN_DV)),
        'gdn_w_out': nrm((N_GDN, GDN_HEADS * GDN_DV, D), (GDN_HEADS * GDN_DV) ** -0.5),
        'sb_w_in': nrm((N_SB, D, 3 * SB_HEADS * SB_DH), D ** -0.5),
        'sb_q_norm_g': gain((N_SB, SB_DH)),
        'sb_k_norm_g': gain((N_SB, SB_DH)),
        'sb_w_out': nrm((N_SB, SB_HEADS * SB_DH, D), (SB_HEADS * SB_DH) ** -0.5),
        'ret_w_in': nrm((N_RET, D, ret_proj), D ** -0.5),
        'ret_norm_g': gain((N_RET, RET_HEADS * RET_DV)),
        'ret_w_out': nrm((N_RET, RET_HEADS * RET_DV, D), (RET_HEADS * RET_DV) ** -0.5),
        'ffn_w_in': nrm((DEPTH, D, 2 * D_FF), D ** -0.5),
        'ffn_conv_w': nrm((DEPTH, FFN_CONV, 2 * D_FF), FFN_CONV ** -0.5),
        'ffn_conv_b': nrm((DEPTH, 2 * D_FF), 0.02),
        'ffn_w_out': nrm((DEPTH, D_FF, D), D_FF ** -0.5),
    }


def reference(x_prompt, x_sample, state_l0_gdn_S, state_l0_gdn_conv, cache_l1_sb_k, cache_l1_sb_v,
              state_l2_ret, state_l3_gdn_S, state_l3_gdn_conv, state_ffn_conv, c_prompt, c_sample,
              ada_w, ada_b, norm_mix_g, norm_ffn_g,
              gdn_w_in, gdn_conv_w, gdn_A_log, gdn_dt_bias, gdn_norm_g, gdn_w_out,
              sb_w_in, sb_q_norm_g, sb_k_norm_g, sb_w_out,
              ret_w_in, ret_norm_g, ret_w_out,
              ffn_w_in, ffn_conv_w, ffn_conv_b, ffn_w_out):

    def run_group(x, c, mixer_states, ffn_bufs, pos0):
        new_mixer_states = []
        new_ffn = []
        for i in range(DEPTH):
            kind, j = i % N_MIXERS, i // N_MIXERS
            mod = jax.nn.silu(c) @ ada_w[i] + ada_b[i]
            sh1, sc1, g1, sh2, sc2, g2 = [m[:, None, :] for m in jnp.split(mod, 6, axis=-1)]
            h = rms_norm(x, norm_mix_g[i]) * (1 + sc1) + sh1
            if kind == 0:
                S0, buf = mixer_states[i]
                out, S, buf = gdn_mixer(h, S0, buf, gdn_w_in[j], gdn_conv_w[j], gdn_A_log[j],
                                        gdn_dt_bias[j], gdn_norm_g[j], gdn_w_out[j])
                new = (S, buf)
            elif kind == 1:
                pk, pv = mixer_states[i]
                out, k, v = sb_mixer(h, pk, pv, sb_w_in[j], sb_q_norm_g[j], sb_k_norm_g[j], sb_w_out[j])
                new = (k, v)
            else:
                (R0,) = mixer_states[i]
                out, R = ret_mixer(h, R0, ret_w_in[j], ret_norm_g[j], ret_w_out[j], pos0)
                new = (R,)
            x = x + g1 * out
            h = rms_norm(x, norm_ffn_g[i]) * (1 + sc2) + sh2
            out, fbuf = conv_ffn(h, ffn_bufs[i], ffn_w_in[i], ffn_conv_w[i], ffn_conv_b[i], ffn_w_out[i])
            x = x + g2 * out
            new_mixer_states.append(new)
            new_ffn.append(fbuf)
        return x, new_mixer_states, jnp.stack(new_ffn)

    B, dt = x_prompt.shape[0], x_prompt.dtype
    zero_states = []
    for i in range(DEPTH):
        kind = i % N_MIXERS
        if kind == 0:
            zero_states.append((jnp.zeros((B, GDN_HEADS, GDN_DK, GDN_DV), dt),
                                jnp.zeros((B, GDN_CONV - 1, GDN_QKV), dt)))
        elif kind == 1:
            zero_states.append((jnp.zeros((B, 0, SB_HEADS, SB_DH), dt),
                                jnp.zeros((B, 0, SB_HEADS, SB_DH), dt)))
        else:
            zero_states.append((jnp.zeros((B, RET_HEADS, RET_DK, RET_DV), dt),))
    zero_ffn = jnp.zeros((DEPTH, B, FFN_CONV - 1, 2 * D_FF), dt)
    y_prompt, p_states, p_ffn_conv = run_group(x_prompt, c_prompt, zero_states, zero_ffn, 0)

    sample_states = [(state_l0_gdn_S, state_l0_gdn_conv), (cache_l1_sb_k, cache_l1_sb_v),
                     (state_l2_ret,), (state_l3_gdn_S, state_l3_gdn_conv)]
    past_len = cache_l1_sb_k.shape[1]
    y_sample, s_states, s_ffn_conv = run_group(x_sample, c_sample, sample_states, state_ffn_conv, past_len)

    (p_l0_S, p_l0_conv), (p_l1_k, p_l1_v), (p_l2_R,), (p_l3_S, p_l3_conv) = p_states
    (s_l0_S, s_l0_conv), (s_l1_k, s_l1_v), (s_l2_R,), (s_l3_S, s_l3_conv) = s_states
    return (y_prompt, y_sample,
            p_l0_S, p_l0_conv, p_l1_k, p_l1_v, p_l2_R, p_l3_S, p_l3_conv, p_ffn_conv,
            s_l0_S, s_l0_conv, s_l1_k, s_l1_v, s_l2_R, s_l3_S, s_l3_conv, s_ffn_conv)
```

```python
import functools
import math

import jax
import jax.numpy as jnp
from jax import lax
from jax.experimental import pallas as pl
from jax.experimental.pallas import tpu as pltpu

F32 = jnp.float32
BF16 = jnp.bfloat16

D_MODEL = 1024
DEPTH = 4
CHUNK = 64
GDN_HEADS = 8
GDN_DK = 128
GDN_QKV = 3 * GDN_HEADS * GDN_DK
SB_HEADS = 16
SB_DH = 64
RET_HEADS = 4
RET_DK = 256
RET_DV = 512
RET_ROPE_BASE = 10000.0
D_FF = 2816
NORM_EPS = 1e-6

LANES = 128
MXU_N = 256
VMEM_LIMIT = 56 * 1024 * 1024

ROW_TILE = 256
GDN_TILE = 256
RET_CHUNK = 256
SB_TILE = 256


def _cparams(*sem):
    return pltpu.CompilerParams(dimension_semantics=sem, vmem_limit_bytes=VMEM_LIMIT)


def _dot(a, b):
    return jnp.dot(a.astype(BF16), b.astype(BF16), preferred_element_type=F32)


def _dot_nt(a, b):
    return lax.dot_general(a.astype(BF16), b.astype(BF16), (((1,), (1,)), ((), ())),
                           preferred_element_type=F32)


def _dot_tn(a, b):
    return lax.dot_general(a.astype(BF16), b.astype(BF16), (((0,), (0,)), ((), ())),
                           preferred_element_type=F32)


def _split(x):
    hi = x.astype(BF16)
    lo = (x - hi.astype(F32)).astype(BF16)
    return hi, lo


def _dot_sel(a, sel):
    hi, lo = _split(a)
    return (jnp.dot(hi, sel, preferred_element_type=F32)
            + jnp.dot(lo, sel, preferred_element_type=F32))


def _sel_dot(sel, b):
    hi, lo = _split(b)
    return (jnp.dot(sel, hi, preferred_element_type=F32)
            + jnp.dot(sel, lo, preferred_element_type=F32))


def _dot3(a, b):
    ah, al = _split(a)
    bh, bl = _split(b)
    return (jnp.dot(ah, bh, preferred_element_type=F32)
            + jnp.dot(ah, bl, preferred_element_type=F32)
            + jnp.dot(al, bh, preferred_element_type=F32))


def _sigmoid(x):
    return 1.0 / (1.0 + jnp.exp(-x))


def _silu(x):
    return x * _sigmoid(x)


def _softplus(x):
    return jnp.maximum(x, 0.0) + jnp.log(1.0 + jnp.exp(-jnp.abs(x)))


def _norm_mod(x, gain, scale, shift):
    ms = jnp.mean(x * x, axis=-1, keepdims=True)
    y = x * lax.rsqrt(ms + NORM_EPS) * gain
    return y * (1.0 + scale) + shift


def _iota2(shape, dim):
    return lax.broadcasted_iota(jnp.int32, shape, dim)


def _ada_kernel(c_ref, w_ref, b_ref, o_ref):
    o_ref[0] = _dot(_silu(c_ref[...]), w_ref[0]) + b_ref[0]


def _ada_mod(c_all, ada_w, ada_b):
    rows = c_all.shape[0]
    n = ada_w.shape[2]
    tn = 1536
    return pl.pallas_call(
        _ada_kernel,
        grid=(DEPTH, n // tn),
        in_specs=[pl.BlockSpec((rows, D_MODEL), lambda i, j: (0, 0)),
                  pl.BlockSpec((1, D_MODEL, tn), lambda i, j: (i, 0, j)),
                  pl.BlockSpec((1, 1, tn), lambda i, j: (i, 0, j))],
        out_specs=pl.BlockSpec((1, rows, tn), lambda i, j: (i, 0, j)),
        out_shape=jax.ShapeDtypeStruct((DEPTH, rows, n), F32),
        compiler_params=_cparams("arbitrary", "arbitrary"),
        name="ada_mod",
    )(c_all, ada_w, ada_b.reshape(DEPTH, 1, n))


def _causal_conv(u, ext_ref, halo_ref, col0, w_ref, width, nb, tt):
    n = u.shape[-1]
    hw = width - 1
    ext_ref[:, 8:8 + tt, :] = u.reshape(nb, tt, n)
    ext_ref[:, 8 - hw:8, :] = halo_ref[:, 8 - hw:8, col0:col0 + n]
    y = ext_ref[:, 8:8 + tt, :] * w_ref[hw:hw + 1, col0:col0 + n]
    for j in range(hw):
        y = y + ext_ref[:, 8 - hw + j:8 - hw + j + tt, :] * w_ref[j:j + 1, col0:col0 + n]
    halo_ref[:, 8 - hw:8, col0:col0 + n] = ext_ref[:, 8 + tt - hw:8 + tt, :]
    return y


FFN_COLS = 256


def _ffn_kernel(x_ref, mod_ref, ng_ref, win_ref, cw_ref, cb_ref, wout_ref, buf_ref,
                y_ref, nbuf_ref, extg_ref, extv_ref, halo_ref, *, nb, tt):
    m = nb * tt

    @pl.when(pl.program_id(1) == 0)
    def _():
        halo_ref[:, 6:8, :] = buf_ref[...]

    x = x_ref[...]
    h = _norm_mod(x, ng_ref[...], mod_ref[:, 4:5, :], mod_ref[:, 3:4, :])
    h = h.reshape(m, D_MODEL).astype(BF16)
    acc = jnp.zeros((m, D_MODEL), F32)
    for c in range(D_FF // FFN_COLS):
        g0 = c * FFN_COLS
        v0 = D_FF + c * FFN_COLS
        ug = jnp.dot(h, win_ref[:, g0:g0 + FFN_COLS], preferred_element_type=F32)
        uv = jnp.dot(h, win_ref[:, v0:v0 + FFN_COLS], preferred_element_type=F32)
        yg = _causal_conv(ug, extg_ref, halo_ref, g0, cw_ref, 3, nb, tt) + cb_ref[:, g0:g0 + FFN_COLS]
        yv = _causal_conv(uv, extv_ref, halo_ref, v0, cw_ref, 3, nb, tt) + cb_ref[:, v0:v0 + FFN_COLS]
        a = (_silu(yg) * yv).reshape(m, FFN_COLS).astype(BF16)
        acc = acc + jnp.dot(a, wout_ref[g0:g0 + FFN_COLS, :], preferred_element_type=F32)
    y_ref[...] = x + mod_ref[:, 5:6, :] * acc.reshape(nb, tt, D_MODEL)
    nbuf_ref[...] = halo_ref[:, 6:8, :]


def _ffn(x, mod, ng, win, cw, cb, wout, buf, nb, tt):
    b, t, d = x.shape
    n2 = 2 * D_FF
    kern = functools.partial(_ffn_kernel, nb=nb, tt=tt)
    return pl.pallas_call(
        kern,
        grid=(b // nb, t // tt),
        in_specs=[pl.BlockSpec((nb, tt, d), lambda i, j: (i, j, 0)),
                  pl.BlockSpec((nb, 8, d), lambda i, j: (i, 0, 0)),
                  pl.BlockSpec((1, d), lambda i, j: (0, 0)),
                  pl.BlockSpec((d, n2), lambda i, j: (0, 0)),
                  pl.BlockSpec((3, n2), lambda i, j: (0, 0)),
                  pl.BlockSpec((1, n2), lambda i, j: (0, 0)),
                  pl.BlockSpec((D_FF, d), lambda i, j: (0, 0)),
                  pl.BlockSpec((nb, 2, n2), lambda i, j: (i, 0, 0))],
        out_specs=[pl.BlockSpec((nb, tt, d), lambda i, j: (i, j, 0)),
                   pl.BlockSpec((nb, 2, n2), lambda i, j: (i, 0, 0))],
        out_shape=[jax.ShapeDtypeStruct((b, t, d), F32),
                   jax.ShapeDtypeStruct((b, 2, n2), F32)],
        scratch_shapes=[pltpu.VMEM((nb, 8 + tt, FFN_COLS), F32),
                        pltpu.VMEM((nb, 8 + tt, FFN_COLS), F32),
                        pltpu.VMEM((nb, 8, n2), F32)],
        compiler_params=_cparams("arbitrary", "arbitrary"),
        name="conv_ffn",
    )(x, mod, ng, win, cw, cb, wout, buf)


def _gdn_proj_kernel(x_ref, mod_ref, ng_ref, wqkv_ref, wz_ref, wab_ref, cw_ref, pv_ref, buf_ref,
                     q_ref, k_ref, v_ref, z_ref, gb_ref, nbuf_ref, ext_ref, halo_ref, *, nb, tt):
    m = nb * tt
    d = D_MODEL

    @pl.when(pl.program_id(1) == 0)
    def _():
        halo_ref[:, 5:8, :] = buf_ref[...]

    h = _norm_mod(x_ref[...], ng_ref[...], mod_ref[:, 1:2, :], mod_ref[:, 0:1, :])
    h = h.reshape(m, d).astype(BF16)
    z_ref[...] = jnp.dot(h, wz_ref[...], preferred_element_type=F32).reshape(nb, tt, d)
    ab = jnp.dot(h, wab_ref[...], preferred_element_type=F32)
    lane = _iota2(ab.shape, 1)
    gb = jnp.where(lane < GDN_HEADS,
                   pv_ref[0:1, :] * _softplus(ab + pv_ref[1:2, :]),
                   _sigmoid(ab))
    gb_ref[...] = gb.reshape(nb, tt, LANES)
    outs = (q_ref, k_ref, v_ref)
    for s in range(GDN_QKV // MXU_N):
        c0 = s * MXU_N
        u = jnp.dot(h, wqkv_ref[:, c0:c0 + MXU_N], preferred_element_type=F32)
        y = _silu(_causal_conv(u, ext_ref, halo_ref, c0, cw_ref, 4, nb, tt))
        which, o0 = divmod(c0, d)
        if which < 2:
            halves = []
            for hh in range(MXU_N // GDN_DK):
                yh = y[:, :, hh * GDN_DK:(hh + 1) * GDN_DK]
                r = lax.rsqrt(jnp.sum(yh * yh, axis=-1, keepdims=True) + NORM_EPS)
                if which == 0:
                    r = r * (GDN_DK ** -0.5)
                halves.append(yh * r)
            y = jnp.concatenate(halves, axis=-1)
        outs[which][:, :, o0:o0 + MXU_N] = y
    nbuf_ref[...] = halo_ref[:, 5:8, :]


def _gdn_proj(x, mod, ng, wqkv, wz, wab, cw, pv, buf, nb, tt):
    b, t, d = x.shape
    kern = functools.partial(_gdn_proj_kernel, nb=nb, tt=tt)
    row = lambda i, j: (i, j, 0)
    const2 = lambda i, j: (0, 0)
    act = jax.ShapeDtypeStruct((b, t, d), F32)
    return pl.pallas_call(
        kern,
        grid=(b // nb, t // tt),
        in_specs=[pl.BlockSpec((nb, tt, d), row),
                  pl.BlockSpec((nb, 8, d), lambda i, j: (i, 0, 0)),
                  pl.BlockSpec((1, d), const2),
                  pl.BlockSpec((d, GDN_QKV), const2),
                  pl.BlockSpec((d, d), const2),
                  pl.BlockSpec((d, LANES), const2),
                  pl.BlockSpec((4, GDN_QKV), const2),
                  pl.BlockSpec((8, LANES), const2),
                  pl.BlockSpec((nb, 3, GDN_QKV), lambda i, j: (i, 0, 0))],
        out_specs=[pl.BlockSpec((nb, tt, d), row)] * 4
                  + [pl.BlockSpec((nb, tt, LANES), row),
                     pl.BlockSpec((nb, 3, GDN_QKV), lambda i, j: (i, 0, 0))],
        out_shape=[act, act, act, act,
                   jax.ShapeDtypeStruct((b, t, LANES), F32),
                   jax.ShapeDtypeStruct((b, 3, GDN_QKV), F32)],
        scratch_shapes=[pltpu.VMEM((nb, 8 + tt, MXU_N), F32),
                        pltpu.VMEM((nb, 8, GDN_QKV), F32)],
        compiler_params=_cparams("arbitrary", "arbitrary"),
        name="gdn_proj",
    )(x, mod, ng, wqkv, wz, wab, cw, pv, buf)


def _tri_inverse(lmat, chunk):
    n = lmat.shape[0]
    row = _iota2((n, n), 0)
    col = _iota2((n, n), 1)
    level = 31 - lax.clz(jnp.bitwise_xor(row, col))
    xinv = (row == col).astype(F32) - jnp.where(level == 0, lmat, 0.0)
    k = 1
    while (1 << k) < chunk:
        coupling = jnp.where(level == k, lmat, 0.0)
        xinv = xinv - _dot3(_dot3(xinv, coupling), xinv)
        k += 1
    return xinv


def _gdn_core_kernel(q_ref, k_ref, v_ref, z_ref, gb_ref, x_ref, mod_ref, og_ref, wout_ref, s0_ref,
                     y_ref, sout_ref, s_ref, o_ref, *, tt, chunk):
    @pl.when(pl.program_id(1) == 0)
    def _():
        s_ref[...] = s0_ref[0]

    nck = tt // chunk
    row = _iota2((tt, tt), 0)
    col = _iota2((tt, tt), 1)
    if nck > 1:
        same = (row // chunk) == (col // chunk)
        incl = (row >= col) & same
        strict = (row > col) & same
    else:
        same = None
        incl = row >= col
        strict = row > col
    gb = gb_ref[0]
    gcol = _sel_dot(incl.astype(BF16), gb)
    if nck > 1:
        glast = _sel_dot(same.astype(BF16), gb)
    else:
        glast = jnp.broadcast_to(jnp.sum(gb, axis=0, keepdims=True), gb.shape)
    grow = gcol.T
    eg = jnp.exp(gcol)
    ekd = jnp.exp(glast - gcol)
    egl = jnp.exp(glast)

    for hd in range(GDN_HEADS):
        c0 = hd * GDN_DK
        qh = q_ref[0, :, c0:c0 + GDN_DK]
        kh = k_ref[0, :, c0:c0 + GDN_DK]
        vh = v_ref[0, :, c0:c0 + GDN_DK]
        beta = gb[:, GDN_HEADS + hd:GDN_HEADS + hd + 1]
        diff = gcol[:, hd:hd + 1] - grow[hd:hd + 1, :]
        decay = jnp.exp(jnp.where(incl, diff, -jnp.inf))
        kb = kh * beta
        lmat = jnp.where(strict, _dot_nt(kb, kh) * decay, 0.0)
        tinv = _tri_inverse(lmat, chunk)
        u = _dot(tinv, vh * beta)
        w = _dot(tinv, kb * eg[:, hd:hd + 1])
        qk = _dot_nt(qh, kh) * decay
        qd = qh * eg[:, hd:hd + 1]
        kd = kh * ekd[:, hd:hd + 1]
        s = s_ref[hd]
        vnew, ocross = [], []
        for c in range(nck):
            r0 = c * chunk
            vn = u[r0:r0 + chunk] - _dot(w[r0:r0 + chunk], s)
            ocross.append(_dot(qd[r0:r0 + chunk], s))
            s = s * egl[r0:r0 + 1, hd:hd + 1] + _dot_tn(kd[r0:r0 + chunk], vn)
            vnew.append(vn)
        s_ref[hd] = s
        vnew = jnp.concatenate(vnew, axis=0) if nck > 1 else vnew[0]
        ocross = jnp.concatenate(ocross, axis=0) if nck > 1 else ocross[0]
        o = ocross + _dot(qk, vnew)
        o = o * lax.rsqrt(jnp.mean(o * o, axis=-1, keepdims=True) + NORM_EPS) * og_ref[...]
        o_ref[:, c0:c0 + GDN_DK] = (o * _silu(z_ref[0, :, c0:c0 + GDN_DK])).astype(BF16)

    out = jnp.dot(o_ref[...], wout_ref[...], preferred_element_type=F32)
    y_ref[0] = x_ref[0] + mod_ref[0, 2:3, :] * out
    sout_ref[0] = s_ref[...]


def _gdn_core(q, k, v, z, gb, x, mod, og, wout, s0, tt, chunk):
    b, t, d = x.shape
    kern = functools.partial(_gdn_core_kernel, tt=tt, chunk=chunk)
    row = lambda i, j: (i, j, 0)
    return pl.pallas_call(
        kern,
        grid=(b, t // tt),
        in_specs=[pl.BlockSpec((1, tt, d), row)] * 4
                 + [pl.BlockSpec((1, tt, LANES), row),
                    pl.BlockSpec((1, tt, d), row),
                    pl.BlockSpec((1, 8, d), lambda i, j: (i, 0, 0)),
                    pl.BlockSpec((1, GDN_DK), lambda i, j: (0, 0)),
                    pl.BlockSpec((d, d), lambda i, j: (0, 0)),
                    pl.BlockSpec((1, GDN_HEADS, GDN_DK, GDN_DK), lambda i, j: (i, 0, 0, 0))],
        out_specs=[pl.BlockSpec((1, tt, d), row),
                   pl.BlockSpec((1, GDN_HEADS, GDN_DK, GDN_DK), lambda i, j: (i, 0, 0, 0))],
        out_shape=[jax.ShapeDtypeStruct((b, t, d), F32),
                   jax.ShapeDtypeStruct((b, GDN_HEADS, GDN_DK, GDN_DK), F32)],
        scratch_shapes=[pltpu.VMEM((GDN_HEADS, GDN_DK, GDN_DK), F32),
                        pltpu.VMEM((tt, d), BF16)],
        compiler_params=_cparams("arbitrary", "arbitrary"),
        name="gdn_core",
    )(q, k, v, z, gb, x, mod, og, wout, s0)


def _sb_proj_kernel(x_ref, mod_ref, ng_ref, w_ref, qkg_ref,
                    kf_ref, vf_ref, qb_ref, kb_ref, vb_ref, *, nb, tt):
    m = nb * tt
    d = D_MODEL
    h = _norm_mod(x_ref[...], ng_ref[...], mod_ref[:, 1:2, :], mod_ref[:, 0:1, :])
    h = h.reshape(m, d).astype(BF16)
    grp = ((_iota2((MXU_N, MXU_N), 0) // SB_DH) == (_iota2((MXU_N, MXU_N), 1) // SB_DH)).astype(BF16)
    for s in range(3 * d // MXU_N):
        c0 = s * MXU_N
        which, o0 = divmod(c0, d)
        u = jnp.dot(h, w_ref[:, c0:c0 + MXU_N], preferred_element_type=F32)
        if which < 2:
            ms = _dot_sel(u * u, grp) * (1.0 / SB_DH)
            u = u * lax.rsqrt(ms + NORM_EPS) * qkg_ref[which:which + 1, o0:o0 + MXU_N]
        u3 = u.reshape(nb, tt, MXU_N)
        if which == 0:
            qb_ref[:, :, o0:o0 + MXU_N] = u3.astype(BF16)
        elif which == 1:
            kf_ref[:, :, o0:o0 + MXU_N] = u3
            kb_ref[:, :, o0:o0 + MXU_N] = u3.astype(BF16)
        else:
            vf_ref[:, :, o0:o0 + MXU_N] = u3
            vb_ref[:, :, o0:o0 + MXU_N] = u3.astype(BF16)


def _sb_proj(x, mod, ng, w, qkg, nb, tt):
    b, t, d = x.shape
    kern = functools.partial(_sb_proj_kernel, nb=nb, tt=tt)
    row = lambda i, j: (i, j, 0)
    const2 = lambda i, j: (0, 0)
    f = jax.ShapeDtypeStruct((b, t, d), F32)
    h = jax.ShapeDtypeStruct((b, t, d), BF16)
    return pl.pallas_call(
        kern,
        grid=(b // nb, t // tt),
        in_specs=[pl.BlockSpec((nb, tt, d), row),
                  pl.BlockSpec((nb, 8, d), lambda i, j: (i, 0, 0)),
                  pl.BlockSpec((1, d), const2),
                  pl.BlockSpec((d, 3 * d), const2),
                  pl.BlockSpec((8, d), const2)],
        out_specs=[pl.BlockSpec((nb, tt, d), row)] * 5,
        out_shape=[f, f, h, h, h],
        compiler_params=_cparams("arbitrary", "arbitrary"),
        name="sb_proj",
    )(x, mod, ng, w, qkg)


def _sb_attn_kernel(*refs, tq, tk, past_len):
    if past_len:
        q_ref, k_ref, v_ref, pk_ref, pv_ref, o_ref = refs
    else:
        q_ref, k_ref, v_ref, o_ref = refs
        pk_ref = pv_ref = None
    qi = pl.program_id(2)
    scale = SB_DH ** -0.5
    q = q_ref[0]
    lane = _iota2((tq, LANES), 1)
    usel = {}
    for n in {tq, tk}:
        bw = min(LANES, n)
        usel[bw] = (_iota2((bw, bw), 0) > _iota2((bw, bw), 1)).astype(BF16)

    def tile(qh, kt, vt, carry, acc, mask):
        n = kt.shape[0]
        bw = min(LANES, n)
        z = _dot_nt_bf(qh, kt) * scale
        ln = -_softplus(z)
        lnm = ln if mask is None else jnp.where(mask, ln, 0.0)
        ws = []
        for mb in reversed(range(n // bw)):
            sl = slice(mb * bw, (mb + 1) * bw)
            after = carry + _dot_sel(lnm[:, sl], usel[bw])
            w = jnp.exp(z[:, sl] + ln[:, sl] + after)
            if mask is not None:
                w = jnp.where(mask[:, sl], w, 0.0)
            ws.append(w)
            carry = carry + jnp.sum(lnm[:, sl], axis=-1, keepdims=True)
        wts = jnp.concatenate(ws[::-1], axis=-1) if len(ws) > 1 else ws[0]
        acc = acc + jnp.dot(wts.astype(BF16), vt, preferred_element_type=F32)
        return carry, acc

    r0 = pl.multiple_of(qi * tq, tq)
    dmask = _iota2((tq, tq), 1) < _iota2((tq, tq), 0)
    outs = []
    for hd in range(2):
        qh = jnp.where((lane // SB_DH) == hd, q, jnp.zeros_like(q))
        carry = jnp.zeros((tq, 1), F32)
        acc = jnp.zeros((tq, LANES), F32)
        carry, acc = tile(qh, k_ref[0, pl.ds(r0, tq), :], v_ref[0, pl.ds(r0, tq), :], carry, acc, dmask)

        def body(i, ca, qh=qh):
            k0 = pl.multiple_of((qi - 1 - i) * tk, tk)
            return tile(qh, k_ref[0, pl.ds(k0, tk), :], v_ref[0, pl.ds(k0, tk), :], ca[0], ca[1], None)

        carry, acc = lax.fori_loop(0, qi, body, (carry, acc))
        if past_len:
            def pbody(i, ca, qh=qh):
                k0 = pl.multiple_of((past_len // tk - 1 - i) * tk, tk)
                return tile(qh, pk_ref[0, pl.ds(k0, tk), :].astype(BF16),
                            pv_ref[0, pl.ds(k0, tk), :].astype(BF16), ca[0], ca[1], None)

            carry, acc = lax.fori_loop(0, past_len // tk, pbody, (carry, acc))
        outs.append(acc)
    o_ref[0] = jnp.where((lane // SB_DH) == 0, outs[0], outs[1]).astype(BF16)


def _dot_nt_bf(a, b):
    return lax.dot_general(a, b, (((1,), (1,)), ((), ())), preferred_element_type=F32)


def _sb_attn(qb, kb, vb, past_k, past_v, tq):
    b, t, d = qb.shape
    past_len = 0 if past_k is None else past_k.shape[1]
    tk = tq if not past_len else min(256, past_len)
    kern = functools.partial(_sb_attn_kernel, tq=tq, tk=tk, past_len=past_len)
    qspec = pl.BlockSpec((1, tq, LANES), lambda i, h, j: (i, j, h))
    kvspec = pl.BlockSpec((1, t, LANES), lambda i, h, j: (i, 0, h))
    in_specs = [qspec, kvspec, kvspec]
    args = [qb, kb, vb]
    if past_len:
        pspec = pl.BlockSpec((1, past_len, LANES), lambda i, h, j: (i, 0, h))
        in_specs += [pspec, pspec]
        args += [past_k, past_v]
    return pl.pallas_call(
        kern,
        grid=(b, d // LANES, t // tq),
        in_specs=in_specs,
        out_specs=qspec,
        out_shape=jax.ShapeDtypeStruct((b, t, d), BF16),
        compiler_params=_cparams("arbitrary", "arbitrary", "arbitrary"),
        name="sb_attn",
    )(*args)


def _out_proj_kernel(o_ref, x_ref, mod_ref, w_ref, y_ref, *, nb, tt):
    out = jnp.dot(o_ref[...].reshape(nb * tt, -1), w_ref[...], preferred_element_type=F32)
    y_ref[...] = x_ref[...] + mod_ref[:, 2:3, :] * out.reshape(nb, tt, D_MODEL)


def _out_proj(o, x, mod, w, nb, tt):
    b, t, d = x.shape
    kin = o.shape[-1]
    kern = functools.partial(_out_proj_kernel, nb=nb, tt=tt)
    row = lambda i, j: (i, j, 0)
    return pl.pallas_call(
        kern,
        grid=(b // nb, t // tt),
        in_specs=[pl.BlockSpec((nb, tt, kin), row),
                  pl.BlockSpec((nb, tt, d), row),
                  pl.BlockSpec((nb, 8, d), lambda i, j: (i, 0, 0)),
                  pl.BlockSpec((kin, d), lambda i, j: (0, 0))],
        out_specs=pl.BlockSpec((nb, tt, d), row),
        out_shape=jax.ShapeDtypeStruct((b, t, d), F32),
        compiler_params=_cparams("arbitrary", "arbitrary"),
        name="out_proj",
    )(o, x, mod, w)


def _ret_log_gamma(hd):
    return math.log1p(-(2.0 ** (-5.0 - hd)))


def _ret_proj_kernel(x_ref, mod_ref, ng_ref, w_ref, cos_ref, sin_ref,
                     q_ref, qd_ref, k_ref, kd_ref, v_ref, gt_ref, *, nb, tt, chunk):
    m = nb * tt
    d = D_MODEL
    hk = RET_HEADS * RET_DK
    hv = RET_HEADS * RET_DV
    h = _norm_mod(x_ref[...], ng_ref[...], mod_ref[:, 1:2, :], mod_ref[:, 0:1, :])
    h = h.reshape(m, d).astype(BF16)
    cos = cos_ref[...][None]
    sin = sin_ref[...][None]
    tpos = pl.program_id(1) * tt + _iota2((1, tt, 1), 1)
    assert chunk & (chunk - 1) == 0
    idx = jnp.bitwise_and(tpos, chunk - 1).astype(F32)
    half = RET_DK // 2
    for s in range(2 * hk // MXU_N):
        c0 = s * MXU_N
        which, o0 = divmod(c0, hk)
        hd = o0 // RET_DK
        lg = _ret_log_gamma(hd)
        u = jnp.dot(h, w_ref[:, c0:c0 + MXU_N], preferred_element_type=F32).reshape(nb, tt, MXU_N)
        x1 = u[:, :, :half]
        x2 = u[:, :, half:]
        r = jnp.concatenate([x1 * cos - x2 * sin, x1 * sin + x2 * cos], axis=-1)
        if which == 0:
            q_ref[:, :, o0:o0 + MXU_N] = r.astype(BF16)
            qd_ref[:, :, o0:o0 + MXU_N] = (r * jnp.exp(lg * (idx + 1.0))).astype(BF16)
        else:
            r = r * (RET_DK ** -0.5)
            k_ref[:, :, o0:o0 + MXU_N] = r.astype(BF16)
            kd_ref[:, :, o0:o0 + MXU_N] = (r * jnp.exp(lg * (chunk - 1.0 - idx))).astype(BF16)
    for s in range(2 * hv // MXU_N):
        c0 = 2 * hk + s * MXU_N
        u = jnp.dot(h, w_ref[:, c0:c0 + MXU_N], preferred_element_type=F32).reshape(nb, tt, MXU_N)
        o0 = s * MXU_N
        if o0 < hv:
            v_ref[:, :, o0:o0 + MXU_N] = u.astype(BF16)
        else:
            gt_ref[:, :, o0 - hv:o0 - hv + MXU_N] = u


def _ret_proj(x, mod, ng, w, cos, sin, nb, tt, chunk):
    b, t, d = x.shape
    hk = RET_HEADS * RET_DK
    hv = RET_HEADS * RET_DV
    kern = functools.partial(_ret_proj_kernel, nb=nb, tt=tt, chunk=chunk)
    row = lambda i, j: (i, j, 0)
    const2 = lambda i, j: (0, 0)
    qs = jax.ShapeDtypeStruct((b, t, hk), BF16)
    return pl.pallas_call(
        kern,
        grid=(b // nb, t // tt),
        in_specs=[pl.BlockSpec((nb, tt, d), row),
                  pl.BlockSpec((nb, 8, d), lambda i, j: (i, 0, 0)),
                  pl.BlockSpec((1, d), const2),
                  pl.BlockSpec((d, 2 * hk + 2 * hv), const2),
                  pl.BlockSpec((tt, RET_DK // 2), lambda i, j: (j, 0)),
                  pl.BlockSpec((tt, RET_DK // 2), lambda i, j: (j, 0))],
        out_specs=[pl.BlockSpec((nb, tt, hk), row)] * 4
                  + [pl.BlockSpec((nb, tt, hv), row)] * 2,
        out_shape=[qs, qs, qs, qs,
                   jax.ShapeDtypeStruct((b, t, hv), BF16),
                   jax.ShapeDtypeStruct((b, t, hv), F32)],
        compiler_params=_cparams("arbitrary", "arbitrary"),
        name="ret_proj",
    )(x, mod, ng, w, cos, sin)


def _ret_core_kernel(q_ref, qd_ref, k_ref, kd_ref, v_ref, gt_ref, x_ref, mod_ref, og_ref, wout_ref, r0_ref,
                     y_ref, rout_ref, r_ref, o_ref, *, chunk):
    @pl.when(pl.program_id(1) == 0)
    def _():
        r_ref[...] = r0_ref[0]

    rel = (_iota2((chunk, chunk), 0) - _iota2((chunk, chunk), 1)).astype(F32)
    for hd in range(RET_HEADS):
        lg = _ret_log_gamma(hd)
        k0 = hd * RET_DK
        v0 = hd * RET_DV
        dmask = jnp.where(rel >= 0, jnp.exp(lg * jnp.maximum(rel, 0.0)), 0.0)
        vh = v_ref[0, :, v0:v0 + RET_DV]
        s = _dot_nt_bf(q_ref[0, :, k0:k0 + RET_DK], k_ref[0, :, k0:k0 + RET_DK]) * dmask
        r = r_ref[hd]
        o = (jnp.dot(s.astype(BF16), vh, preferred_element_type=F32)
             + jnp.dot(qd_ref[0, :, k0:k0 + RET_DK], r.astype(BF16), preferred_element_type=F32))
        r_ref[hd] = r * math.exp(lg * chunk) + lax.dot_general(
            kd_ref[0, :, k0:k0 + RET_DK], vh, (((0,), (0,)), ((), ())), preferred_element_type=F32)
        mu = jnp.mean(o, axis=-1, keepdims=True)
        dlt = o - mu
        var = jnp.mean(dlt * dlt, axis=-1, keepdims=True)
        o = dlt * lax.rsqrt(var + NORM_EPS) * og_ref[:, v0:v0 + RET_DV]
        o_ref[:, v0:v0 + RET_DV] = (o * _silu(gt_ref[0, :, v0:v0 + RET_DV])).astype(BF16)
    out = jnp.dot(o_ref[...], wout_ref[...], preferred_element_type=F32)
    y_ref[0] = x_ref[0] + mod_ref[0, 2:3, :] * out
    rout_ref[0] = r_ref[...]


def _ret_core(q, qd, k, kd, v, gt, x, mod, og, wout, r0, chunk):
    b, t, d = x.shape
    hk = RET_HEADS * RET_DK
    hv = RET_HEADS * RET_DV
    kern = functools.partial(_ret_core_kernel, chunk=chunk)
    row = lambda i, j: (i, j, 0)
    st = pl.BlockSpec((1, RET_HEADS, RET_DK, RET_DV), lambda i, j: (i, 0, 0, 0))
    return pl.pallas_call(
        kern,
        grid=(b, t // chunk),
        in_specs=[pl.BlockSpec((1, chunk, hk), row)] * 4
                 + [pl.BlockSpec((1, chunk, hv), row)] * 2
                 + [pl.BlockSpec((1, chunk, d), row),
                    pl.BlockSpec((1, 8, d), lambda i, j: (i, 0, 0)),
                    pl.BlockSpec((1, hv), lambda i, j: (0, 0)),
                    pl.BlockSpec((hv, d), lambda i, j: (0, 0)),
                    st],
        out_specs=[pl.BlockSpec((1, chunk, d), row), st],
        out_shape=[jax.ShapeDtypeStruct((b, t, d), F32),
                   jax.ShapeDtypeStruct((b, RET_HEADS, RET_DK, RET_DV), F32)],
        scratch_shapes=[pltpu.VMEM((RET_HEADS, RET_DK, RET_DV), F32),
                        pltpu.VMEM((chunk, hv), BF16)],
        compiler_params=_cparams("arbitrary", "arbitrary"),
        name="ret_core",
    )(q, qd, k, kd, v, gt, x, mod, og, wout, r0)


def _run_group(x, mod_all, states, ffn_bufs, pos0, wts):
    b, t, d = x.shape
    if t >= ROW_TILE:
        nb, tt = 1, ROW_TILE
    else:
        nb, tt = b, t
    new_states, new_ffn = [], []
    for i in range(DEPTH):
        kind, j = i % 3, i // 3
        mod = mod_all[i]
        if kind == 0:
            s0, cbuf = states[i]
            gw = wts["gdn"][j]
            q, k, v, z, gb, ncbuf = _gdn_proj(x, mod, wts["norm_mix_g"][i], gw["wqkv"], gw["wz"], gw["wab"],
                                              gw["conv_w"], gw["pv"], cbuf, nb, tt)
            x, s_new = _gdn_core(q, k, v, z, gb, x, mod, gw["norm_g"], gw["wout"], s0,
                                 min(GDN_TILE, t), min(CHUNK, t))
            new_states.append((s_new, ncbuf))
        elif kind == 1:
            pk, pv = states[i]
            sw = wts["sb"][j]
            kf, vf, qb, kb, vb = _sb_proj(x, mod, wts["norm_mix_g"][i], sw["win"], sw["qkg"], nb, tt)
            o = _sb_attn(qb, kb, vb, pk, pv, min(SB_TILE, t))
            x = _out_proj(o, x, mod, sw["wout"], nb, tt)
            new_states.append((kf.reshape(b, t, SB_HEADS, SB_DH), vf.reshape(b, t, SB_HEADS, SB_DH)))
        else:
            (r0,) = states[i]
            rw = wts["ret"][j]
            chunk = min(RET_CHUNK, t)
            half = RET_DK // 2
            inv_freq = RET_ROPE_BASE ** (-jnp.arange(half, dtype=F32) / half)
            ang = (pos0 + jnp.arange(t)).astype(F32)[:, None] * inv_freq[None, :]
            q, qd, k, kd, v, gt = _ret_proj(x, mod, wts["norm_mix_g"][i], rw["win"], jnp.cos(ang), jnp.sin(ang),
                                            nb, tt, chunk)
            x, r_new = _ret_core(q, qd, k, kd, v, gt, x, mod, rw["norm_g"], rw["wout"], r0, chunk)
            new_states.append((r_new,))
        fw = wts["ffn"][i]
        x, fbuf = _ffn(x, mod, wts["norm_ffn_g"][i], fw["win"], fw["conv_w"], fw["conv_b"], fw["wout"],
                       ffn_bufs[i], nb, tt)
        new_ffn.append(fbuf)
    return x, new_states, jnp.stack(new_ffn)


def kernel(x_prompt, x_sample, state_l0_gdn_S, state_l0_gdn_conv, cache_l1_sb_k, cache_l1_sb_v, state_l2_ret, state_l3_gdn_S, state_l3_gdn_conv, state_ffn_conv, c_prompt, c_sample, ada_w, ada_b, norm_mix_g, norm_ffn_g, gdn_w_in, gdn_conv_w, gdn_A_log, gdn_dt_bias, gdn_norm_g, gdn_w_out, sb_w_in, sb_q_norm_g, sb_k_norm_g, sb_w_out, ret_w_in, ret_norm_g, ret_w_out, ffn_w_in, ffn_conv_w, ffn_conv_b, ffn_w_out):
    d = D_MODEL
    bp, tp, _ = x_prompt.shape
    bs, ts, _ = x_sample.shape

    hv = GDN_HEADS * GDN_DK
    wts = {"norm_mix_g": [norm_mix_g[i].reshape(1, d) for i in range(DEPTH)],
           "norm_ffn_g": [norm_ffn_g[i].reshape(1, d) for i in range(DEPTH)],
           "gdn": [], "sb": [], "ret": [], "ffn": []}
    for j in range(gdn_w_in.shape[0]):
        w = gdn_w_in[j]
        wab = jnp.pad(w[:, GDN_QKV + hv:], ((0, 0), (0, LANES - 2 * GDN_HEADS)))
        pv = jnp.zeros((8, LANES), F32)
        pv = pv.at[0, :GDN_HEADS].set(-jnp.exp(gdn_A_log[j].astype(F32)))
        pv = pv.at[1, :GDN_HEADS].set(gdn_dt_bias[j].astype(F32))
        wts["gdn"].append({"wqkv": w[:, :GDN_QKV].astype(BF16),
                           "wz": w[:, GDN_QKV:GDN_QKV + hv].astype(BF16),
                           "wab": wab.astype(BF16),
                           "conv_w": gdn_conv_w[j], "pv": pv,
                           "norm_g": gdn_norm_g[j].reshape(1, GDN_DK),
                           "wout": gdn_w_out[j].astype(BF16)})
    for j in range(sb_w_in.shape[0]):
        qkg = jnp.zeros((8, d), F32)
        qkg = qkg.at[0].set(jnp.tile(sb_q_norm_g[j], SB_HEADS)).at[1].set(jnp.tile(sb_k_norm_g[j], SB_HEADS))
        wts["sb"].append({"win": sb_w_in[j].astype(BF16), "qkg": qkg, "wout": sb_w_out[j].astype(BF16)})
    for j in range(ret_w_in.shape[0]):
        wts["ret"].append({"win": ret_w_in[j].astype(BF16),
                           "norm_g": ret_norm_g[j].reshape(1, RET_HEADS * RET_DV),
                           "wout": ret_w_out[j].astype(BF16)})
    for i in range(DEPTH):
        wts["ffn"].append({"win": ffn_w_in[i].astype(BF16), "conv_w": ffn_conv_w[i],
                           "conv_b": ffn_conv_b[i].reshape(1, 2 * D_FF), "wout": ffn_w_out[i].astype(BF16)})

    nrow = bp + bs
    rows = -(-nrow // 8) * 8
    c_all = jnp.pad(jnp.concatenate([c_prompt, c_sample], axis=0), ((0, rows - nrow), (0, 0)))
    mod = _ada_mod(c_all, ada_w, ada_b)
    mod = jnp.pad(mod.reshape(DEPTH, rows, 6, d), ((0, 0), (0, 0), (0, 2), (0, 0)))
    mod_p, mod_s = mod[:, :bp], mod[:, bp:nrow]

    dt = x_prompt.dtype
    zero_states = [(jnp.zeros((bp, GDN_HEADS, GDN_DK, GDN_DK), dt), jnp.zeros((bp, 3, GDN_QKV), dt)),
                   (None, None),
                   (jnp.zeros((bp, RET_HEADS, RET_DK, RET_DV), dt),),
                   (jnp.zeros((bp, GDN_HEADS, GDN_DK, GDN_DK), dt), jnp.zeros((bp, 3, GDN_QKV), dt))]
    zero_ffn = jnp.zeros((DEPTH, bp, 2, 2 * D_FF), dt)
    y_prompt, p_states, p_ffn_conv = _run_group(x_prompt, mod_p, zero_states, zero_ffn, 0, wts)

    past_len = cache_l1_sb_k.shape[1]
    sample_states = [(state_l0_gdn_S, state_l0_gdn_conv),
                     (cache_l1_sb_k.reshape(bs, past_len, d), cache_l1_sb_v.reshape(bs, past_len, d)),
                     (state_l2_ret,), (state_l3_gdn_S, state_l3_gdn_conv)]
    y_sample, s_states, s_ffn_conv = _run_group(x_sample, mod_s, sample_states, state_ffn_conv, past_len, wts)

    (p_l0_S, p_l0_conv), (p_l1_k, p_l1_v), (p_l2_R,), (p_l3_S, p_l3_conv) = p_states
    (s_l0_S, s_l0_conv), (s_l1_k, s_l1_v), (s_l2_R,), (s_l3_S, s_l3_conv) = s_states
    return (y_prompt, y_sample,
            p_l0_S, p_l0_conv, p_l1_k, p_l1_v, p_l2_R, p_l3_S, p_l3_conv, p_ffn_conv,
            s_l0_S, s_l0_conv, s_l1_k, s_l1_v, s_l2_R, s_l3_S, s_l3_conv, s_ffn_conv)
```

```python
import functools
import math

import jax
import jax.numpy as jnp
from jax import lax
from jax.experimental import pallas as pl
from jax.experimental.pallas import tpu as pltpu

F32 = jnp.float32
BF16 = jnp.bfloat16

D_MODEL = 1024
DEPTH = 4
CHUNK = 64
GDN_HEADS = 8
GDN_DK = 128
GDN_QKV = 3 * GDN_HEADS * GDN_DK
SB_HEADS = 16
SB_DH = 64
RET_HEADS = 4
RET_DK = 256
RET_DV = 512
RET_ROPE_BASE = 10000.0
D_FF = 2816
NORM_EPS = 1e-6

LANES = 128
MXU_N = 256
VMEM_LIMIT = 56 * 1024 * 1024

ROW_TILE = 256
GDN_TILE = 256
RET_CHUNK = 256
SB_TILE = 256
SB_DEAD = 120.0


def _cparams(*sem):
    return pltpu.CompilerParams(dimension_semantics=sem, vmem_limit_bytes=VMEM_LIMIT)


def _dot(a, b):
    return jnp.dot(a.astype(BF16), b.astype(BF16), preferred_element_type=F32)


def _dot_nt(a, b):
    return lax.dot_general(a.astype(BF16), b.astype(BF16), (((1,), (1,)), ((), ())),
                           preferred_element_type=F32)


def _dot_tn(a, b):
    return lax.dot_general(a.astype(BF16), b.astype(BF16), (((0,), (0,)), ((), ())),
                           preferred_element_type=F32)


def _split(x):
    hi = x.astype(BF16)
    lo = (x - hi.astype(F32)).astype(BF16)
    return hi, lo


def _dot_sel(a, sel):
    hi, lo = _split(a)
    return (jnp.dot(hi, sel, preferred_element_type=F32)
            + jnp.dot(lo, sel, preferred_element_type=F32))


def _sel_dot(sel, b):
    hi, lo = _split(b)
    return (jnp.dot(sel, hi, preferred_element_type=F32)
            + jnp.dot(sel, lo, preferred_element_type=F32))


def _dot3(a, b):
    ah, al = _split(a)
    bh, bl = _split(b)
    return (jnp.dot(ah, bh, preferred_element_type=F32)
            + jnp.dot(ah, bl, preferred_element_type=F32)
            + jnp.dot(al, bh, preferred_element_type=F32))


def _sigmoid(x):
    return 1.0 / (1.0 + jnp.exp(-x))


def _silu(x):
    return x * _sigmoid(x)


def _softplus(x):
    return jnp.maximum(x, 0.0) + jnp.log(1.0 + jnp.exp(-jnp.abs(x)))


def _norm_mod(x, gain, scale, shift):
    ms = jnp.mean(x * x, axis=-1, keepdims=True)
    y = x * lax.rsqrt(ms + NORM_EPS) * gain
    return y * (1.0 + scale) + shift


def _iota2(shape, dim):
    return lax.broadcasted_iota(jnp.int32, shape, dim)


def _ada_kernel(c_ref, w_ref, b_ref, o_ref):
    o_ref[0] = _dot(_silu(c_ref[...]), w_ref[0]) + b_ref[0]


def _ada_mod(c_all, ada_w, ada_b):
    rows = c_all.shape[0]
    n = ada_w.shape[2]
    tn = 1536
    return pl.pallas_call(
        _ada_kernel,
        grid=(DEPTH, n // tn),
        in_specs=[pl.BlockSpec((rows, D_MODEL), lambda i, j: (0, 0)),
                  pl.BlockSpec((1, D_MODEL, tn), lambda i, j: (i, 0, j)),
                  pl.BlockSpec((1, 1, tn), lambda i, j: (i, 0, j))],
        out_specs=pl.BlockSpec((1, rows, tn), lambda i, j: (i, 0, j)),
        out_shape=jax.ShapeDtypeStruct((DEPTH, rows, n), F32),
        compiler_params=_cparams("arbitrary", "arbitrary"),
        name="ada_mod",
    )(c_all, ada_w, ada_b.reshape(DEPTH, 1, n))


def _causal_conv(u, ext_ref, halo_ref, col0, w_ref, width, nb, tt):
    n = u.shape[-1]
    hw = width - 1
    ext_ref[:, 8:8 + tt, :] = u.reshape(nb, tt, n)
    ext_ref[:, 8 - hw:8, :] = halo_ref[:, 8 - hw:8, col0:col0 + n]
    y = ext_ref[:, 8:8 + tt, :] * w_ref[hw:hw + 1, col0:col0 + n]
    for j in range(hw):
        y = y + ext_ref[:, 8 - hw + j:8 - hw + j + tt, :] * w_ref[j:j + 1, col0:col0 + n]
    halo_ref[:, 8 - hw:8, col0:col0 + n] = ext_ref[:, 8 + tt - hw:8 + tt, :]
    return y


FFN_COLS = 256


def _ffn_kernel(x_ref, mod_ref, ng_ref, win_ref, cw_ref, cb_ref, wout_ref, buf_ref,
                y_ref, nbuf_ref, extg_ref, extv_ref, halo_ref, *, nb, tt):
    m = nb * tt

    @pl.when(pl.program_id(1) == 0)
    def _():
        halo_ref[:, 6:8, :] = buf_ref[...]

    x = x_ref[...]
    h = _norm_mod(x, ng_ref[...], mod_ref[:, 4:5, :], mod_ref[:, 3:4, :])
    h = h.reshape(m, D_MODEL).astype(BF16)
    acc = jnp.zeros((m, D_MODEL), F32)
    for c in range(D_FF // FFN_COLS):
        g0 = c * FFN_COLS
        v0 = D_FF + c * FFN_COLS
        ug = jnp.dot(h, win_ref[:, g0:g0 + FFN_COLS], preferred_element_type=F32)
        uv = jnp.dot(h, win_ref[:, v0:v0 + FFN_COLS], preferred_element_type=F32)
        yg = _causal_conv(ug, extg_ref, halo_ref, g0, cw_ref, 3, nb, tt) + cb_ref[:, g0:g0 + FFN_COLS]
        yv = _causal_conv(uv, extv_ref, halo_ref, v0, cw_ref, 3, nb, tt) + cb_ref[:, v0:v0 + FFN_COLS]
        a = (_silu(yg) * yv).reshape(m, FFN_COLS).astype(BF16)
        acc = acc + jnp.dot(a, wout_ref[g0:g0 + FFN_COLS, :], preferred_element_type=F32)
    y_ref[...] = x + mod_ref[:, 5:6, :] * acc.reshape(nb, tt, D_MODEL)
    nbuf_ref[...] = halo_ref[:, 6:8, :]


def _ffn(x, mod, ng, win, cw, cb, wout, buf, nb, tt):
    b, t, d = x.shape
    n2 = 2 * D_FF
    kern = functools.partial(_ffn_kernel, nb=nb, tt=tt)
    return pl.pallas_call(
        kern,
        grid=(b // nb, t // tt),
        in_specs=[pl.BlockSpec((nb, tt, d), lambda i, j: (i, j, 0)),
                  pl.BlockSpec((nb, 8, d), lambda i, j: (i, 0, 0)),
                  pl.BlockSpec((1, d), lambda i, j: (0, 0)),
                  pl.BlockSpec((d, n2), lambda i, j: (0, 0)),
                  pl.BlockSpec((3, n2), lambda i, j: (0, 0)),
                  pl.BlockSpec((1, n2), lambda i, j: (0, 0)),
                  pl.BlockSpec((D_FF, d), lambda i, j: (0, 0)),
                  pl.BlockSpec((nb, 2, n2), lambda i, j: (i, 0, 0))],
        out_specs=[pl.BlockSpec((nb, tt, d), lambda i, j: (i, j, 0)),
                   pl.BlockSpec((nb, 2, n2), lambda i, j: (i, 0, 0))],
        out_shape=[jax.ShapeDtypeStruct((b, t, d), F32),
                   jax.ShapeDtypeStruct((b, 2, n2), F32)],
        scratch_shapes=[pltpu.VMEM((nb, 8 + tt, FFN_COLS), F32),
                        pltpu.VMEM((nb, 8 + tt, FFN_COLS), F32),
                        pltpu.VMEM((nb, 8, n2), F32)],
        compiler_params=_cparams("arbitrary", "arbitrary"),
        name="conv_ffn",
    )(x, mod, ng, win, cw, cb, wout, buf)


def _gdn_proj_kernel(x_ref, mod_ref, ng_ref, wqkv_ref, wz_ref, wab_ref, cw_ref, pv_ref, buf_ref,
                     q_ref, k_ref, v_ref, z_ref, gb_ref, nbuf_ref, ext_ref, halo_ref, *, nb, tt):
    m = nb * tt
    d = D_MODEL

    @pl.when(pl.program_id(1) == 0)
    def _():
        halo_ref[:, 5:8, :] = buf_ref[...]

    h = _norm_mod(x_ref[...], ng_ref[...], mod_ref[:, 1:2, :], mod_ref[:, 0:1, :])
    h = h.reshape(m, d).astype(BF16)
    z_ref[...] = jnp.dot(h, wz_ref[...], preferred_element_type=F32).reshape(nb, tt, d)
    ab = jnp.dot(h, wab_ref[...], preferred_element_type=F32)
    lane = _iota2(ab.shape, 1)
    gb = jnp.where(lane < GDN_HEADS,
                   pv_ref[0:1, :] * _softplus(ab + pv_ref[1:2, :]),
                   _sigmoid(ab))
    gb_ref[...] = gb.reshape(nb, tt, LANES)
    outs = (q_ref, k_ref, v_ref)
    for s in range(GDN_QKV // MXU_N):
        c0 = s * MXU_N
        u = jnp.dot(h, wqkv_ref[:, c0:c0 + MXU_N], preferred_element_type=F32)
        y = _silu(_causal_conv(u, ext_ref, halo_ref, c0, cw_ref, 4, nb, tt))
        which, o0 = divmod(c0, d)
        if which < 2:
            halves = []
            for hh in range(MXU_N // GDN_DK):
                yh = y[:, :, hh * GDN_DK:(hh + 1) * GDN_DK]
                r = lax.rsqrt(jnp.sum(yh * yh, axis=-1, keepdims=True) + NORM_EPS)
                if which == 0:
                    r = r * (GDN_DK ** -0.5)
                halves.append(yh * r)
            y = jnp.concatenate(halves, axis=-1)
        outs[which][:, :, o0:o0 + MXU_N] = y
    nbuf_ref[...] = halo_ref[:, 5:8, :]


def _gdn_proj(x, mod, ng, wqkv, wz, wab, cw, pv, buf, nb, tt):
    b, t, d = x.shape
    kern = functools.partial(_gdn_proj_kernel, nb=nb, tt=tt)
    row = lambda i, j: (i, j, 0)
    const2 = lambda i, j: (0, 0)
    act = jax.ShapeDtypeStruct((b, t, d), F32)
    return pl.pallas_call(
        kern,
        grid=(b // nb, t // tt),
        in_specs=[pl.BlockSpec((nb, tt, d), row),
                  pl.BlockSpec((nb, 8, d), lambda i, j: (i, 0, 0)),
                  pl.BlockSpec((1, d), const2),
                  pl.BlockSpec((d, GDN_QKV), const2),
                  pl.BlockSpec((d, d), const2),
                  pl.BlockSpec((d, LANES), const2),
                  pl.BlockSpec((4, GDN_QKV), const2),
                  pl.BlockSpec((8, LANES), const2),
                  pl.BlockSpec((nb, 3, GDN_QKV), lambda i, j: (i, 0, 0))],
        out_specs=[pl.BlockSpec((nb, tt, d), row)] * 4
                  + [pl.BlockSpec((nb, tt, LANES), row),
                     pl.BlockSpec((nb, 3, GDN_QKV), lambda i, j: (i, 0, 0))],
        out_shape=[act, act, act, act,
                   jax.ShapeDtypeStruct((b, t, LANES), F32),
                   jax.ShapeDtypeStruct((b, 3, GDN_QKV), F32)],
        scratch_shapes=[pltpu.VMEM((nb, 8 + tt, MXU_N), F32),
                        pltpu.VMEM((nb, 8, GDN_QKV), F32)],
        compiler_params=_cparams("arbitrary", "arbitrary"),
        name="gdn_proj",
    )(x, mod, ng, wqkv, wz, wab, cw, pv, buf)


def _tri_inverse(lmat, chunk):
    n = lmat.shape[0]
    row = _iota2((n, n), 0)
    col = _iota2((n, n), 1)
    level = 31 - lax.clz(jnp.bitwise_xor(row, col))
    xinv = (row == col).astype(F32) - jnp.where(level == 0, lmat, 0.0)
    k = 1
    while (1 << k) < chunk:
        coupling = jnp.where(level == k, lmat, 0.0)
        xinv = xinv - _dot3(_dot3(xinv, coupling), xinv)
        k += 1
    return xinv


def _gdn_core_kernel(q_ref, k_ref, v_ref, z_ref, gb_ref, x_ref, mod_ref, og_ref, wout_ref, s0_ref,
                     y_ref, sout_ref, s_ref, o_ref, *, tt, chunk):
    @pl.when(pl.program_id(1) == 0)
    def _():
        s_ref[...] = s0_ref[0]

    nck = tt // chunk
    row = _iota2((tt, tt), 0)
    col = _iota2((tt, tt), 1)
    if nck > 1:
        same = (row // chunk) == (col // chunk)
        incl = (row >= col) & same
        strict = (row > col) & same
    else:
        same = None
        incl = row >= col
        strict = row > col
    gb = gb_ref[0]
    gcol = _sel_dot(incl.astype(BF16), gb)
    if nck > 1:
        glast = _sel_dot(same.astype(BF16), gb)
    else:
        glast = jnp.broadcast_to(jnp.sum(gb, axis=0, keepdims=True), gb.shape)
    grow = gcol.T
    eg = jnp.exp(gcol)
    ekd = jnp.exp(glast - gcol)
    egl = jnp.exp(glast)

    for hd in range(GDN_HEADS):
        c0 = hd * GDN_DK
        qh = q_ref[0, :, c0:c0 + GDN_DK]
        kh = k_ref[0, :, c0:c0 + GDN_DK]
        vh = v_ref[0, :, c0:c0 + GDN_DK]
        beta = gb[:, GDN_HEADS + hd:GDN_HEADS + hd + 1]
        diff = gcol[:, hd:hd + 1] - grow[hd:hd + 1, :]
        decay = jnp.exp(jnp.where(incl, diff, -jnp.inf))
        kb = kh * beta
        lmat = jnp.where(strict, _dot_nt(kb, kh) * decay, 0.0)
        tinv = _tri_inverse(lmat, chunk)
        u = _dot(tinv, vh * beta)
        w = _dot(tinv, kb * eg[:, hd:hd + 1])
        qk = _dot_nt(qh, kh) * decay
        qd = qh * eg[:, hd:hd + 1]
        kd = kh * ekd[:, hd:hd + 1]
        s = s_ref[hd]
        vnew, ocross = [], []
        for c in range(nck):
            r0 = c * chunk
            vn = u[r0:r0 + chunk] - _dot(w[r0:r0 + chunk], s)
            ocross.append(_dot(qd[r0:r0 + chunk], s))
            s = s * egl[r0:r0 + 1, hd:hd + 1] + _dot_tn(kd[r0:r0 + chunk], vn)
            vnew.append(vn)
        s_ref[hd] = s
        vnew = jnp.concatenate(vnew, axis=0) if nck > 1 else vnew[0]
        ocross = jnp.concatenate(ocross, axis=0) if nck > 1 else ocross[0]
        o = ocross + _dot(qk, vnew)
        o = o * lax.rsqrt(jnp.mean(o * o, axis=-1, keepdims=True) + NORM_EPS) * og_ref[...]
        o_ref[:, c0:c0 + GDN_DK] = (o * _silu(z_ref[0, :, c0:c0 + GDN_DK])).astype(BF16)

    out = jnp.dot(o_ref[...], wout_ref[...], preferred_element_type=F32)
    y_ref[0] = x_ref[0] + mod_ref[0, 2:3, :] * out
    sout_ref[0] = s_ref[...]


def _gdn_core(q, k, v, z, gb, x, mod, og, wout, s0, tt, chunk):
    b, t, d = x.shape
    kern = functools.partial(_gdn_core_kernel, tt=tt, chunk=chunk)
    row = lambda i, j: (i, j, 0)
    return pl.pallas_call(
        kern,
        grid=(b, t // tt),
        in_specs=[pl.BlockSpec((1, tt, d), row)] * 4
                 + [pl.BlockSpec((1, tt, LANES), row),
                    pl.BlockSpec((1, tt, d), row),
                    pl.BlockSpec((1, 8, d), lambda i, j: (i, 0, 0)),
                    pl.BlockSpec((1, GDN_DK), lambda i, j: (0, 0)),
                    pl.BlockSpec((d, d), lambda i, j: (0, 0)),
                    pl.BlockSpec((1, GDN_HEADS, GDN_DK, GDN_DK), lambda i, j: (i, 0, 0, 0))],
        out_specs=[pl.BlockSpec((1, tt, d), row),
                   pl.BlockSpec((1, GDN_HEADS, GDN_DK, GDN_DK), lambda i, j: (i, 0, 0, 0))],
        out_shape=[jax.ShapeDtypeStruct((b, t, d), F32),
                   jax.ShapeDtypeStruct((b, GDN_HEADS, GDN_DK, GDN_DK), F32)],
        scratch_shapes=[pltpu.VMEM((GDN_HEADS, GDN_DK, GDN_DK), F32),
                        pltpu.VMEM((tt, d), BF16)],
        compiler_params=_cparams("arbitrary", "arbitrary"),
        name="gdn_core",
    )(q, k, v, z, gb, x, mod, og, wout, s0)


def _sb_proj_kernel(x_ref, mod_ref, ng_ref, w_ref, qkg_ref,
                    kf_ref, vf_ref, qb_ref, kb_ref, vb_ref, *, nb, tt):
    m = nb * tt
    d = D_MODEL
    h = _norm_mod(x_ref[...], ng_ref[...], mod_ref[:, 1:2, :], mod_ref[:, 0:1, :])
    h = h.reshape(m, d).astype(BF16)
    grp = ((_iota2((MXU_N, MXU_N), 0) // SB_DH) == (_iota2((MXU_N, MXU_N), 1) // SB_DH)).astype(BF16)
    for s in range(3 * d // MXU_N):
        c0 = s * MXU_N
        which, o0 = divmod(c0, d)
        u = jnp.dot(h, w_ref[:, c0:c0 + MXU_N], preferred_element_type=F32)
        if which < 2:
            ms = _dot_sel(u * u, grp) * (1.0 / SB_DH)
            u = u * lax.rsqrt(ms + NORM_EPS) * qkg_ref[which:which + 1, o0:o0 + MXU_N]
        u3 = u.reshape(nb, tt, MXU_N)
        if which == 0:
            qb_ref[:, :, o0:o0 + MXU_N] = u3.astype(BF16)
        elif which == 1:
            kf_ref[:, :, o0:o0 + MXU_N] = u3
            kb_ref[:, :, o0:o0 + MXU_N] = u3.astype(BF16)
        else:
            vf_ref[:, :, o0:o0 + MXU_N] = u3
            vb_ref[:, :, o0:o0 + MXU_N] = u3.astype(BF16)


def _sb_proj(x, mod, ng, w, qkg, nb, tt):
    b, t, d = x.shape
    kern = functools.partial(_sb_proj_kernel, nb=nb, tt=tt)
    row = lambda i, j: (i, j, 0)
    const2 = lambda i, j: (0, 0)
    f = jax.ShapeDtypeStruct((b, t, d), F32)
    h = jax.ShapeDtypeStruct((b, t, d), BF16)
    return pl.pallas_call(
        kern,
        grid=(b // nb, t // tt),
        in_specs=[pl.BlockSpec((nb, tt, d), row),
                  pl.BlockSpec((nb, 8, d), lambda i, j: (i, 0, 0)),
                  pl.BlockSpec((1, d), const2),
                  pl.BlockSpec((d, 3 * d), const2),
                  pl.BlockSpec((8, d), const2)],
        out_specs=[pl.BlockSpec((nb, tt, d), row)] * 5,
        out_shape=[f, f, h, h, h],
        compiler_params=_cparams("arbitrary", "arbitrary"),
        name="sb_proj",
    )(x, mod, ng, w, qkg)


def _suffix_selector(bw):
    r = jnp.bitwise_and(_iota2((2 * bw, bw + LANES), 0), bw - 1)
    c = _iota2((2 * bw, bw + LANES), 1)
    return jnp.where(c < bw, (r > c).astype(F32), 1.0).astype(BF16)


def _sb_attn_kernel(*refs, tq, tk, past_len):
    if past_len:
        q_ref, k_ref, v_ref, pk_ref, pv_ref, o_ref = refs
    else:
        q_ref, k_ref, v_ref, o_ref = refs
        pk_ref = pv_ref = None
    qi = pl.program_id(2)
    q = q_ref[0] * (SB_DH ** -0.5)
    lane = _iota2((tq, LANES), 1)
    qhs = [jnp.where((lane // SB_DH) == hd, q, jnp.zeros_like(q)) for hd in range(2)]
    sel = {bw: _suffix_selector(bw) for bw in {min(LANES, tq), min(LANES, tk)}}

    def tile(qh, kt, vt, carry, acc, mask):
        n = kt.shape[0]
        bw = min(LANES, n)
        z = _dot_nt_bf(qh, kt)
        sp = jnp.maximum(z, 0.0) + jnp.log(1.0 + jnp.exp(-jnp.abs(z)))
        spm = sp if mask is None else jnp.where(mask, sp, 0.0)
        ws = []
        for mb in reversed(range(n // bw)):
            sl = slice(mb * bw, (mb + 1) * bw)
            hi = spm[:, sl].astype(BF16)
            lo = (spm[:, sl] - hi.astype(F32)).astype(BF16)
            r = jnp.dot(jnp.concatenate([hi, lo], axis=1), sel[bw], preferred_element_type=F32)
            w = jnp.exp(z[:, sl] - sp[:, sl] - carry[:, :bw] - r[:, :bw])
            if mask is not None:
                w = jnp.where(mask[:, sl], w, 0.0)
            ws.append(w.astype(BF16))
            carry = carry + r[:, bw:]
        wts = jnp.concatenate(ws[::-1], axis=-1) if len(ws) > 1 else ws[0]
        acc = acc + jnp.dot(wts, vt, preferred_element_type=F32)
        return carry, acc

    def both(kt, vt, state, mask):
        c0, a0 = tile(qhs[0], kt, vt, state[0], state[1], mask)
        c1, a1 = tile(qhs[1], kt, vt, state[2], state[3], mask)
        return c0, a0, c1, a1

    zero = jnp.zeros((tq, LANES), F32)
    r0 = pl.multiple_of(qi * tq, tq)
    dmask = _iota2((tq, tq), 1) < _iota2((tq, tq), 0)
    state = both(k_ref[0, pl.ds(r0, tq), :], v_ref[0, pl.ds(r0, tq), :], (zero, zero, zero, zero), dmask)

    def alive(st):
        return jnp.min(jnp.minimum(st[0], st[2])) < SB_DEAD

    def key_loop(ntiles, load, st):
        def cond(c):
            return jnp.logical_and(c[0] < ntiles, c[1])

        def body(c):
            kt, vt = load(pl.multiple_of((ntiles - 1 - c[0]) * tk, tk))
            nst = both(kt, vt, c[2:], None)
            return (c[0] + 1, alive(nst)) + nst

        return lax.while_loop(cond, body, (jnp.int32(0), alive(st)) + st)[2:]

    state = key_loop(qi, lambda k0: (k_ref[0, pl.ds(k0, tk), :], v_ref[0, pl.ds(k0, tk), :]), state)
    if past_len:
        state = key_loop(past_len // tk,
                         lambda k0: (pk_ref[0, pl.ds(k0, tk), :].astype(BF16),
                                     pv_ref[0, pl.ds(k0, tk), :].astype(BF16)), state)
    o_ref[0] = jnp.where((lane // SB_DH) == 0, state[1], state[3]).astype(BF16)


def _dot_nt_bf(a, b):
    return lax.dot_general(a, b, (((1,), (1,)), ((), ())), preferred_element_type=F32)


def _sb_attn(qb, kb, vb, past_k, past_v, tq):
    b, t, d = qb.shape
    past_len = 0 if past_k is None else past_k.shape[1]
    tk = tq if not past_len else min(256, past_len)
    kern = functools.partial(_sb_attn_kernel, tq=tq, tk=tk, past_len=past_len)
    qspec = pl.BlockSpec((1, tq, LANES), lambda i, h, j: (i, j, h))
    kvspec = pl.BlockSpec((1, t, LANES), lambda i, h, j: (i, 0, h))
    in_specs = [qspec, kvspec, kvspec]
    args = [qb, kb, vb]
    if past_len:
        pspec = pl.BlockSpec((1, past_len, LANES), lambda i, h, j: (i, 0, h))
        in_specs += [pspec, pspec]
        args += [past_k, past_v]
    return pl.pallas_call(
        kern,
        grid=(b, d // LANES, t // tq),
        in_specs=in_specs,
        out_specs=qspec,
        out_shape=jax.ShapeDtypeStruct((b, t, d), BF16),
        compiler_params=_cparams("arbitrary", "arbitrary", "arbitrary"),
        name="sb_attn",
    )(*args)


def _out_proj_kernel(o_ref, x_ref, mod_ref, w_ref, y_ref, *, nb, tt):
    out = jnp.dot(o_ref[...].reshape(nb * tt, -1), w_ref[...], preferred_element_type=F32)
    y_ref[...] = x_ref[...] + mod_ref[:, 2:3, :] * out.reshape(nb, tt, D_MODEL)


def _out_proj(o, x, mod, w, nb, tt):
    b, t, d = x.shape
    kin = o.shape[-1]
    kern = functools.partial(_out_proj_kernel, nb=nb, tt=tt)
    row = lambda i, j: (i, j, 0)
    return pl.pallas_call(
        kern,
        grid=(b // nb, t // tt),
        in_specs=[pl.BlockSpec((nb, tt, kin), row),
                  pl.BlockSpec((nb, tt, d), row),
                  pl.BlockSpec((nb, 8, d), lambda i, j: (i, 0, 0)),
                  pl.BlockSpec((kin, d), lambda i, j: (0, 0))],
        out_specs=pl.BlockSpec((nb, tt, d), row),
        out_shape=jax.ShapeDtypeStruct((b, t, d), F32),
        compiler_params=_cparams("arbitrary", "arbitrary"),
        name="out_proj",
    )(o, x, mod, w)


def _ret_log_gamma(hd):
    return math.log1p(-(2.0 ** (-5.0 - hd)))


def _ret_proj_kernel(x_ref, mod_ref, ng_ref, w_ref, cos_ref, sin_ref,
                     q_ref, qd_ref, k_ref, kd_ref, v_ref, gt_ref, *, nb, tt, chunk):
    m = nb * tt
    d = D_MODEL
    hk = RET_HEADS * RET_DK
    hv = RET_HEADS * RET_DV
    h = _norm_mod(x_ref[...], ng_ref[...], mod_ref[:, 1:2, :], mod_ref[:, 0:1, :])
    h = h.reshape(m, d).astype(BF16)
    cos = cos_ref[...][None]
    sin = sin_ref[...][None]
    tpos = pl.program_id(1) * tt + _iota2((1, tt, 1), 1)
    assert chunk & (chunk - 1) == 0
    idx = jnp.bitwise_and(tpos, chunk - 1).astype(F32)
    half = RET_DK // 2
    for s in range(2 * hk // MXU_N):
        c0 = s * MXU_N
        which, o0 = divmod(c0, hk)
        hd = o0 // RET_DK
        lg = _ret_log_gamma(hd)
        u = jnp.dot(h, w_ref[:, c0:c0 + MXU_N], preferred_element_type=F32).reshape(nb, tt, MXU_N)
        x1 = u[:, :, :half]
        x2 = u[:, :, half:]
        r = jnp.concatenate([x1 * cos - x2 * sin, x1 * sin + x2 * cos], axis=-1)
        if which == 0:
            q_ref[:, :, o0:o0 + MXU_N] = r.astype(BF16)
            qd_ref[:, :, o0:o0 + MXU_N] = (r * jnp.exp(lg * (idx + 1.0))).astype(BF16)
        else:
            r = r * (RET_DK ** -0.5)
            k_ref[:, :, o0:o0 + MXU_N] = r.astype(BF16)
            kd_ref[:, :, o0:o0 + MXU_N] = (r * jnp.exp(lg * (chunk - 1.0 - idx))).astype(BF16)
    for s in range(2 * hv // MXU_N):
        c0 = 2 * hk + s * MXU_N
        u = jnp.dot(h, w_ref[:, c0:c0 + MXU_N], preferred_element_type=F32).reshape(nb, tt, MXU_N)
        o0 = s * MXU_N
        if o0 < hv:
            v_ref[:, :, o0:o0 + MXU_N] = u.astype(BF16)
        else:
            gt_ref[:, :, o0 - hv:o0 - hv + MXU_N] = u


def _ret_proj(x, mod, ng, w, cos, sin, nb, tt, chunk):
    b, t, d = x.shape
    hk = RET_HEADS * RET_DK
    hv = RET_HEADS * RET_DV
    kern = functools.partial(_ret_proj_kernel, nb=nb, tt=tt, chunk=chunk)
    row = lambda i, j: (i, j, 0)
    const2 = lambda i, j: (0, 0)
    qs = jax.ShapeDtypeStruct((b, t, hk), BF16)
    return pl.pallas_call(
        kern,
        grid=(b // nb, t // tt),
        in_specs=[pl.BlockSpec((nb, tt, d), row),
                  pl.BlockSpec((nb, 8, d), lambda i, j: (i, 0, 0)),
                  pl.BlockSpec((1, d), const2),
                  pl.BlockSpec((d, 2 * hk + 2 * hv), const2),
                  pl.BlockSpec((tt, RET_DK // 2), lambda i, j: (j, 0)),
                  pl.BlockSpec((tt, RET_DK // 2), lambda i, j: (j, 0))],
        out_specs=[pl.BlockSpec((nb, tt, hk), row)] * 4
                  + [pl.BlockSpec((nb, tt, hv), row)] * 2,
        out_shape=[qs, qs, qs, qs,
                   jax.ShapeDtypeStruct((b, t, hv), BF16),
                   jax.ShapeDtypeStruct((b, t, hv), F32)],
        compiler_params=_cparams("arbitrary", "arbitrary"),
        name="ret_proj",
    )(x, mod, ng, w, cos, sin)


def _ret_core_kernel(q_ref, qd_ref, k_ref, kd_ref, v_ref, gt_ref, x_ref, mod_ref, og_ref, wout_ref, r0_ref,
                     y_ref, rout_ref, r_ref, o_ref, *, chunk):
    @pl.when(pl.program_id(1) == 0)
    def _():
        r_ref[...] = r0_ref[0]

    rel = (_iota2((chunk, chunk), 0) - _iota2((chunk, chunk), 1)).astype(F32)
    for hd in range(RET_HEADS):
        lg = _ret_log_gamma(hd)
        k0 = hd * RET_DK
        v0 = hd * RET_DV
        dmask = jnp.where(rel >= 0, jnp.exp(lg * jnp.maximum(rel, 0.0)), 0.0)
        vh = v_ref[0, :, v0:v0 + RET_DV]
        s = _dot_nt_bf(q_ref[0, :, k0:k0 + RET_DK], k_ref[0, :, k0:k0 + RET_DK]) * dmask
        r = r_ref[hd]
        o = (jnp.dot(s.astype(BF16), vh, preferred_element_type=F32)
             + jnp.dot(qd_ref[0, :, k0:k0 + RET_DK], r.astype(BF16), preferred_element_type=F32))
        r_ref[hd] = r * math.exp(lg * chunk) + lax.dot_general(
            kd_ref[0, :, k0:k0 + RET_DK], vh, (((0,), (0,)), ((), ())), preferred_element_type=F32)
        mu = jnp.mean(o, axis=-1, keepdims=True)
        dlt = o - mu
        var = jnp.mean(dlt * dlt, axis=-1, keepdims=True)
        o = dlt * lax.rsqrt(var + NORM_EPS) * og_ref[:, v0:v0 + RET_DV]
        o_ref[:, v0:v0 + RET_DV] = (o * _silu(gt_ref[0, :, v0:v0 + RET_DV])).astype(BF16)
    out = jnp.dot(o_ref[...], wout_ref[...], preferred_element_type=F32)
    y_ref[0] = x_ref[0] + mod_ref[0, 2:3, :] * out
    rout_ref[0] = r_ref[...]


def _ret_core(q, qd, k, kd, v, gt, x, mod, og, wout, r0, chunk):
    b, t, d = x.shape
    hk = RET_HEADS * RET_DK
    hv = RET_HEADS * RET_DV
    kern = functools.partial(_ret_core_kernel, chunk=chunk)
    row = lambda i, j: (i, j, 0)
    st = pl.BlockSpec((1, RET_HEADS, RET_DK, RET_DV), lambda i, j: (i, 0, 0, 0))
    return pl.pallas_call(
        kern,
        grid=(b, t // chunk),
        in_specs=[pl.BlockSpec((1, chunk, hk), row)] * 4
                 + [pl.BlockSpec((1, chunk, hv), row)] * 2
                 + [pl.BlockSpec((1, chunk, d), row),
                    pl.BlockSpec((1, 8, d), lambda i, j: (i, 0, 0)),
                    pl.BlockSpec((1, hv), lambda i, j: (0, 0)),
                    pl.BlockSpec((hv, d), lambda i, j: (0, 0)),
                    st],
        out_specs=[pl.BlockSpec((1, chunk, d), row), st],
        out_shape=[jax.ShapeDtypeStruct((b, t, d), F32),
                   jax.ShapeDtypeStruct((b, RET_HEADS, RET_DK, RET_DV), F32)],
        scratch_shapes=[pltpu.VMEM((RET_HEADS, RET_DK, RET_DV), F32),
                        pltpu.VMEM((chunk, hv), BF16)],
        compiler_params=_cparams("arbitrary", "arbitrary"),
        name="ret_core",
    )(q, qd, k, kd, v, gt, x, mod, og, wout, r0)


def _run_group(x, mod_all, states, ffn_bufs, pos0, wts):
    b, t, d = x.shape
    if t >= ROW_TILE:
        nb, tt = 1, ROW_TILE
    else:
        nb, tt = b, t
    new_states, new_ffn = [], []
    for i in range(DEPTH):
        kind, j = i % 3, i // 3
        mod = mod_all[i]
        if kind == 0:
            s0, cbuf = states[i]
            gw = wts["gdn"][j]
            q, k, v, z, gb, ncbuf = _gdn_proj(x, mod, wts["norm_mix_g"][i], gw["wqkv"], gw["wz"], gw["wab"],
                                              gw["conv_w"], gw["pv"], cbuf, nb, tt)
            x, s_new = _gdn_core(q, k, v, z, gb, x, mod, gw["norm_g"], gw["wout"], s0,
                                 min(GDN_TILE, t), min(CHUNK, t))
            new_states.append((s_new, ncbuf))
        elif kind == 1:
            pk, pv = states[i]
            sw = wts["sb"][j]
            kf, vf, qb, kb, vb = _sb_proj(x, mod, wts["norm_mix_g"][i], sw["win"], sw["qkg"], nb, tt)
            o = _sb_attn(qb, kb, vb, pk, pv, min(SB_TILE, t))
            x = _out_proj(o, x, mod, sw["wout"], nb, tt)
            new_states.append((kf.reshape(b, t, SB_HEADS, SB_DH), vf.reshape(b, t, SB_HEADS, SB_DH)))
        else:
            (r0,) = states[i]
            rw = wts["ret"][j]
            chunk = min(RET_CHUNK, t)
            half = RET_DK // 2
            inv_freq = RET_ROPE_BASE ** (-jnp.arange(half, dtype=F32) / half)
            ang = (pos0 + jnp.arange(t)).astype(F32)[:, None] * inv_freq[None, :]
            q, qd, k, kd, v, gt = _ret_proj(x, mod, wts["norm_mix_g"][i], rw["win"], jnp.cos(ang), jnp.sin(ang),
                                            nb, tt, chunk)
            x, r_new = _ret_core(q, qd, k, kd, v, gt, x, mod, rw["norm_g"], rw["wout"], r0, chunk)
            new_states.append((r_new,))
        fw = wts["ffn"][i]
        x, fbuf = _ffn(x, mod, wts["norm_ffn_g"][i], fw["win"], fw["conv_w"], fw["conv_b"], fw["wout"],
                       ffn_bufs[i], nb, tt)
        new_ffn.append(fbuf)
    return x, new_states, jnp.stack(new_ffn)


def kernel(x_prompt, x_sample, state_l0_gdn_S, state_l0_gdn_conv, cache_l1_sb_k, cache_l1_sb_v, state_l2_ret, state_l3_gdn_S, state_l3_gdn_conv, state_ffn_conv, c_prompt, c_sample, ada_w, ada_b, norm_mix_g, norm_ffn_g, gdn_w_in, gdn_conv_w, gdn_A_log, gdn_dt_bias, gdn_norm_g, gdn_w_out, sb_w_in, sb_q_norm_g, sb_k_norm_g, sb_w_out, ret_w_in, ret_norm_g, ret_w_out, ffn_w_in, ffn_conv_w, ffn_conv_b, ffn_w_out):
    d = D_MODEL
    bp, tp, _ = x_prompt.shape
    bs, ts, _ = x_sample.shape

    hv = GDN_HEADS * GDN_DK
    wts = {"norm_mix_g": [norm_mix_g[i].reshape(1, d) for i in range(DEPTH)],
           "norm_ffn_g": [norm_ffn_g[i].reshape(1, d) for i in range(DEPTH)],
           "gdn": [], "sb": [], "ret": [], "ffn": []}
    for j in range(gdn_w_in.shape[0]):
        w = gdn_w_in[j]
        wab = jnp.pad(w[:, GDN_QKV + hv:], ((0, 0), (0, LANES - 2 * GDN_HEADS)))
        pv = jnp.zeros((8, LANES), F32)
        pv = pv.at[0, :GDN_HEADS].set(-jnp.exp(gdn_A_log[j].astype(F32)))
        pv = pv.at[1, :GDN_HEADS].set(gdn_dt_bias[j].astype(F32))
        wts["gdn"].append({"wqkv": w[:, :GDN_QKV].astype(BF16),
                           "wz": w[:, GDN_QKV:GDN_QKV + hv].astype(BF16),
                           "wab": wab.astype(BF16),
                           "conv_w": gdn_conv_w[j], "pv": pv,
                           "norm_g": gdn_norm_g[j].reshape(1, GDN_DK),
                           "wout": gdn_w_out[j].astype(BF16)})
    for j in range(sb_w_in.shape[0]):
        qkg = jnp.zeros((8, d), F32)
        qkg = qkg.at[0].set(jnp.tile(sb_q_norm_g[j], SB_HEADS)).at[1].set(jnp.tile(sb_k_norm_g[j], SB_HEADS))
        wts["sb"].append({"win": sb_w_in[j].astype(BF16), "qkg": qkg, "wout": sb_w_out[j].astype(BF16)})
    for j in range(ret_w_in.shape[0]):
        wts["ret"].append({"win": ret_w_in[j].astype(BF16),
                           "norm_g": ret_norm_g[j].reshape(1, RET_HEADS * RET_DV),
                           "wout": ret_w_out[j].astype(BF16)})
    for i in range(DEPTH):
        wts["ffn"].append({"win": ffn_w_in[i].astype(BF16), "conv_w": ffn_conv_w[i],
                           "conv_b": ffn_conv_b[i].reshape(1, 2 * D_FF), "wout": ffn_w_out[i].astype(BF16)})

    nrow = bp + bs
    rows = -(-nrow // 8) * 8
    c_all = jnp.pad(jnp.concatenate([c_prompt, c_sample], axis=0), ((0, rows - nrow), (0, 0)))
    mod = _ada_mod(c_all, ada_w, ada_b)
    mod = jnp.pad(mod.reshape(DEPTH, rows, 6, d), ((0, 0), (0, 0), (0, 2), (0, 0)))
    mod_p, mod_s = mod[:, :bp], mod[:, bp:nrow]

    dt = x_prompt.dtype
    zero_states = [(jnp.zeros((bp, GDN_HEADS, GDN_DK, GDN_DK), dt), jnp.zeros((bp, 3, GDN_QKV), dt)),
                   (None, None),
                   (jnp.zeros((bp, RET_HEADS, RET_DK, RET_DV), dt),),
                   (jnp.zeros((bp, GDN_HEADS, GDN_DK, GDN_DK), dt), jnp.zeros((bp, 3, GDN_QKV), dt))]
    zero_ffn = jnp.zeros((DEPTH, bp, 2, 2 * D_FF), dt)
    y_prompt, p_states, p_ffn_conv = _run_group(x_prompt, mod_p, zero_states, zero_ffn, 0, wts)

    past_len = cache_l1_sb_k.shape[1]
    sample_states = [(state_l0_gdn_S, state_l0_gdn_conv),
                     (cache_l1_sb_k.reshape(bs, past_len, d), cache_l1_sb_v.reshape(bs, past_len, d)),
                     (state_l2_ret,), (state_l3_gdn_S, state_l3_gdn_conv)]
    y_sample, s_states, s_ffn_conv = _run_group(x_sample, mod_s, sample_states, state_ffn_conv, past_len, wts)

    (p_l0_S, p_l0_conv), (p_l1_k, p_l1_v), (p_l2_R,), (p_l3_S, p_l3_conv) = p_states
    (s_l0_S, s_l0_conv), (s_l1_k, s_l1_v), (s_l2_R,), (s_l3_S, s_l3_conv) = s_states
    return (y_prompt, y_sample,
            p_l0_S, p_l0_conv, p_l1_k, p_l1_v, p_l2_R, p_l3_S, p_l3_conv, p_ffn_conv,
            s_l0_S, s_l0_conv, s_l1_k, s_l1_v, s_l2_R, s_l3_S, s_l3_conv, s_ffn_conv)
```

```python
import functools
import math

import jax
import jax.numpy as jnp
from jax import lax
from jax.experimental import pallas as pl
from jax.experimental.pallas import tpu as pltpu

F32 = jnp.float32
BF16 = jnp.bfloat16

D_MODEL = 1024
DEPTH = 4
CHUNK = 64
GDN_HEADS = 8
GDN_DK = 128
GDN_QKV = 3 * GDN_HEADS * GDN_DK
SB_HEADS = 16
SB_DH = 64
RET_HEADS = 4
RET_DK = 256
RET_DV = 512
RET_ROPE_BASE = 10000.0
D_FF = 2816
NORM_EPS = 1e-6

LANES = 128
MXU_N = 256
VMEM_LIMIT = 56 * 1024 * 1024

ROW_TILE = 512
GDN_TILE = 256
RET_CHUNK = 256
SB_TILE = 256
SB_DEAD = 120.0


def _cparams(*sem):
    return pltpu.CompilerParams(dimension_semantics=sem, vmem_limit_bytes=VMEM_LIMIT)


def _wspec(shape):
    return pl.BlockSpec(shape, lambda i, j: (0,) * len(shape), pipeline_mode=pl.Buffered(1))


def _dot(a, b):
    return jnp.dot(a.astype(BF16), b.astype(BF16), preferred_element_type=F32)


def _dot_nt(a, b):
    return lax.dot_general(a.astype(BF16), b.astype(BF16), (((1,), (1,)), ((), ())),
                           preferred_element_type=F32)


def _dot_tn(a, b):
    return lax.dot_general(a.astype(BF16), b.astype(BF16), (((0,), (0,)), ((), ())),
                           preferred_element_type=F32)


def _split(x):
    hi = x.astype(BF16)
    lo = (x - hi.astype(F32)).astype(BF16)
    return hi, lo


def _dot_sel(a, sel):
    hi, lo = _split(a)
    return (jnp.dot(hi, sel, preferred_element_type=F32)
            + jnp.dot(lo, sel, preferred_element_type=F32))


def _sel_dot(sel, b):
    hi, lo = _split(b)
    return (jnp.dot(sel, hi, preferred_element_type=F32)
            + jnp.dot(sel, lo, preferred_element_type=F32))


def _dot3(a, b):
    ah, al = _split(a)
    bh, bl = _split(b)
    return (jnp.dot(ah, bh, preferred_element_type=F32)
            + jnp.dot(ah, bl, preferred_element_type=F32)
            + jnp.dot(al, bh, preferred_element_type=F32))


def _sigmoid(x):
    return 1.0 / (1.0 + jnp.exp(-x))


def _silu(x):
    return x * _sigmoid(x)


def _softplus(x):
    return jnp.maximum(x, 0.0) + jnp.log(1.0 + jnp.exp(-jnp.abs(x)))


def _norm_mod(x, gain, scale, shift):
    ms = jnp.mean(x * x, axis=-1, keepdims=True)
    y = x * lax.rsqrt(ms + NORM_EPS) * gain
    return y * (1.0 + scale) + shift


def _iota2(shape, dim):
    return lax.broadcasted_iota(jnp.int32, shape, dim)


def _ada_kernel(c_ref, w_ref, b_ref, o_ref):
    o_ref[0] = _dot(_silu(c_ref[...]), w_ref[0]) + b_ref[0]


def _ada_mod(c_all, ada_w, ada_b):
    rows = c_all.shape[0]
    n = ada_w.shape[2]
    tn = 1536
    return pl.pallas_call(
        _ada_kernel,
        grid=(DEPTH, n // tn),
        in_specs=[pl.BlockSpec((rows, D_MODEL), lambda i, j: (0, 0)),
                  pl.BlockSpec((1, D_MODEL, tn), lambda i, j: (i, 0, j)),
                  pl.BlockSpec((1, 1, tn), lambda i, j: (i, 0, j))],
        out_specs=pl.BlockSpec((1, rows, tn), lambda i, j: (i, 0, j)),
        out_shape=jax.ShapeDtypeStruct((DEPTH, rows, n), F32),
        compiler_params=_cparams("arbitrary", "arbitrary"),
        name="ada_mod",
    )(c_all, ada_w, ada_b.reshape(DEPTH, 1, n))


def _causal_conv(u, ext_ref, halo_ref, col0, w_ref, width, nb, tt):
    n = u.shape[-1]
    hw = width - 1
    ext_ref[:, 8:8 + tt, :] = u.reshape(nb, tt, n)
    ext_ref[:, 8 - hw:8, :] = halo_ref[:, 8 - hw:8, col0:col0 + n]
    y = ext_ref[:, 8:8 + tt, :] * w_ref[hw:hw + 1, col0:col0 + n]
    for j in range(hw):
        y = y + ext_ref[:, 8 - hw + j:8 - hw + j + tt, :] * w_ref[j:j + 1, col0:col0 + n]
    halo_ref[:, 8 - hw:8, col0:col0 + n] = ext_ref[:, 8 + tt - hw:8 + tt, :]
    return y


FFN_COLS = 256


def _ffn_kernel(x_ref, mod_ref, ng_ref, win_ref, cw_ref, cb_ref, wout_ref, buf_ref,
                y_ref, nbuf_ref, extg_ref, extv_ref, halo_ref, *, nb, tt):
    m = nb * tt

    @pl.when(pl.program_id(1) == 0)
    def _():
        halo_ref[:, 6:8, :] = buf_ref[...]

    x = x_ref[...]
    h = _norm_mod(x, ng_ref[...], mod_ref[:, 4:5, :], mod_ref[:, 3:4, :])
    h = h.reshape(m, D_MODEL).astype(BF16)
    acc = jnp.zeros((m, D_MODEL), F32)
    for c in range(D_FF // FFN_COLS):
        g0 = c * FFN_COLS
        v0 = D_FF + c * FFN_COLS
        ug = jnp.dot(h, win_ref[:, g0:g0 + FFN_COLS], preferred_element_type=F32)
        uv = jnp.dot(h, win_ref[:, v0:v0 + FFN_COLS], preferred_element_type=F32)
        yg = _causal_conv(ug, extg_ref, halo_ref, g0, cw_ref, 3, nb, tt) + cb_ref[:, g0:g0 + FFN_COLS]
        yv = _causal_conv(uv, extv_ref, halo_ref, v0, cw_ref, 3, nb, tt) + cb_ref[:, v0:v0 + FFN_COLS]
        a = (_silu(yg) * yv).reshape(m, FFN_COLS).astype(BF16)
        acc = acc + jnp.dot(a, wout_ref[g0:g0 + FFN_COLS, :], preferred_element_type=F32)
    y_ref[...] = x + mod_ref[:, 5:6, :] * acc.reshape(nb, tt, D_MODEL)
    nbuf_ref[...] = halo_ref[:, 6:8, :]


def _ffn(x, mod, ng, win, cw, cb, wout, buf, nb, tt):
    b, t, d = x.shape
    n2 = 2 * D_FF
    kern = functools.partial(_ffn_kernel, nb=nb, tt=tt)
    return pl.pallas_call(
        kern,
        grid=(b // nb, t // tt),
        in_specs=[pl.BlockSpec((nb, tt, d), lambda i, j: (i, j, 0)),
                  pl.BlockSpec((nb, 8, d), lambda i, j: (i, 0, 0)),
                  pl.BlockSpec((1, d), lambda i, j: (0, 0)),
                  _wspec((d, n2)),
                  pl.BlockSpec((3, n2), lambda i, j: (0, 0)),
                  pl.BlockSpec((1, n2), lambda i, j: (0, 0)),
                  _wspec((D_FF, d)),
                  pl.BlockSpec((nb, 2, n2), lambda i, j: (i, 0, 0))],
        out_specs=[pl.BlockSpec((nb, tt, d), lambda i, j: (i, j, 0)),
                   pl.BlockSpec((nb, 2, n2), lambda i, j: (i, 0, 0))],
        out_shape=[jax.ShapeDtypeStruct((b, t, d), F32),
                   jax.ShapeDtypeStruct((b, 2, n2), F32)],
        scratch_shapes=[pltpu.VMEM((nb, 8 + tt, FFN_COLS), F32),
                        pltpu.VMEM((nb, 8 + tt, FFN_COLS), F32),
                        pltpu.VMEM((nb, 8, n2), F32)],
        compiler_params=_cparams("arbitrary", "arbitrary"),
        name="conv_ffn",
    )(x, mod, ng, win, cw, cb, wout, buf)


def _gdn_proj_kernel(x_ref, mod_ref, ng_ref, wqkv_ref, wz_ref, wab_ref, cw_ref, pv_ref, buf_ref,
                     q_ref, k_ref, v_ref, z_ref, gb_ref, nbuf_ref, ext_ref, halo_ref, *, nb, tt):
    m = nb * tt
    d = D_MODEL

    @pl.when(pl.program_id(1) == 0)
    def _():
        halo_ref[:, 5:8, :] = buf_ref[...]

    h = _norm_mod(x_ref[...], ng_ref[...], mod_ref[:, 1:2, :], mod_ref[:, 0:1, :])
    h = h.reshape(m, d).astype(BF16)
    z_ref[...] = jnp.dot(h, wz_ref[...], preferred_element_type=F32).reshape(nb, tt, d)
    ab = jnp.dot(h, wab_ref[...], preferred_element_type=F32)
    lane = _iota2(ab.shape, 1)
    gb = jnp.where(lane < GDN_HEADS,
                   pv_ref[0:1, :] * _softplus(ab + pv_ref[1:2, :]),
                   _sigmoid(ab))
    gb_ref[...] = gb.reshape(nb, tt, LANES)
    outs = (q_ref, k_ref, v_ref)
    for s in range(GDN_QKV // MXU_N):
        c0 = s * MXU_N
        u = jnp.dot(h, wqkv_ref[:, c0:c0 + MXU_N], preferred_element_type=F32)
        y = _silu(_causal_conv(u, ext_ref, halo_ref, c0, cw_ref, 4, nb, tt))
        which, o0 = divmod(c0, d)
        if which < 2:
            halves = []
            for hh in range(MXU_N // GDN_DK):
                yh = y[:, :, hh * GDN_DK:(hh + 1) * GDN_DK]
                r = lax.rsqrt(jnp.sum(yh * yh, axis=-1, keepdims=True) + NORM_EPS)
                if which == 0:
                    r = r * (GDN_DK ** -0.5)
                halves.append(yh * r)
            y = jnp.concatenate(halves, axis=-1)
        outs[which][:, :, o0:o0 + MXU_N] = y
    nbuf_ref[...] = halo_ref[:, 5:8, :]


def _gdn_proj(x, mod, ng, wqkv, wz, wab, cw, pv, buf, nb, tt):
    b, t, d = x.shape
    kern = functools.partial(_gdn_proj_kernel, nb=nb, tt=tt)
    row = lambda i, j: (i, j, 0)
    const2 = lambda i, j: (0, 0)
    act = jax.ShapeDtypeStruct((b, t, d), F32)
    return pl.pallas_call(
        kern,
        grid=(b // nb, t // tt),
        in_specs=[pl.BlockSpec((nb, tt, d), row),
                  pl.BlockSpec((nb, 8, d), lambda i, j: (i, 0, 0)),
                  pl.BlockSpec((1, d), const2),
                  _wspec((d, GDN_QKV)),
                  _wspec((d, d)),
                  pl.BlockSpec((d, LANES), const2),
                  pl.BlockSpec((4, GDN_QKV), const2),
                  pl.BlockSpec((8, LANES), const2),
                  pl.BlockSpec((nb, 3, GDN_QKV), lambda i, j: (i, 0, 0))],
        out_specs=[pl.BlockSpec((nb, tt, d), row)] * 4
                  + [pl.BlockSpec((nb, tt, LANES), row),
                     pl.BlockSpec((nb, 3, GDN_QKV), lambda i, j: (i, 0, 0))],
        out_shape=[act, act, act, act,
                   jax.ShapeDtypeStruct((b, t, LANES), F32),
                   jax.ShapeDtypeStruct((b, 3, GDN_QKV), F32)],
        scratch_shapes=[pltpu.VMEM((nb, 8 + tt, MXU_N), F32),
                        pltpu.VMEM((nb, 8, GDN_QKV), F32)],
        compiler_params=_cparams("arbitrary", "arbitrary"),
        name="gdn_proj",
    )(x, mod, ng, wqkv, wz, wab, cw, pv, buf)


def _tri_inverses(lmats, chunk):
    n = lmats[0].shape[0]
    row = _iota2((n, n), 0)
    col = _iota2((n, n), 1)
    level = 31 - lax.clz(jnp.bitwise_xor(row, col))
    eye = (row == col).astype(F32)
    xs = [eye - jnp.where(level == 0, lm, 0.0) for lm in lmats]
    k = 1
    while (1 << k) < chunk:
        ps = [_dot(x, jnp.where(level == k, lm, 0.0)) for x, lm in zip(xs, lmats)]
        xs = [x - _dot(p, x) for x, p in zip(xs, ps)]
        k += 1
    return xs


def _gdn_core_kernel(q_ref, k_ref, v_ref, z_ref, gb_ref, x_ref, mod_ref, og_ref, wout_ref, s0_ref,
                     y_ref, sout_ref, s_ref, o_ref, *, tt, chunk):
    @pl.when(pl.program_id(1) == 0)
    def _():
        s_ref[...] = s0_ref[0]

    nck = tt // chunk
    row = _iota2((tt, tt), 0)
    col = _iota2((tt, tt), 1)
    if nck > 1:
        same = (row // chunk) == (col // chunk)
        incl = (row >= col) & same
        strict = (row > col) & same
    else:
        same = None
        incl = row >= col
        strict = row > col
    gb = gb_ref[0]
    gcol = _sel_dot(incl.astype(BF16), gb)
    if nck > 1:
        glast = _sel_dot(same.astype(BF16), gb)
    else:
        glast = jnp.broadcast_to(jnp.sum(gb, axis=0, keepdims=True), gb.shape)
    grow = gcol.T
    eg = jnp.exp(gcol)
    ekd = jnp.exp(glast - gcol)
    egl = jnp.exp(glast)

    heads = range(GDN_HEADS)
    cols = [slice(hd * GDN_DK, (hd + 1) * GDN_DK) for hd in heads]
    kbs, lmats, qks = [], [], []
    for hd in heads:
        kh = k_ref[0, :, cols[hd]]
        diff = gcol[:, hd:hd + 1] - grow[hd:hd + 1, :]
        decay = jnp.exp(jnp.where(incl, diff, -jnp.inf))
        kb = kh * gb[:, GDN_HEADS + hd:GDN_HEADS + hd + 1]
        kbs.append(kb)
        lmats.append(jnp.where(strict, _dot_nt(kb, kh) * decay, 0.0))
        qks.append((_dot_nt(q_ref[0, :, cols[hd]], kh) * decay).astype(BF16))
    tinvs = [t.astype(BF16) for t in _tri_inverses(lmats, chunk)]
    us, ws = [], []
    for hd in heads:
        beta = gb[:, GDN_HEADS + hd:GDN_HEADS + hd + 1]
        us.append(_dot(tinvs[hd], v_ref[0, :, cols[hd]] * beta))
        ws.append(_dot(tinvs[hd], kbs[hd] * eg[:, hd:hd + 1]).astype(BF16))
    ss = [s_ref[hd] for hd in heads]
    vnews = [[] for _ in heads]
    ocross = [[] for _ in heads]
    for c in range(nck):
        rows = slice(c * chunk, (c + 1) * chunk)
        for hd in heads:
            sb = ss[hd].astype(BF16)
            vn = us[hd][rows] - _dot(ws[hd][rows], sb)
            ocross[hd].append(_dot(q_ref[0, rows, cols[hd]] * eg[rows, hd:hd + 1], sb))
            kd = k_ref[0, rows, cols[hd]] * ekd[rows, hd:hd + 1]
            ss[hd] = ss[hd] * egl[c * chunk:c * chunk + 1, hd:hd + 1] + _dot_tn(kd, vn)
            vnews[hd].append(vn.astype(BF16))
    for hd in heads:
        s_ref[hd] = ss[hd]
        vnew = jnp.concatenate(vnews[hd], axis=0) if nck > 1 else vnews[hd][0]
        oc = jnp.concatenate(ocross[hd], axis=0) if nck > 1 else ocross[hd][0]
        o = oc + _dot(qks[hd], vnew)
        o = o * lax.rsqrt(jnp.mean(o * o, axis=-1, keepdims=True) + NORM_EPS) * og_ref[...]
        o_ref[:, cols[hd]] = (o * _silu(z_ref[0, :, cols[hd]])).astype(BF16)

    out = jnp.dot(o_ref[...], wout_ref[...], preferred_element_type=F32)
    y_ref[0] = x_ref[0] + mod_ref[0, 2:3, :] * out
    sout_ref[0] = s_ref[...]


def _gdn_core(q, k, v, z, gb, x, mod, og, wout, s0, tt, chunk):
    b, t, d = x.shape
    kern = functools.partial(_gdn_core_kernel, tt=tt, chunk=chunk)
    row = lambda i, j: (i, j, 0)
    return pl.pallas_call(
        kern,
        grid=(b, t // tt),
        in_specs=[pl.BlockSpec((1, tt, d), row)] * 4
                 + [pl.BlockSpec((1, tt, LANES), row),
                    pl.BlockSpec((1, tt, d), row),
                    pl.BlockSpec((1, 8, d), lambda i, j: (i, 0, 0)),
                    pl.BlockSpec((1, GDN_DK), lambda i, j: (0, 0)),
                    _wspec((d, d)),
                    pl.BlockSpec((1, GDN_HEADS, GDN_DK, GDN_DK), lambda i, j: (i, 0, 0, 0))],
        out_specs=[pl.BlockSpec((1, tt, d), row),
                   pl.BlockSpec((1, GDN_HEADS, GDN_DK, GDN_DK), lambda i, j: (i, 0, 0, 0))],
        out_shape=[jax.ShapeDtypeStruct((b, t, d), F32),
                   jax.ShapeDtypeStruct((b, GDN_HEADS, GDN_DK, GDN_DK), F32)],
        scratch_shapes=[pltpu.VMEM((GDN_HEADS, GDN_DK, GDN_DK), F32),
                        pltpu.VMEM((tt, d), BF16)],
        compiler_params=_cparams("arbitrary", "arbitrary"),
        name="gdn_core",
    )(q, k, v, z, gb, x, mod, og, wout, s0)


def _sb_proj_kernel(x_ref, mod_ref, ng_ref, w_ref, qkg_ref,
                    kf_ref, vf_ref, qb_ref, kb_ref, vb_ref, *, nb, tt):
    m = nb * tt
    d = D_MODEL
    h = _norm_mod(x_ref[...], ng_ref[...], mod_ref[:, 1:2, :], mod_ref[:, 0:1, :])
    h = h.reshape(m, d).astype(BF16)
    grp = ((_iota2((MXU_N, MXU_N), 0) // SB_DH) == (_iota2((MXU_N, MXU_N), 1) // SB_DH)).astype(BF16)
    for s in range(3 * d // MXU_N):
        c0 = s * MXU_N
        which, o0 = divmod(c0, d)
        u = jnp.dot(h, w_ref[:, c0:c0 + MXU_N], preferred_element_type=F32)
        if which < 2:
            ms = _dot_sel(u * u, grp) * (1.0 / SB_DH)
            u = u * lax.rsqrt(ms + NORM_EPS) * qkg_ref[which:which + 1, o0:o0 + MXU_N]
        u3 = u.reshape(nb, tt, MXU_N)
        if which == 0:
            qb_ref[:, :, o0:o0 + MXU_N] = u3.astype(BF16)
        elif which == 1:
            kf_ref[:, :, o0:o0 + MXU_N] = u3
            kb_ref[:, :, o0:o0 + MXU_N] = u3.astype(BF16)
        else:
            vf_ref[:, :, o0:o0 + MXU_N] = u3
            vb_ref[:, :, o0:o0 + MXU_N] = u3.astype(BF16)


def _sb_proj(x, mod, ng, w, qkg, nb, tt):
    b, t, d = x.shape
    kern = functools.partial(_sb_proj_kernel, nb=nb, tt=tt)
    row = lambda i, j: (i, j, 0)
    const2 = lambda i, j: (0, 0)
    f = jax.ShapeDtypeStruct((b, t, d), F32)
    h = jax.ShapeDtypeStruct((b, t, d), BF16)
    return pl.pallas_call(
        kern,
        grid=(b // nb, t // tt),
        in_specs=[pl.BlockSpec((nb, tt, d), row),
                  pl.BlockSpec((nb, 8, d), lambda i, j: (i, 0, 0)),
                  pl.BlockSpec((1, d), const2),
                  _wspec((d, 3 * d)),
                  pl.BlockSpec((8, d), const2)],
        out_specs=[pl.BlockSpec((nb, tt, d), row)] * 5,
        out_shape=[f, f, h, h, h],
        compiler_params=_cparams("arbitrary", "arbitrary"),
        name="sb_proj",
    )(x, mod, ng, w, qkg)


def _suffix_selector(bw):
    r = jnp.bitwise_and(_iota2((2 * bw, bw + LANES), 0), bw - 1)
    c = _iota2((2 * bw, bw + LANES), 1)
    return jnp.where(c < bw, (r > c).astype(F32), 1.0).astype(BF16)


def _sb_attn_kernel(*refs, tq, tk, past_len):
    if past_len:
        q_ref, k_ref, v_ref, pk_ref, pv_ref, o_ref = refs
    else:
        q_ref, k_ref, v_ref, o_ref = refs
        pk_ref = pv_ref = None
    qi = pl.program_id(2)
    q = q_ref[0] * (SB_DH ** -0.5)
    lane = _iota2((tq, LANES), 1)
    qhs = [jnp.where((lane // SB_DH) == hd, q, jnp.zeros_like(q)) for hd in range(2)]
    sel = {bw: _suffix_selector(bw) for bw in {min(LANES, tq), min(LANES, tk)}}

    def tile(qh, kt, vt, carry, acc, mask):
        n = kt.shape[0]
        bw = min(LANES, n)
        z = _dot_nt_bf(qh, kt)
        sp = jnp.maximum(z, 0.0) + jnp.log(1.0 + jnp.exp(-jnp.abs(z)))
        spm = sp if mask is None else jnp.where(mask, sp, 0.0)
        ws = []
        for mb in reversed(range(n // bw)):
            sl = slice(mb * bw, (mb + 1) * bw)
            hi = spm[:, sl].astype(BF16)
            lo = (spm[:, sl] - hi.astype(F32)).astype(BF16)
            r = jnp.dot(jnp.concatenate([hi, lo], axis=1), sel[bw], preferred_element_type=F32)
            w = jnp.exp(z[:, sl] - sp[:, sl] - carry[:, :bw] - r[:, :bw])
            if mask is not None:
                w = jnp.where(mask[:, sl], w, 0.0)
            ws.append(w.astype(BF16))
            carry = carry + r[:, bw:]
        wts = jnp.concatenate(ws[::-1], axis=-1) if len(ws) > 1 else ws[0]
        acc = acc + jnp.dot(wts, vt, preferred_element_type=F32)
        return carry, acc

    def both(kt, vt, state, mask):
        c0, a0 = tile(qhs[0], kt, vt, state[0], state[1], mask)
        c1, a1 = tile(qhs[1], kt, vt, state[2], state[3], mask)
        return c0, a0, c1, a1

    zero = jnp.zeros((tq, LANES), F32)
    r0 = pl.multiple_of(qi * tq, tq)
    dmask = _iota2((tq, tq), 1) < _iota2((tq, tq), 0)
    state = both(k_ref[0, pl.ds(r0, tq), :], v_ref[0, pl.ds(r0, tq), :], (zero, zero, zero, zero), dmask)

    def alive(st):
        return jnp.min(jnp.minimum(st[0], st[2])) < SB_DEAD

    def key_loop(ntiles, load, st):
        def cond(c):
            return jnp.logical_and(c[0] < ntiles, c[1])

        def body(c):
            kt, vt = load(pl.multiple_of((ntiles - 1 - c[0]) * tk, tk))
            nst = both(kt, vt, c[2:], None)
            return (c[0] + 1, alive(nst)) + nst

        return lax.while_loop(cond, body, (jnp.int32(0), alive(st)) + st)[2:]

    state = key_loop(qi, lambda k0: (k_ref[0, pl.ds(k0, tk), :], v_ref[0, pl.ds(k0, tk), :]), state)
    if past_len:
        state = key_loop(past_len // tk,
                         lambda k0: (pk_ref[0, pl.ds(k0, tk), :].astype(BF16),
                                     pv_ref[0, pl.ds(k0, tk), :].astype(BF16)), state)
    o_ref[0] = jnp.where((lane // SB_DH) == 0, state[1], state[3]).astype(BF16)


def _dot_nt_bf(a, b):
    return lax.dot_general(a, b, (((1,), (1,)), ((), ())), preferred_element_type=F32)


def _sb_attn(qb, kb, vb, past_k, past_v, tq):
    b, t, d = qb.shape
    past_len = 0 if past_k is None else past_k.shape[1]
    tk = tq if not past_len else min(256, past_len)
    kern = functools.partial(_sb_attn_kernel, tq=tq, tk=tk, past_len=past_len)
    qspec = pl.BlockSpec((1, tq, LANES), lambda i, h, j: (i, j, h))
    kvspec = pl.BlockSpec((1, t, LANES), lambda i, h, j: (i, 0, h))
    in_specs = [qspec, kvspec, kvspec]
    args = [qb, kb, vb]
    if past_len:
        pspec = pl.BlockSpec((1, past_len, LANES), lambda i, h, j: (i, 0, h))
        in_specs += [pspec, pspec]
        args += [past_k, past_v]
    return pl.pallas_call(
        kern,
        grid=(b, d // LANES, t // tq),
        in_specs=in_specs,
        out_specs=qspec,
        out_shape=jax.ShapeDtypeStruct((b, t, d), BF16),
        compiler_params=_cparams("arbitrary", "arbitrary", "arbitrary"),
        name="sb_attn",
    )(*args)


def _out_proj_kernel(o_ref, x_ref, mod_ref, w_ref, y_ref, *, nb, tt):
    out = jnp.dot(o_ref[...].reshape(nb * tt, -1), w_ref[...], preferred_element_type=F32)
    y_ref[...] = x_ref[...] + mod_ref[:, 2:3, :] * out.reshape(nb, tt, D_MODEL)


def _out_proj(o, x, mod, w, nb, tt):
    b, t, d = x.shape
    kin = o.shape[-1]
    kern = functools.partial(_out_proj_kernel, nb=nb, tt=tt)
    row = lambda i, j: (i, j, 0)
    return pl.pallas_call(
        kern,
        grid=(b // nb, t // tt),
        in_specs=[pl.BlockSpec((nb, tt, kin), row),
                  pl.BlockSpec((nb, tt, d), row),
                  pl.BlockSpec((nb, 8, d), lambda i, j: (i, 0, 0)),
                  _wspec((kin, d))],
        out_specs=pl.BlockSpec((nb, tt, d), row),
        out_shape=jax.ShapeDtypeStruct((b, t, d), F32),
        compiler_params=_cparams("arbitrary", "arbitrary"),
        name="out_proj",
    )(o, x, mod, w)


def _ret_log_gamma(hd):
    return math.log1p(-(2.0 ** (-5.0 - hd)))


def _ret_proj_kernel(x_ref, mod_ref, ng_ref, w_ref, cos_ref, sin_ref,
                     q_ref, qd_ref, k_ref, kd_ref, v_ref, gt_ref, *, nb, tt, chunk):
    m = nb * tt
    d = D_MODEL
    hk = RET_HEADS * RET_DK
    hv = RET_HEADS * RET_DV
    h = _norm_mod(x_ref[...], ng_ref[...], mod_ref[:, 1:2, :], mod_ref[:, 0:1, :])
    h = h.reshape(m, d).astype(BF16)
    cos = cos_ref[...][None]
    sin = sin_ref[...][None]
    tpos = pl.program_id(1) * tt + _iota2((1, tt, 1), 1)
    assert chunk & (chunk - 1) == 0
    idx = jnp.bitwise_and(tpos, chunk - 1).astype(F32)
    half = RET_DK // 2
    for s in range(2 * hk // MXU_N):
        c0 = s * MXU_N
        which, o0 = divmod(c0, hk)
        hd = o0 // RET_DK
        lg = _ret_log_gamma(hd)
        u = jnp.dot(h, w_ref[:, c0:c0 + MXU_N], preferred_element_type=F32).reshape(nb, tt, MXU_N)
        x1 = u[:, :, :half]
        x2 = u[:, :, half:]
        r = jnp.concatenate([x1 * cos - x2 * sin, x1 * sin + x2 * cos], axis=-1)
        if which == 0:
            q_ref[:, :, o0:o0 + MXU_N] = r.astype(BF16)
            qd_ref[:, :, o0:o0 + MXU_N] = (r * jnp.exp(lg * (idx + 1.0))).astype(BF16)
        else:
            r = r * (RET_DK ** -0.5)
            k_ref[:, :, o0:o0 + MXU_N] = r.astype(BF16)
            kd_ref[:, :, o0:o0 + MXU_N] = (r * jnp.exp(lg * (chunk - 1.0 - idx))).astype(BF16)
    for s in range(2 * hv // MXU_N):
        c0 = 2 * hk + s * MXU_N
        u = jnp.dot(h, w_ref[:, c0:c0 + MXU_N], preferred_element_type=F32).reshape(nb, tt, MXU_N)
        o0 = s * MXU_N
        if o0 < hv:
            v_ref[:, :, o0:o0 + MXU_N] = u.astype(BF16)
        else:
            gt_ref[:, :, o0 - hv:o0 - hv + MXU_N] = u


def _ret_proj(x, mod, ng, w, cos, sin, nb, tt, chunk):
    b, t, d = x.shape
    hk = RET_HEADS * RET_DK
    hv = RET_HEADS * RET_DV
    kern = functools.partial(_ret_proj_kernel, nb=nb, tt=tt, chunk=chunk)
    row = lambda i, j: (i, j, 0)
    const2 = lambda i, j: (0, 0)
    qs = jax.ShapeDtypeStruct((b, t, hk), BF16)
    return pl.pallas_call(
        kern,
        grid=(b // nb, t // tt),
        in_specs=[pl.BlockSpec((nb, tt, d), row),
                  pl.BlockSpec((nb, 8, d), lambda i, j: (i, 0, 0)),
                  pl.BlockSpec((1, d), const2),
                  _wspec((d, 2 * hk + 2 * hv)),
                  pl.BlockSpec((tt, RET_DK // 2), lambda i, j: (j, 0)),
                  pl.BlockSpec((tt, RET_DK // 2), lambda i, j: (j, 0))],
        out_specs=[pl.BlockSpec((nb, tt, hk), row)] * 4
                  + [pl.BlockSpec((nb, tt, hv), row)] * 2,
        out_shape=[qs, qs, qs, qs,
                   jax.ShapeDtypeStruct((b, t, hv), BF16),
                   jax.ShapeDtypeStruct((b, t, hv), F32)],
        compiler_params=_cparams("arbitrary", "arbitrary"),
        name="ret_proj",
    )(x, mod, ng, w, cos, sin)


def _ret_core_kernel(q_ref, qd_ref, k_ref, kd_ref, v_ref, gt_ref, x_ref, mod_ref, og_ref, wout_ref, r0_ref,
                     y_ref, rout_ref, r_ref, o_ref, *, chunk):
    @pl.when(pl.program_id(1) == 0)
    def _():
        r_ref[...] = r0_ref[0]

    rel = (_iota2((chunk, chunk), 0) - _iota2((chunk, chunk), 1)).astype(F32)
    for hd in range(RET_HEADS):
        lg = _ret_log_gamma(hd)
        k0 = hd * RET_DK
        v0 = hd * RET_DV
        dmask = jnp.where(rel >= 0, jnp.exp(lg * jnp.maximum(rel, 0.0)), 0.0)
        vh = v_ref[0, :, v0:v0 + RET_DV]
        s = _dot_nt_bf(q_ref[0, :, k0:k0 + RET_DK], k_ref[0, :, k0:k0 + RET_DK]) * dmask
        r = r_ref[hd]
        o = (jnp.dot(s.astype(BF16), vh, preferred_element_type=F32)
             + jnp.dot(qd_ref[0, :, k0:k0 + RET_DK], r.astype(BF16), preferred_element_type=F32))
        r_ref[hd] = r * math.exp(lg * chunk) + lax.dot_general(
            kd_ref[0, :, k0:k0 + RET_DK], vh, (((0,), (0,)), ((), ())), preferred_element_type=F32)
        mu = jnp.mean(o, axis=-1, keepdims=True)
        dlt = o - mu
        var = jnp.mean(dlt * dlt, axis=-1, keepdims=True)
        o = dlt * lax.rsqrt(var + NORM_EPS) * og_ref[:, v0:v0 + RET_DV]
        o_ref[:, v0:v0 + RET_DV] = (o * _silu(gt_ref[0, :, v0:v0 + RET_DV])).astype(BF16)
    out = jnp.dot(o_ref[...], wout_ref[...], preferred_element_type=F32)
    y_ref[0] = x_ref[0] + mod_ref[0, 2:3, :] * out
    rout_ref[0] = r_ref[...]


def _ret_core(q, qd, k, kd, v, gt, x, mod, og, wout, r0, chunk):
    b, t, d = x.shape
    hk = RET_HEADS * RET_DK
    hv = RET_HEADS * RET_DV
    kern = functools.partial(_ret_core_kernel, chunk=chunk)
    row = lambda i, j: (i, j, 0)
    st = pl.BlockSpec((1, RET_HEADS, RET_DK, RET_DV), lambda i, j: (i, 0, 0, 0))
    return pl.pallas_call(
        kern,
        grid=(b, t // chunk),
        in_specs=[pl.BlockSpec((1, chunk, hk), row)] * 4
                 + [pl.BlockSpec((1, chunk, hv), row)] * 2
                 + [pl.BlockSpec((1, chunk, d), row),
                    pl.BlockSpec((1, 8, d), lambda i, j: (i, 0, 0)),
                    pl.BlockSpec((1, hv), lambda i, j: (0, 0)),
                    _wspec((hv, d)),
                    st],
        out_specs=[pl.BlockSpec((1, chunk, d), row), st],
        out_shape=[jax.ShapeDtypeStruct((b, t, d), F32),
                   jax.ShapeDtypeStruct((b, RET_HEADS, RET_DK, RET_DV), F32)],
        scratch_shapes=[pltpu.VMEM((RET_HEADS, RET_DK, RET_DV), F32),
                        pltpu.VMEM((chunk, hv), BF16)],
        compiler_params=_cparams("arbitrary", "arbitrary"),
        name="ret_core",
    )(q, qd, k, kd, v, gt, x, mod, og, wout, r0)


def _run_group(x, mod_all, states, ffn_bufs, pos0, wts):
    b, t, d = x.shape
    if t >= ROW_TILE:
        nb, tt = 1, ROW_TILE
    else:
        nb, tt = b, t
    new_states, new_ffn = [], []
    for i in range(DEPTH):
        kind, j = i % 3, i // 3
        mod = mod_all[i]
        if kind == 0:
            s0, cbuf = states[i]
            gw = wts["gdn"][j]
            q, k, v, z, gb, ncbuf = _gdn_proj(x, mod, wts["norm_mix_g"][i], gw["wqkv"], gw["wz"], gw["wab"],
                                              gw["conv_w"], gw["pv"], cbuf, nb, tt)
            x, s_new = _gdn_core(q, k, v, z, gb, x, mod, gw["norm_g"], gw["wout"], s0,
                                 min(GDN_TILE, t), min(CHUNK, t))
            new_states.append((s_new, ncbuf))
        elif kind == 1:
            pk, pv = states[i]
            sw = wts["sb"][j]
            kf, vf, qb, kb, vb = _sb_proj(x, mod, wts["norm_mix_g"][i], sw["win"], sw["qkg"], nb, tt)
            o = _sb_attn(qb, kb, vb, pk, pv, min(SB_TILE, t))
            x = _out_proj(o, x, mod, sw["wout"], nb, tt)
            new_states.append((kf.reshape(b, t, SB_HEADS, SB_DH), vf.reshape(b, t, SB_HEADS, SB_DH)))
        else:
            (r0,) = states[i]
            rw = wts["ret"][j]
            chunk = min(RET_CHUNK, t)
            half = RET_DK // 2
            inv_freq = RET_ROPE_BASE ** (-jnp.arange(half, dtype=F32) / half)
            ang = (pos0 + jnp.arange(t)).astype(F32)[:, None] * inv_freq[None, :]
            q, qd, k, kd, v, gt = _ret_proj(x, mod, wts["norm_mix_g"][i], rw["win"], jnp.cos(ang), jnp.sin(ang),
                                            nb, tt, chunk)
            x, r_new = _ret_core(q, qd, k, kd, v, gt, x, mod, rw["norm_g"], rw["wout"], r0, chunk)
            new_states.append((r_new,))
        fw = wts["ffn"][i]
        x, fbuf = _ffn(x, mod, wts["norm_ffn_g"][i], fw["win"], fw["conv_w"], fw["conv_b"], fw["wout"],
                       ffn_bufs[i], nb, tt)
        new_ffn.append(fbuf)
    return x, new_states, jnp.stack(new_ffn)


def kernel(x_prompt, x_sample, state_l0_gdn_S, state_l0_gdn_conv, cache_l1_sb_k, cache_l1_sb_v, state_l2_ret, state_l3_gdn_S, state_l3_gdn_conv, state_ffn_conv, c_prompt, c_sample, ada_w, ada_b, norm_mix_g, norm_ffn_g, gdn_w_in, gdn_conv_w, gdn_A_log, gdn_dt_bias, gdn_norm_g, gdn_w_out, sb_w_in, sb_q_norm_g, sb_k_norm_g, sb_w_out, ret_w_in, ret_norm_g, ret_w_out, ffn_w_in, ffn_conv_w, ffn_conv_b, ffn_w_out):
    d = D_MODEL
    bp, tp, _ = x_prompt.shape
    bs, ts, _ = x_sample.shape

    hv = GDN_HEADS * GDN_DK
    wts = {"norm_mix_g": [norm_mix_g[i].reshape(1, d) for i in range(DEPTH)],
           "norm_ffn_g": [norm_ffn_g[i].reshape(1, d) for i in range(DEPTH)],
           "gdn": [], "sb": [], "ret": [], "ffn": []}
    for j in range(gdn_w_in.shape[0]):
        w = gdn_w_in[j]
        wab = jnp.pad(w[:, GDN_QKV + hv:], ((0, 0), (0, LANES - 2 * GDN_HEADS)))
        pv = jnp.zeros((8, LANES), F32)
        pv = pv.at[0, :GDN_HEADS].set(-jnp.exp(gdn_A_log[j].astype(F32)))
        pv = pv.at[1, :GDN_HEADS].set(gdn_dt_bias[j].astype(F32))
        wts["gdn"].append({"wqkv": w[:, :GDN_QKV].astype(BF16),
                           "wz": w[:, GDN_QKV:GDN_QKV + hv].astype(BF16),
                           "wab": wab.astype(BF16),
                           "conv_w": gdn_conv_w[j], "pv": pv,
                           "norm_g": gdn_norm_g[j].reshape(1, GDN_DK),
                           "wout": gdn_w_out[j].astype(BF16)})
    for j in range(sb_w_in.shape[0]):
        qkg = jnp.zeros((8, d), F32)
        qkg = qkg.at[0].set(jnp.tile(sb_q_norm_g[j], SB_HEADS)).at[1].set(jnp.tile(sb_k_norm_g[j], SB_HEADS))
        wts["sb"].append({"win": sb_w_in[j].astype(BF16), "qkg": qkg, "wout": sb_w_out[j].astype(BF16)})
    for j in range(ret_w_in.shape[0]):
        wts["ret"].append({"win": ret_w_in[j].astype(BF16),
                           "norm_g": ret_norm_g[j].reshape(1, RET_HEADS * RET_DV),
                           "wout": ret_w_out[j].astype(BF16)})
    for i in range(DEPTH):
        wts["ffn"].append({"win": ffn_w_in[i].astype(BF16), "conv_w": ffn_conv_w[i],
                           "conv_b": ffn_conv_b[i].reshape(1, 2 * D_FF), "wout": ffn_w_out[i].astype(BF16)})

    nrow = bp + bs
    rows = -(-nrow // 8) * 8
    c_all = jnp.pad(jnp.concatenate([c_prompt, c_sample], axis=0), ((0, rows - nrow), (0, 0)))
    mod = _ada_mod(c_all, ada_w, ada_b)
    mod = jnp.pad(mod.reshape(DEPTH, rows, 6, d), ((0, 0), (0, 0), (0, 2), (0, 0)))
    mod_p, mod_s = mod[:, :bp], mod[:, bp:nrow]

    dt = x_prompt.dtype
    zero_states = [(jnp.zeros((bp, GDN_HEADS, GDN_DK, GDN_DK), dt), jnp.zeros((bp, 3, GDN_QKV), dt)),
                   (None, None),
                   (jnp.zeros((bp, RET_HEADS, RET_DK, RET_DV), dt),),
                   (jnp.zeros((bp, GDN_HEADS, GDN_DK, GDN_DK), dt), jnp.zeros((bp, 3, GDN_QKV), dt))]
    zero_ffn = jnp.zeros((DEPTH, bp, 2, 2 * D_FF), dt)
    y_prompt, p_states, p_ffn_conv = _run_group(x_prompt, mod_p, zero_states, zero_ffn, 0, wts)

    past_len = cache_l1_sb_k.shape[1]
    sample_states = [(state_l0_gdn_S, state_l0_gdn_conv),
                     (cache_l1_sb_k.reshape(bs, past_len, d), cache_l1_sb_v.reshape(bs, past_len, d)),
                     (state_l2_ret,), (state_l3_gdn_S, state_l3_gdn_conv)]
    y_sample, s_states, s_ffn_conv = _run_group(x_sample, mod_s, sample_states, state_ffn_conv, past_len, wts)

    (p_l0_S, p_l0_conv), (p_l1_k, p_l1_v), (p_l2_R,), (p_l3_S, p_l3_conv) = p_states
    (s_l0_S, s_l0_conv), (s_l1_k, s_l1_v), (s_l2_R,), (s_l3_S, s_l3_conv) = s_states
    return (y_prompt, y_sample,
            p_l0_S, p_l0_conv, p_l1_k, p_l1_v, p_l2_R, p_l3_S, p_l3_conv, p_ffn_conv,
            s_l0_S, s_l0_conv, s_l1_k, s_l1_v, s_l2_R, s_l3_S, s_l3_conv, s_ffn_conv)
```

```python
import functools
import math

import jax
import jax.numpy as jnp
from jax import lax
from jax.experimental import pallas as pl
from jax.experimental.pallas import tpu as pltpu

F32 = jnp.float32
BF16 = jnp.bfloat16

D_MODEL = 1024
DEPTH = 4
CHUNK = 64
GDN_HEADS = 8
GDN_DK = 128
GDN_QKV = 3 * GDN_HEADS * GDN_DK
SB_HEADS = 16
SB_DH = 64
RET_HEADS = 4
RET_DK = 256
RET_DV = 512
RET_ROPE_BASE = 10000.0
D_FF = 2816
NORM_EPS = 1e-6

LANES = 128
MXU_N = 256
VMEM_LIMIT = 56 * 1024 * 1024

ROW_TILE = 512
GDN_TILE = 256
RET_CHUNK = 256
SB_TILE = 256
SB_DEAD = 120.0


def _cparams(*sem):
    return pltpu.CompilerParams(dimension_semantics=sem, vmem_limit_bytes=VMEM_LIMIT)


def _wspec(shape):
    return pl.BlockSpec(shape, lambda i, j: (0,) * len(shape), pipeline_mode=pl.Buffered(1))


def _dot(a, b):
    return jnp.dot(a.astype(BF16), b.astype(BF16), preferred_element_type=F32)


def _dot_nt(a, b):
    return lax.dot_general(a.astype(BF16), b.astype(BF16), (((1,), (1,)), ((), ())),
                           preferred_element_type=F32)


def _dot_tn(a, b):
    return lax.dot_general(a.astype(BF16), b.astype(BF16), (((0,), (0,)), ((), ())),
                           preferred_element_type=F32)


def _split(x):
    hi = x.astype(BF16)
    lo = (x - hi.astype(F32)).astype(BF16)
    return hi, lo


def _dot_sel(a, sel):
    hi, lo = _split(a)
    return (jnp.dot(hi, sel, preferred_element_type=F32)
            + jnp.dot(lo, sel, preferred_element_type=F32))


def _sel_dot(sel, b):
    hi, lo = _split(b)
    return (jnp.dot(sel, hi, preferred_element_type=F32)
            + jnp.dot(sel, lo, preferred_element_type=F32))


def _dot3(a, b):
    ah, al = _split(a)
    bh, bl = _split(b)
    return (jnp.dot(ah, bh, preferred_element_type=F32)
            + jnp.dot(ah, bl, preferred_element_type=F32)
            + jnp.dot(al, bh, preferred_element_type=F32))


def _sigmoid(x):
    return 1.0 / (1.0 + jnp.exp(-x))


def _silu(x):
    return x * _sigmoid(x)


def _softplus(x):
    return jnp.maximum(x, 0.0) + jnp.log(1.0 + jnp.exp(-jnp.abs(x)))


def _norm_mod(x, gain, scale, shift):
    ms = jnp.mean(x * x, axis=-1, keepdims=True)
    y = x * lax.rsqrt(ms + NORM_EPS) * gain
    return y * (1.0 + scale) + shift


def _iota2(shape, dim):
    return lax.broadcasted_iota(jnp.int32, shape, dim)


def _ada_kernel(c_ref, w_ref, b_ref, o_ref):
    o_ref[0] = _dot(_silu(c_ref[...]), w_ref[0]) + b_ref[0]


def _ada_mod(c_all, ada_w, ada_b):
    rows = c_all.shape[0]
    n = ada_w.shape[2]
    tn = 1536
    return pl.pallas_call(
        _ada_kernel,
        grid=(DEPTH, n // tn),
        in_specs=[pl.BlockSpec((rows, D_MODEL), lambda i, j: (0, 0)),
                  pl.BlockSpec((1, D_MODEL, tn), lambda i, j: (i, 0, j)),
                  pl.BlockSpec((1, 1, tn), lambda i, j: (i, 0, j))],
        out_specs=pl.BlockSpec((1, rows, tn), lambda i, j: (i, 0, j)),
        out_shape=jax.ShapeDtypeStruct((DEPTH, rows, n), F32),
        compiler_params=_cparams("arbitrary", "arbitrary"),
        name="ada_mod",
    )(c_all, ada_w, ada_b.reshape(DEPTH, 1, n))


def _causal_conv(u, ext_ref, halo_ref, col0, w_ref, width, nb, tt):
    n = u.shape[-1]
    hw = width - 1
    ext_ref[:, 8:8 + tt, :] = u.reshape(nb, tt, n)
    ext_ref[:, 8 - hw:8, :] = halo_ref[:, 8 - hw:8, col0:col0 + n]
    y = ext_ref[:, 8:8 + tt, :] * w_ref[hw:hw + 1, col0:col0 + n]
    for j in range(hw):
        y = y + ext_ref[:, 8 - hw + j:8 - hw + j + tt, :] * w_ref[j:j + 1, col0:col0 + n]
    halo_ref[:, 8 - hw:8, col0:col0 + n] = ext_ref[:, 8 + tt - hw:8 + tt, :]
    return y


FFN_COLS = 256


def _ffn_kernel(x_ref, mod_ref, ng_ref, win_ref, cw_ref, cb_ref, wout_ref, buf_ref,
                y_ref, nbuf_ref, extg_ref, extv_ref, halo_ref, *, nb, tt):
    m = nb * tt

    @pl.when(pl.program_id(1) == 0)
    def _():
        halo_ref[:, 6:8, :] = buf_ref[...]

    x = x_ref[...]
    h = _norm_mod(x, ng_ref[...], mod_ref[:, 4:5, :], mod_ref[:, 3:4, :])
    h = h.reshape(m, D_MODEL).astype(BF16)
    acc = jnp.zeros((m, D_MODEL), F32)
    for c in range(D_FF // FFN_COLS):
        g0 = c * FFN_COLS
        v0 = D_FF + c * FFN_COLS
        ug = jnp.dot(h, win_ref[:, g0:g0 + FFN_COLS], preferred_element_type=F32)
        uv = jnp.dot(h, win_ref[:, v0:v0 + FFN_COLS], preferred_element_type=F32)
        yg = _causal_conv(ug, extg_ref, halo_ref, g0, cw_ref, 3, nb, tt) + cb_ref[:, g0:g0 + FFN_COLS]
        yv = _causal_conv(uv, extv_ref, halo_ref, v0, cw_ref, 3, nb, tt) + cb_ref[:, v0:v0 + FFN_COLS]
        a = (_silu(yg) * yv).reshape(m, FFN_COLS).astype(BF16)
        acc = acc + jnp.dot(a, wout_ref[g0:g0 + FFN_COLS, :], preferred_element_type=F32)
    y_ref[...] = x + mod_ref[:, 5:6, :] * acc.reshape(nb, tt, D_MODEL)
    nbuf_ref[...] = halo_ref[:, 6:8, :]


def _ffn(x, mod, ng, win, cw, cb, wout, buf, nb, tt):
    b, t, d = x.shape
    n2 = 2 * D_FF
    kern = functools.partial(_ffn_kernel, nb=nb, tt=tt)
    return pl.pallas_call(
        kern,
        grid=(b // nb, t // tt),
        in_specs=[pl.BlockSpec((nb, tt, d), lambda i, j: (i, j, 0)),
                  pl.BlockSpec((nb, 8, d), lambda i, j: (i, 0, 0)),
                  pl.BlockSpec((1, d), lambda i, j: (0, 0)),
                  _wspec((d, n2)),
                  pl.BlockSpec((3, n2), lambda i, j: (0, 0)),
                  pl.BlockSpec((1, n2), lambda i, j: (0, 0)),
                  _wspec((D_FF, d)),
                  pl.BlockSpec((nb, 2, n2), lambda i, j: (i, 0, 0))],
        out_specs=[pl.BlockSpec((nb, tt, d), lambda i, j: (i, j, 0)),
                   pl.BlockSpec((nb, 2, n2), lambda i, j: (i, 0, 0))],
        out_shape=[jax.ShapeDtypeStruct((b, t, d), F32),
                   jax.ShapeDtypeStruct((b, 2, n2), F32)],
        scratch_shapes=[pltpu.VMEM((nb, 8 + tt, FFN_COLS), F32),
                        pltpu.VMEM((nb, 8 + tt, FFN_COLS), F32),
                        pltpu.VMEM((nb, 8, n2), F32)],
        compiler_params=_cparams("arbitrary", "arbitrary"),
        name="conv_ffn",
    )(x, mod, ng, win, cw, cb, wout, buf)


def _gdn_proj_kernel(x_ref, mod_ref, ng_ref, wqkv_ref, wz_ref, wab_ref, cw_ref, pv_ref, buf_ref,
                     q_ref, k_ref, v_ref, z_ref, gb_ref, nbuf_ref, ext_ref, halo_ref, *, nb, tt):
    m = nb * tt
    d = D_MODEL

    @pl.when(pl.program_id(1) == 0)
    def _():
        halo_ref[:, 5:8, :] = buf_ref[...]

    h = _norm_mod(x_ref[...], ng_ref[...], mod_ref[:, 1:2, :], mod_ref[:, 0:1, :])
    h = h.reshape(m, d).astype(BF16)
    z_ref[...] = jnp.dot(h, wz_ref[...], preferred_element_type=F32).reshape(nb, tt, d)
    ab = jnp.dot(h, wab_ref[...], preferred_element_type=F32)
    lane = _iota2(ab.shape, 1)
    gb = jnp.where(lane < GDN_HEADS,
                   pv_ref[0:1, :] * _softplus(ab + pv_ref[1:2, :]),
                   _sigmoid(ab))
    gb_ref[...] = gb.reshape(nb, tt, LANES)
    outs = (q_ref, k_ref, v_ref)
    for s in range(GDN_QKV // MXU_N):
        c0 = s * MXU_N
        u = jnp.dot(h, wqkv_ref[:, c0:c0 + MXU_N], preferred_element_type=F32)
        y = _silu(_causal_conv(u, ext_ref, halo_ref, c0, cw_ref, 4, nb, tt))
        which, o0 = divmod(c0, d)
        if which < 2:
            halves = []
            for hh in range(MXU_N // GDN_DK):
                yh = y[:, :, hh * GDN_DK:(hh + 1) * GDN_DK]
                r = lax.rsqrt(jnp.sum(yh * yh, axis=-1, keepdims=True) + NORM_EPS)
                if which == 0:
                    r = r * (GDN_DK ** -0.5)
                halves.append(yh * r)
            y = jnp.concatenate(halves, axis=-1)
        outs[which][:, :, o0:o0 + MXU_N] = y
    nbuf_ref[...] = halo_ref[:, 5:8, :]


def _gdn_proj(x, mod, ng, wqkv, wz, wab, cw, pv, buf, nb, tt):
    b, t, d = x.shape
    kern = functools.partial(_gdn_proj_kernel, nb=nb, tt=tt)
    row = lambda i, j: (i, j, 0)
    const2 = lambda i, j: (0, 0)
    act = jax.ShapeDtypeStruct((b, t, d), F32)
    return pl.pallas_call(
        kern,
        grid=(b // nb, t // tt),
        in_specs=[pl.BlockSpec((nb, tt, d), row),
                  pl.BlockSpec((nb, 8, d), lambda i, j: (i, 0, 0)),
                  pl.BlockSpec((1, d), const2),
                  _wspec((d, GDN_QKV)),
                  _wspec((d, d)),
                  pl.BlockSpec((d, LANES), const2),
                  pl.BlockSpec((4, GDN_QKV), const2),
                  pl.BlockSpec((8, LANES), const2),
                  pl.BlockSpec((nb, 3, GDN_QKV), lambda i, j: (i, 0, 0))],
        out_specs=[pl.BlockSpec((nb, tt, d), row)] * 4
                  + [pl.BlockSpec((nb, tt, LANES), row),
                     pl.BlockSpec((nb, 3, GDN_QKV), lambda i, j: (i, 0, 0))],
        out_shape=[act, act, act, act,
                   jax.ShapeDtypeStruct((b, t, LANES), F32),
                   jax.ShapeDtypeStruct((b, 3, GDN_QKV), F32)],
        scratch_shapes=[pltpu.VMEM((nb, 8 + tt, MXU_N), F32),
                        pltpu.VMEM((nb, 8, GDN_QKV), F32)],
        compiler_params=_cparams("arbitrary", "arbitrary"),
        name="gdn_proj",
    )(x, mod, ng, wqkv, wz, wab, cw, pv, buf)


def _tri_inverses(lmats, chunk):
    n = lmats[0].shape[0]
    row = _iota2((n, n), 0)
    col = _iota2((n, n), 1)
    level = 31 - lax.clz(jnp.bitwise_xor(row, col))
    eye = (row == col).astype(F32)
    xs = [eye - jnp.where(level == 0, lm, 0.0) for lm in lmats]
    k = 1
    while (1 << k) < chunk:
        ps = [_dot(x, jnp.where(level == k, lm, 0.0)) for x, lm in zip(xs, lmats)]
        xs = [x - _dot(p, x) for x, p in zip(xs, ps)]
        k += 1
    return xs


def _gdn_core_kernel(q_ref, k_ref, v_ref, z_ref, gb_ref, x_ref, mod_ref, og_ref, wout_ref, s0_ref,
                     y_ref, sout_ref, s_ref, o_ref, *, tt, chunk):
    @pl.when(pl.program_id(1) == 0)
    def _():
        s_ref[...] = s0_ref[0]

    nck = tt // chunk
    row = _iota2((tt, tt), 0)
    col = _iota2((tt, tt), 1)
    if nck > 1:
        same = (row // chunk) == (col // chunk)
        incl = (row >= col) & same
        strict = (row > col) & same
    else:
        same = None
        incl = row >= col
        strict = row > col
    gb = gb_ref[0]
    gcol = _sel_dot(incl.astype(BF16), gb)
    if nck > 1:
        glast = _sel_dot(same.astype(BF16), gb)
    else:
        glast = jnp.broadcast_to(jnp.sum(gb, axis=0, keepdims=True), gb.shape)
    grow = gcol.T
    eg = jnp.exp(gcol)
    ekd = jnp.exp(glast - gcol)
    egl = jnp.exp(glast)

    heads = range(GDN_HEADS)
    cols = [slice(hd * GDN_DK, (hd + 1) * GDN_DK) for hd in heads]
    kbs, lmats, qks = [], [], []
    for hd in heads:
        kh = k_ref[0, :, cols[hd]]
        diff = gcol[:, hd:hd + 1] - grow[hd:hd + 1, :]
        decay = jnp.exp(jnp.where(incl, diff, -jnp.inf))
        kb = kh * gb[:, GDN_HEADS + hd:GDN_HEADS + hd + 1]
        kbs.append(kb)
        lmats.append(jnp.where(strict, _dot_nt(kb, kh) * decay, 0.0))
        qks.append((_dot_nt(q_ref[0, :, cols[hd]], kh) * decay).astype(BF16))
    tinvs = [t.astype(BF16) for t in _tri_inverses(lmats, chunk)]
    us, ws = [], []
    for hd in heads:
        beta = gb[:, GDN_HEADS + hd:GDN_HEADS + hd + 1]
        us.append(_dot(tinvs[hd], v_ref[0, :, cols[hd]] * beta))
        ws.append(_dot(tinvs[hd], kbs[hd] * eg[:, hd:hd + 1]).astype(BF16))
    ss = [s_ref[hd] for hd in heads]
    vnews = [[] for _ in heads]
    ocross = [[] for _ in heads]
    for c in range(nck):
        rows = slice(c * chunk, (c + 1) * chunk)
        for hd in heads:
            sb = ss[hd].astype(BF16)
            vn = us[hd][rows] - _dot(ws[hd][rows], sb)
            ocross[hd].append(_dot(q_ref[0, rows, cols[hd]] * eg[rows, hd:hd + 1], sb))
            kd = k_ref[0, rows, cols[hd]] * ekd[rows, hd:hd + 1]
            ss[hd] = ss[hd] * egl[c * chunk:c * chunk + 1, hd:hd + 1] + _dot_tn(kd, vn)
            vnews[hd].append(vn.astype(BF16))
    for hd in heads:
        s_ref[hd] = ss[hd]
        vnew = jnp.concatenate(vnews[hd], axis=0) if nck > 1 else vnews[hd][0]
        oc = jnp.concatenate(ocross[hd], axis=0) if nck > 1 else ocross[hd][0]
        o = oc + _dot(qks[hd], vnew)
        o = o * lax.rsqrt(jnp.mean(o * o, axis=-1, keepdims=True) + NORM_EPS) * og_ref[...]
        o_ref[:, cols[hd]] = (o * _silu(z_ref[0, :, cols[hd]])).astype(BF16)

    out = jnp.dot(o_ref[...], wout_ref[...], preferred_element_type=F32)
    y_ref[0] = x_ref[0] + mod_ref[0, 2:3, :] * out
    sout_ref[0] = s_ref[...]


def _gdn_core(q, k, v, z, gb, x, mod, og, wout, s0, tt, chunk):
    b, t, d = x.shape
    kern = functools.partial(_gdn_core_kernel, tt=tt, chunk=chunk)
    row = lambda i, j: (i, j, 0)
    return pl.pallas_call(
        kern,
        grid=(b, t // tt),
        in_specs=[pl.BlockSpec((1, tt, d), row)] * 4
                 + [pl.BlockSpec((1, tt, LANES), row),
                    pl.BlockSpec((1, tt, d), row),
                    pl.BlockSpec((1, 8, d), lambda i, j: (i, 0, 0)),
                    pl.BlockSpec((1, GDN_DK), lambda i, j: (0, 0)),
                    _wspec((d, d)),
                    pl.BlockSpec((1, GDN_HEADS, GDN_DK, GDN_DK), lambda i, j: (i, 0, 0, 0))],
        out_specs=[pl.BlockSpec((1, tt, d), row),
                   pl.BlockSpec((1, GDN_HEADS, GDN_DK, GDN_DK), lambda i, j: (i, 0, 0, 0))],
        out_shape=[jax.ShapeDtypeStruct((b, t, d), F32),
                   jax.ShapeDtypeStruct((b, GDN_HEADS, GDN_DK, GDN_DK), F32)],
        scratch_shapes=[pltpu.VMEM((GDN_HEADS, GDN_DK, GDN_DK), F32),
                        pltpu.VMEM((tt, d), BF16)],
        compiler_params=_cparams("arbitrary", "arbitrary"),
        name="gdn_core",
    )(q, k, v, z, gb, x, mod, og, wout, s0)


def _sb_proj_kernel(x_ref, mod_ref, ng_ref, w_ref, qkg_ref,
                    kf_ref, vf_ref, qb_ref, kb_ref, vb_ref, *, nb, tt):
    m = nb * tt
    d = D_MODEL
    h = _norm_mod(x_ref[...], ng_ref[...], mod_ref[:, 1:2, :], mod_ref[:, 0:1, :])
    h = h.reshape(m, d).astype(BF16)
    grp = ((_iota2((MXU_N, MXU_N), 0) // SB_DH) == (_iota2((MXU_N, MXU_N), 1) // SB_DH)).astype(BF16)
    for s in range(3 * d // MXU_N):
        c0 = s * MXU_N
        which, o0 = divmod(c0, d)
        u = jnp.dot(h, w_ref[:, c0:c0 + MXU_N], preferred_element_type=F32)
        if which < 2:
            ms = _dot_sel(u * u, grp) * (1.0 / SB_DH)
            u = u * lax.rsqrt(ms + NORM_EPS) * qkg_ref[which:which + 1, o0:o0 + MXU_N]
        u3 = u.reshape(nb, tt, MXU_N)
        if which == 0:
            qb_ref[:, :, o0:o0 + MXU_N] = u3.astype(BF16)
        elif which == 1:
            kf_ref[:, :, o0:o0 + MXU_N] = u3
            kb_ref[:, :, o0:o0 + MXU_N] = u3.astype(BF16)
        else:
            vf_ref[:, :, o0:o0 + MXU_N] = u3
            vb_ref[:, :, o0:o0 + MXU_N] = u3.astype(BF16)


def _sb_proj(x, mod, ng, w, qkg, nb, tt):
    b, t, d = x.shape
    kern = functools.partial(_sb_proj_kernel, nb=nb, tt=tt)
    row = lambda i, j: (i, j, 0)
    const2 = lambda i, j: (0, 0)
    f = jax.ShapeDtypeStruct((b, t, d), F32)
    h = jax.ShapeDtypeStruct((b, t, d), BF16)
    return pl.pallas_call(
        kern,
        grid=(b // nb, t // tt),
        in_specs=[pl.BlockSpec((nb, tt, d), row),
                  pl.BlockSpec((nb, 8, d), lambda i, j: (i, 0, 0)),
                  pl.BlockSpec((1, d), const2),
                  _wspec((d, 3 * d)),
                  pl.BlockSpec((8, d), const2)],
        out_specs=[pl.BlockSpec((nb, tt, d), row)] * 5,
        out_shape=[f, f, h, h, h],
        compiler_params=_cparams("arbitrary", "arbitrary"),
        name="sb_proj",
    )(x, mod, ng, w, qkg)


def _suffix_selector(bw):
    r = jnp.bitwise_and(_iota2((2 * bw, bw + LANES), 0), bw - 1)
    c = _iota2((2 * bw, bw + LANES), 1)
    return jnp.where(c < bw, (r > c).astype(F32), 1.0).astype(BF16)


def _dot_nt_bf(a, b):
    return lax.dot_general(a, b, (((1,), (1,)), ((), ())), preferred_element_type=F32)


def _sb_tile(qh, kt, vt, carry, acc, mask, sel):
    n = kt.shape[0]
    bw = min(LANES, n)
    z = _dot_nt_bf(qh, kt)
    sp = jnp.maximum(z, 0.0) + jnp.log(1.0 + jnp.exp(-jnp.abs(z)))
    spm = sp if mask is None else jnp.where(mask, sp, 0.0)
    ws = []
    for mb in reversed(range(n // bw)):
        sl = slice(mb * bw, (mb + 1) * bw)
        hi = spm[:, sl].astype(BF16)
        lo = (spm[:, sl] - hi.astype(F32)).astype(BF16)
        r = jnp.dot(jnp.concatenate([hi, lo], axis=1), sel[bw], preferred_element_type=F32)
        w = jnp.exp(z[:, sl] - sp[:, sl] - carry[:, :bw] - r[:, :bw])
        if mask is not None:
            w = jnp.where(mask[:, sl], w, 0.0)
        ws.append(w.astype(BF16))
        carry = carry + r[:, bw:]
    wts = jnp.concatenate(ws[::-1], axis=-1) if len(ws) > 1 else ws[0]
    acc = acc + jnp.dot(wts, vt, preferred_element_type=F32)
    return carry, acc


def _sb_head_queries(q):
    q = q * (SB_DH ** -0.5)
    lane = _iota2(q.shape, 1)
    return [jnp.where((lane // SB_DH) == hd, q, jnp.zeros_like(q)) for hd in range(2)]


def _sb_fast_kernel(q_ref, kd_ref, vd_ref, kp_ref, vp_ref, o_ref, need_ref, st_ref, *,
                    tq, prev_is_self, more_before_prev):
    j = pl.program_id(1)
    tp = kp_ref.shape[1]
    npair = D_MODEL // LANES
    sel = {bw: _suffix_selector(bw) for bw in {min(LANES, tq), min(LANES, tp)}}
    dmask = _iota2((tq, tq), 1) < _iota2((tq, tq), 0)
    zero = jnp.zeros((tq, LANES), F32)
    pairs = [slice(hp * LANES, (hp + 1) * LANES) for hp in range(npair)]

    for hp in range(npair):
        qhs = _sb_head_queries(q_ref[0, :, pairs[hp]])
        for hd in range(2):
            c, a = _sb_tile(qhs[hd], kd_ref[0, :, pairs[hp]], vd_ref[0, :, pairs[hp]], zero, zero, dmask, sel)
            st_ref[hp, 2 * hd] = c
            st_ref[hp, 2 * hd + 1] = a

    def prev_tile():
        for hp in range(npair):
            qhs = _sb_head_queries(q_ref[0, :, pairs[hp]])
            for hd in range(2):
                c, a = _sb_tile(qhs[hd], kp_ref[0, :, pairs[hp]], vp_ref[0, :, pairs[hp]],
                                st_ref[hp, 2 * hd], st_ref[hp, 2 * hd + 1], None, sel)
                st_ref[hp, 2 * hd] = c
                st_ref[hp, 2 * hd + 1] = a

    if prev_is_self:
        pl.when(j > 0)(prev_tile)
    else:
        prev_tile()

    lane = _iota2((tq, LANES), 1)
    cmin = None
    for hp in range(npair):
        o_ref[0, :, pairs[hp]] = jnp.where((lane // SB_DH) == 0, st_ref[hp, 1], st_ref[hp, 3]).astype(BF16)
        c = jnp.minimum(st_ref[hp, 0], st_ref[hp, 2])
        cmin = c if cmin is None else jnp.minimum(cmin, c)
    alive = jnp.where(jnp.min(cmin, axis=0, keepdims=True) < SB_DEAD, 1.0, 0.0)
    if prev_is_self:
        alive = jnp.where(j > 1, alive, 0.0)
    elif not more_before_prev:
        alive = jnp.zeros_like(alive)
    need_ref[0, 0] = jnp.broadcast_to(alive, (8, LANES))


def _sb_fast(qb, kb, vb, prev_k, prev_v, tq, prev_is_self, more_before_prev):
    b, t, d = qb.shape
    tp = tq if prev_is_self else prev_k.shape[1]
    kern = functools.partial(_sb_fast_kernel, tq=tq, prev_is_self=prev_is_self,
                             more_before_prev=more_before_prev)
    cur = pl.BlockSpec((1, tq, d), lambda i, j: (i, j, 0))
    if prev_is_self:
        prev = pl.BlockSpec((1, tp, d), lambda i, j: (i, jnp.maximum(j - 1, 0), 0))
    else:
        prev = pl.BlockSpec((1, tp, d), lambda i, j: (i, 0, 0))
    return pl.pallas_call(
        kern,
        grid=(b, t // tq),
        in_specs=[cur, cur, cur, prev, prev],
        out_specs=[cur, pl.BlockSpec((1, 1, 8, LANES), lambda i, j: (i, j, 0, 0))],
        out_shape=[jax.ShapeDtypeStruct((b, t, d), BF16),
                   jax.ShapeDtypeStruct((b, t // tq, 8, LANES), F32)],
        scratch_shapes=[pltpu.VMEM((d // LANES, 4, tq, LANES), F32)],
        compiler_params=_cparams("arbitrary", "arbitrary"),
        name="sb_fast",
    )(qb, kb, vb, prev_k, prev_v)


def _sb_attention(qb, kb, vb, past_k, past_v, tq):
    b, t, d = qb.shape
    if past_k is None:
        o, need = _sb_fast(qb, kb, vb, kb, vb, tq, True, False)
        full = lambda: _sb_attn(qb, kb, vb, None, None, tq)
    else:
        plen = past_k.shape[1]
        tp = min(256, plen)
        tail = lambda c: c[:, plen - tp:].reshape(b, tp, d).astype(BF16)
        o, need = _sb_fast(qb, kb, vb, tail(past_k), tail(past_v), tq, False, plen > tp)
        full = lambda: _sb_attn(qb, kb, vb, past_k.reshape(b, plen, d), past_v.reshape(b, plen, d), tq)
    return lax.cond(jnp.max(need) > 0.0, full, lambda: o)


def _sb_attn_kernel(*refs, tq, tk, past_len):
    if past_len:
        q_ref, k_ref, v_ref, pk_ref, pv_ref, o_ref = refs
    else:
        q_ref, k_ref, v_ref, o_ref = refs
        pk_ref = pv_ref = None
    qi = pl.program_id(2)
    lane = _iota2((tq, LANES), 1)
    qhs = _sb_head_queries(q_ref[0])
    sel = {bw: _suffix_selector(bw) for bw in {min(LANES, tq), min(LANES, tk)}}

    def both(kt, vt, state, mask):
        c0, a0 = _sb_tile(qhs[0], kt, vt, state[0], state[1], mask, sel)
        c1, a1 = _sb_tile(qhs[1], kt, vt, state[2], state[3], mask, sel)
        return c0, a0, c1, a1

    zero = jnp.zeros((tq, LANES), F32)
    r0 = pl.multiple_of(qi * tq, tq)
    dmask = _iota2((tq, tq), 1) < _iota2((tq, tq), 0)
    state = both(k_ref[0, pl.ds(r0, tq), :], v_ref[0, pl.ds(r0, tq), :], (zero, zero, zero, zero), dmask)

    def alive(st):
        return jnp.min(jnp.minimum(st[0], st[2])) < SB_DEAD

    def key_loop(ntiles, load, st):
        def cond(c):
            return jnp.logical_and(c[0] < ntiles, c[1])

        def body(c):
            kt, vt = load(pl.multiple_of((ntiles - 1 - c[0]) * tk, tk))
            nst = both(kt, vt, c[2:], None)
            return (c[0] + 1, alive(nst)) + nst

        return lax.while_loop(cond, body, (jnp.int32(0), alive(st)) + st)[2:]

    state = key_loop(qi, lambda k0: (k_ref[0, pl.ds(k0, tk), :], v_ref[0, pl.ds(k0, tk), :]), state)
    if past_len:
        state = key_loop(past_len // tk,
                         lambda k0: (pk_ref[0, pl.ds(k0, tk), :].astype(BF16),
                                     pv_ref[0, pl.ds(k0, tk), :].astype(BF16)), state)
    o_ref[0] = jnp.where((lane // SB_DH) == 0, state[1], state[3]).astype(BF16)


def _sb_attn(qb, kb, vb, past_k, past_v, tq):
    b, t, d = qb.shape
    past_len = 0 if past_k is None else past_k.shape[1]
    tk = tq if not past_len else min(256, past_len)
    kern = functools.partial(_sb_attn_kernel, tq=tq, tk=tk, past_len=past_len)
    qspec = pl.BlockSpec((1, tq, LANES), lambda i, h, j: (i, j, h))
    kvspec = pl.BlockSpec((1, t, LANES), lambda i, h, j: (i, 0, h))
    in_specs = [qspec, kvspec, kvspec]
    args = [qb, kb, vb]
    if past_len:
        pspec = pl.BlockSpec((1, past_len, LANES), lambda i, h, j: (i, 0, h))
        in_specs += [pspec, pspec]
        args += [past_k, past_v]
    return pl.pallas_call(
        kern,
        grid=(b, d // LANES, t // tq),
        in_specs=in_specs,
        out_specs=qspec,
        out_shape=jax.ShapeDtypeStruct((b, t, d), BF16),
        compiler_params=_cparams("arbitrary", "arbitrary", "arbitrary"),
        name="sb_attn",
    )(*args)


def _out_proj_kernel(o_ref, x_ref, mod_ref, w_ref, y_ref, *, nb, tt):
    out = jnp.dot(o_ref[...].reshape(nb * tt, -1), w_ref[...], preferred_element_type=F32)
    y_ref[...] = x_ref[...] + mod_ref[:, 2:3, :] * out.reshape(nb, tt, D_MODEL)


def _out_proj(o, x, mod, w, nb, tt):
    b, t, d = x.shape
    kin = o.shape[-1]
    kern = functools.partial(_out_proj_kernel, nb=nb, tt=tt)
    row = lambda i, j: (i, j, 0)
    return pl.pallas_call(
        kern,
        grid=(b // nb, t // tt),
        in_specs=[pl.BlockSpec((nb, tt, kin), row),
                  pl.BlockSpec((nb, tt, d), row),
                  pl.BlockSpec((nb, 8, d), lambda i, j: (i, 0, 0)),
                  _wspec((kin, d))],
        out_specs=pl.BlockSpec((nb, tt, d), row),
        out_shape=jax.ShapeDtypeStruct((b, t, d), F32),
        compiler_params=_cparams("arbitrary", "arbitrary"),
        name="out_proj",
    )(o, x, mod, w)


def _ret_log_gamma(hd):
    return math.log1p(-(2.0 ** (-5.0 - hd)))


def _ret_proj_kernel(x_ref, mod_ref, ng_ref, w_ref, cos_ref, sin_ref,
                     q_ref, qd_ref, k_ref, kd_ref, v_ref, gt_ref, *, nb, tt, chunk):
    m = nb * tt
    d = D_MODEL
    hk = RET_HEADS * RET_DK
    hv = RET_HEADS * RET_DV
    h = _norm_mod(x_ref[...], ng_ref[...], mod_ref[:, 1:2, :], mod_ref[:, 0:1, :])
    h = h.reshape(m, d).astype(BF16)
    cos = cos_ref[...][None]
    sin = sin_ref[...][None]
    tpos = pl.program_id(1) * tt + _iota2((1, tt, 1), 1)
    assert chunk & (chunk - 1) == 0
    idx = jnp.bitwise_and(tpos, chunk - 1).astype(F32)
    half = RET_DK // 2
    for s in range(2 * hk // MXU_N):
        c0 = s * MXU_N
        which, o0 = divmod(c0, hk)
        hd = o0 // RET_DK
        lg = _ret_log_gamma(hd)
        u = jnp.dot(h, w_ref[:, c0:c0 + MXU_N], preferred_element_type=F32).reshape(nb, tt, MXU_N)
        x1 = u[:, :, :half]
        x2 = u[:, :, half:]
        r = jnp.concatenate([x1 * cos - x2 * sin, x1 * sin + x2 * cos], axis=-1)
        if which == 0:
            q_ref[:, :, o0:o0 + MXU_N] = r.astype(BF16)
            qd_ref[:, :, o0:o0 + MXU_N] = (r * jnp.exp(lg * (idx + 1.0))).astype(BF16)
        else:
            r = r * (RET_DK ** -0.5)
            k_ref[:, :, o0:o0 + MXU_N] = r.astype(BF16)
            kd_ref[:, :, o0:o0 + MXU_N] = (r * jnp.exp(lg * (chunk - 1.0 - idx))).astype(BF16)
    for s in range(2 * hv // MXU_N):
        c0 = 2 * hk + s * MXU_N
        u = jnp.dot(h, w_ref[:, c0:c0 + MXU_N], preferred_element_type=F32).reshape(nb, tt, MXU_N)
        o0 = s * MXU_N
        if o0 < hv:
            v_ref[:, :, o0:o0 + MXU_N] = u.astype(BF16)
        else:
            gt_ref[:, :, o0 - hv:o0 - hv + MXU_N] = u


def _ret_proj(x, mod, ng, w, cos, sin, nb, tt, chunk):
    b, t, d = x.shape
    hk = RET_HEADS * RET_DK
    hv = RET_HEADS * RET_DV
    kern = functools.partial(_ret_proj_kernel, nb=nb, tt=tt, chunk=chunk)
    row = lambda i, j: (i, j, 0)
    const2 = lambda i, j: (0, 0)
    qs = jax.ShapeDtypeStruct((b, t, hk), BF16)
    return pl.pallas_call(
        kern,
        grid=(b // nb, t // tt),
        in_specs=[pl.BlockSpec((nb, tt, d), row),
                  pl.BlockSpec((nb, 8, d), lambda i, j: (i, 0, 0)),
                  pl.BlockSpec((1, d), const2),
                  _wspec((d, 2 * hk + 2 * hv)),
                  pl.BlockSpec((tt, RET_DK // 2), lambda i, j: (j, 0)),
                  pl.BlockSpec((tt, RET_DK // 2), lambda i, j: (j, 0))],
        out_specs=[pl.BlockSpec((nb, tt, hk), row)] * 4
                  + [pl.BlockSpec((nb, tt, hv), row)] * 2,
        out_shape=[qs, qs, qs, qs,
                   jax.ShapeDtypeStruct((b, t, hv), BF16),
                   jax.ShapeDtypeStruct((b, t, hv), F32)],
        compiler_params=_cparams("arbitrary", "arbitrary"),
        name="ret_proj",
    )(x, mod, ng, w, cos, sin)


def _ret_core_kernel(q_ref, qd_ref, k_ref, kd_ref, v_ref, gt_ref, x_ref, mod_ref, og_ref, wout_ref, r0_ref,
                     y_ref, rout_ref, r_ref, o_ref, *, chunk):
    @pl.when(pl.program_id(1) == 0)
    def _():
        r_ref[...] = r0_ref[0]

    rel = (_iota2((chunk, chunk), 0) - _iota2((chunk, chunk), 1)).astype(F32)
    for hd in range(RET_HEADS):
        lg = _ret_log_gamma(hd)
        k0 = hd * RET_DK
        v0 = hd * RET_DV
        dmask = jnp.where(rel >= 0, jnp.exp(lg * jnp.maximum(rel, 0.0)), 0.0)
        vh = v_ref[0, :, v0:v0 + RET_DV]
        s = _dot_nt_bf(q_ref[0, :, k0:k0 + RET_DK], k_ref[0, :, k0:k0 + RET_DK]) * dmask
        r = r_ref[hd]
        o = (jnp.dot(s.astype(BF16), vh, preferred_element_type=F32)
             + jnp.dot(qd_ref[0, :, k0:k0 + RET_DK], r.astype(BF16), preferred_element_type=F32))
        r_ref[hd] = r * math.exp(lg * chunk) + lax.dot_general(
            kd_ref[0, :, k0:k0 + RET_DK], vh, (((0,), (0,)), ((), ())), preferred_element_type=F32)
        mu = jnp.mean(o, axis=-1, keepdims=True)
        dlt = o - mu
        var = jnp.mean(dlt * dlt, axis=-1, keepdims=True)
        o = dlt * lax.rsqrt(var + NORM_EPS) * og_ref[:, v0:v0 + RET_DV]
        o_ref[:, v0:v0 + RET_DV] = (o * _silu(gt_ref[0, :, v0:v0 + RET_DV])).astype(BF16)
    out = jnp.dot(o_ref[...], wout_ref[...], preferred_element_type=F32)
    y_ref[0] = x_ref[0] + mod_ref[0, 2:3, :] * out
    rout_ref[0] = r_ref[...]


def _ret_core(q, qd, k, kd, v, gt, x, mod, og, wout, r0, chunk):
    b, t, d = x.shape
    hk = RET_HEADS * RET_DK
    hv = RET_HEADS * RET_DV
    kern = functools.partial(_ret_core_kernel, chunk=chunk)
    row = lambda i, j: (i, j, 0)
    st = pl.BlockSpec((1, RET_HEADS, RET_DK, RET_DV), lambda i, j: (i, 0, 0, 0))
    return pl.pallas_call(
        kern,
        grid=(b, t // chunk),
        in_specs=[pl.BlockSpec((1, chunk, hk), row)] * 4
                 + [pl.BlockSpec((1, chunk, hv), row)] * 2
                 + [pl.BlockSpec((1, chunk, d), row),
                    pl.BlockSpec((1, 8, d), lambda i, j: (i, 0, 0)),
                    pl.BlockSpec((1, hv), lambda i, j: (0, 0)),
                    _wspec((hv, d)),
                    st],
        out_specs=[pl.BlockSpec((1, chunk, d), row), st],
        out_shape=[jax.ShapeDtypeStruct((b, t, d), F32),
                   jax.ShapeDtypeStruct((b, RET_HEADS, RET_DK, RET_DV), F32)],
        scratch_shapes=[pltpu.VMEM((RET_HEADS, RET_DK, RET_DV), F32),
                        pltpu.VMEM((chunk, hv), BF16)],
        compiler_params=_cparams("arbitrary", "arbitrary"),
        name="ret_core",
    )(q, qd, k, kd, v, gt, x, mod, og, wout, r0)


def _run_group(x, mod_all, states, ffn_bufs, pos0, wts):
    b, t, d = x.shape
    if t >= ROW_TILE:
        nb, tt = 1, ROW_TILE
    else:
        nb, tt = b, t
    new_states, new_ffn = [], []
    for i in range(DEPTH):
        kind, j = i % 3, i // 3
        mod = mod_all[i]
        if kind == 0:
            s0, cbuf = states[i]
            gw = wts["gdn"][j]
            q, k, v, z, gb, ncbuf = _gdn_proj(x, mod, wts["norm_mix_g"][i], gw["wqkv"], gw["wz"], gw["wab"],
                                              gw["conv_w"], gw["pv"], cbuf, nb, tt)
            x, s_new = _gdn_core(q, k, v, z, gb, x, mod, gw["norm_g"], gw["wout"], s0,
                                 min(GDN_TILE, t), min(CHUNK, t))
            new_states.append((s_new, ncbuf))
        elif kind == 1:
            pk, pv = states[i]
            sw = wts["sb"][j]
            kf, vf, qb, kb, vb = _sb_proj(x, mod, wts["norm_mix_g"][i], sw["win"], sw["qkg"], nb, tt)
            o = _sb_attention(qb, kb, vb, pk, pv, min(SB_TILE, t))
            x = _out_proj(o, x, mod, sw["wout"], nb, tt)
            new_states.append((kf.reshape(b, t, SB_HEADS, SB_DH), vf.reshape(b, t, SB_HEADS, SB_DH)))
        else:
            (r0,) = states[i]
            rw = wts["ret"][j]
            chunk = min(RET_CHUNK, t)
            half = RET_DK // 2
            inv_freq = RET_ROPE_BASE ** (-jnp.arange(half, dtype=F32) / half)
            ang = (pos0 + jnp.arange(t)).astype(F32)[:, None] * inv_freq[None, :]
            q, qd, k, kd, v, gt = _ret_proj(x, mod, wts["norm_mix_g"][i], rw["win"], jnp.cos(ang), jnp.sin(ang),
                                            nb, tt, chunk)
            x, r_new = _ret_core(q, qd, k, kd, v, gt, x, mod, rw["norm_g"], rw["wout"], r0, chunk)
            new_states.append((r_new,))
        fw = wts["ffn"][i]
        x, fbuf = _ffn(x, mod, wts["norm_ffn_g"][i], fw["win"], fw["conv_w"], fw["conv_b"], fw["wout"],
                       ffn_bufs[i], nb, tt)
        new_ffn.append(fbuf)
    return x, new_states, jnp.stack(new_ffn)


def kernel(x_prompt, x_sample, state_l0_gdn_S, state_l0_gdn_conv, cache_l1_sb_k, cache_l1_sb_v, state_l2_ret, state_l3_gdn_S, state_l3_gdn_conv, state_ffn_conv, c_prompt, c_sample, ada_w, ada_b, norm_mix_g, norm_ffn_g, gdn_w_in, gdn_conv_w, gdn_A_log, gdn_dt_bias, gdn_norm_g, gdn_w_out, sb_w_in, sb_q_norm_g, sb_k_norm_g, sb_w_out, ret_w_in, ret_norm_g, ret_w_out, ffn_w_in, ffn_conv_w, ffn_conv_b, ffn_w_out):
    d = D_MODEL
    bp, tp, _ = x_prompt.shape
    bs, ts, _ = x_sample.shape

    hv = GDN_HEADS * GDN_DK
    wts = {"norm_mix_g": [norm_mix_g[i].reshape(1, d) for i in range(DEPTH)],
           "norm_ffn_g": [norm_ffn_g[i].reshape(1, d) for i in range(DEPTH)],
           "gdn": [], "sb": [], "ret": [], "ffn": []}
    for j in range(gdn_w_in.shape[0]):
        w = gdn_w_in[j]
        wab = jnp.pad(w[:, GDN_QKV + hv:], ((0, 0), (0, LANES - 2 * GDN_HEADS)))
        pv = jnp.zeros((8, LANES), F32)
        pv = pv.at[0, :GDN_HEADS].set(-jnp.exp(gdn_A_log[j].astype(F32)))
        pv = pv.at[1, :GDN_HEADS].set(gdn_dt_bias[j].astype(F32))
        wts["gdn"].append({"wqkv": w[:, :GDN_QKV].astype(BF16),
                           "wz": w[:, GDN_QKV:GDN_QKV + hv].astype(BF16),
                           "wab": wab.astype(BF16),
                           "conv_w": gdn_conv_w[j], "pv": pv,
                           "norm_g": gdn_norm_g[j].reshape(1, GDN_DK),
                           "wout": gdn_w_out[j].astype(BF16)})
    for j in range(sb_w_in.shape[0]):
        qkg = jnp.zeros((8, d), F32)
        qkg = qkg.at[0].set(jnp.tile(sb_q_norm_g[j], SB_HEADS)).at[1].set(jnp.tile(sb_k_norm_g[j], SB_HEADS))
        wts["sb"].append({"win": sb_w_in[j].astype(BF16), "qkg": qkg, "wout": sb_w_out[j].astype(BF16)})
    for j in range(ret_w_in.shape[0]):
        wts["ret"].append({"win": ret_w_in[j].astype(BF16),
                           "norm_g": ret_norm_g[j].reshape(1, RET_HEADS * RET_DV),
                           "wout": ret_w_out[j].astype(BF16)})
    for i in range(DEPTH):
        wts["ffn"].append({"win": ffn_w_in[i].astype(BF16), "conv_w": ffn_conv_w[i],
                           "conv_b": ffn_conv_b[i].reshape(1, 2 * D_FF), "wout": ffn_w_out[i].astype(BF16)})

    nrow = bp + bs
    rows = -(-nrow // 8) * 8
    c_all = jnp.pad(jnp.concatenate([c_prompt, c_sample], axis=0), ((0, rows - nrow), (0, 0)))
    mod = _ada_mod(c_all, ada_w, ada_b)
    mod = jnp.pad(mod.reshape(DEPTH, rows, 6, d), ((0, 0), (0, 0), (0, 2), (0, 0)))
    mod_p, mod_s = mod[:, :bp], mod[:, bp:nrow]

    dt = x_prompt.dtype
    zero_states = [(jnp.zeros((bp, GDN_HEADS, GDN_DK, GDN_DK), dt), jnp.zeros((bp, 3, GDN_QKV), dt)),
                   (None, None),
                   (jnp.zeros((bp, RET_HEADS, RET_DK, RET_DV), dt),),
                   (jnp.zeros((bp, GDN_HEADS, GDN_DK, GDN_DK), dt), jnp.zeros((bp, 3, GDN_QKV), dt))]
    zero_ffn = jnp.zeros((DEPTH, bp, 2, 2 * D_FF), dt)
    y_prompt, p_states, p_ffn_conv = _run_group(x_prompt, mod_p, zero_states, zero_ffn, 0, wts)

    past_len = cache_l1_sb_k.shape[1]
    sample_states = [(state_l0_gdn_S, state_l0_gdn_conv),
                     (cache_l1_sb_k, cache_l1_sb_v),
                     (state_l2_ret,), (state_l3_gdn_S, state_l3_gdn_conv)]
    y_sample, s_states, s_ffn_conv = _run_group(x_sample, mod_s, sample_states, state_ffn_conv, past_len, wts)

    (p_l0_S, p_l0_conv), (p_l1_k, p_l1_v), (p_l2_R,), (p_l3_S, p_l3_conv) = p_states
    (s_l0_S, s_l0_conv), (s_l1_k, s_l1_v), (s_l2_R,), (s_l3_S, s_l3_conv) = s_states
    return (y_prompt, y_sample,
            p_l0_S, p_l0_conv, p_l1_k, p_l1_v, p_l2_R, p_l3_S, p_l3_conv, p_ffn_conv,
            s_l0_S, s_l0_conv, s_l1_k, s_l1_v, s_l2_R, s_l3_S, s_l3_conv, s_ffn_conv)
```

```python
import functools
import math

import jax
import jax.numpy as jnp
from jax import lax
from jax.experimental import pallas as pl
from jax.experimental.pallas import tpu as pltpu

F32 = jnp.float32
BF16 = jnp.bfloat16

D_MODEL = 1024
DEPTH = 4
CHUNK = 64
GDN_HEADS = 8
GDN_DK = 128
GDN_QKV = 3 * GDN_HEADS * GDN_DK
SB_HEADS = 16
SB_DH = 64
RET_HEADS = 4
RET_DK = 256
RET_DV = 512
RET_ROPE_BASE = 10000.0
D_FF = 2816
NORM_EPS = 1e-6

LANES = 128
MXU_N = 256
VMEM_LIMIT = 56 * 1024 * 1024

ROW_TILE = 512
GDN_TILE = 256
RET_CHUNK = 256
SB_TILE = 256
SB_DEAD = 105.0


def _cparams(*sem):
    return pltpu.CompilerParams(dimension_semantics=sem, vmem_limit_bytes=VMEM_LIMIT)


def _wspec(shape):
    return pl.BlockSpec(shape, lambda i, j: (0,) * len(shape), pipeline_mode=pl.Buffered(1))


def _dot(a, b):
    return jnp.dot(a.astype(BF16), b.astype(BF16), preferred_element_type=F32)


def _dot_nt(a, b):
    return lax.dot_general(a.astype(BF16), b.astype(BF16), (((1,), (1,)), ((), ())),
                           preferred_element_type=F32)


def _dot_tn(a, b):
    return lax.dot_general(a.astype(BF16), b.astype(BF16), (((0,), (0,)), ((), ())),
                           preferred_element_type=F32)


def _split(x):
    hi = x.astype(BF16)
    lo = (x - hi.astype(F32)).astype(BF16)
    return hi, lo


def _dot_sel(a, sel):
    hi, lo = _split(a)
    return (jnp.dot(hi, sel, preferred_element_type=F32)
            + jnp.dot(lo, sel, preferred_element_type=F32))


def _sel_dot(sel, b):
    hi, lo = _split(b)
    return (jnp.dot(sel, hi, preferred_element_type=F32)
            + jnp.dot(sel, lo, preferred_element_type=F32))


def _dot3(a, b):
    ah, al = _split(a)
    bh, bl = _split(b)
    return (jnp.dot(ah, bh, preferred_element_type=F32)
            + jnp.dot(ah, bl, preferred_element_type=F32)
            + jnp.dot(al, bh, preferred_element_type=F32))


def _sigmoid(x):
    return 1.0 / (1.0 + jnp.exp(-x))


def _silu(x):
    return x * _sigmoid(x)


def _softplus(x):
    return jnp.maximum(x, 0.0) + jnp.log(1.0 + jnp.exp(-jnp.abs(x)))


def _norm_mod(x, gain, scale, shift):
    ms = jnp.mean(x * x, axis=-1, keepdims=True)
    y = x * lax.rsqrt(ms + NORM_EPS) * gain
    return y * (1.0 + scale) + shift


def _iota2(shape, dim):
    return lax.broadcasted_iota(jnp.int32, shape, dim)


def _ada_kernel(c_ref, w_ref, b_ref, o_ref):
    o_ref[0] = _dot(_silu(c_ref[...]), w_ref[0]) + b_ref[0]


def _ada_mod(c_all, ada_w, ada_b):
    rows = c_all.shape[0]
    n = ada_w.shape[2]
    tn = 1536
    return pl.pallas_call(
        _ada_kernel,
        grid=(DEPTH, n // tn),
        in_specs=[pl.BlockSpec((rows, D_MODEL), lambda i, j: (0, 0)),
                  pl.BlockSpec((1, D_MODEL, tn), lambda i, j: (i, 0, j)),
                  pl.BlockSpec((1, 1, tn), lambda i, j: (i, 0, j))],
        out_specs=pl.BlockSpec((1, rows, tn), lambda i, j: (i, 0, j)),
        out_shape=jax.ShapeDtypeStruct((DEPTH, rows, n), F32),
        compiler_params=_cparams("arbitrary", "arbitrary"),
        name="ada_mod",
    )(c_all, ada_w, ada_b.reshape(DEPTH, 1, n))


def _conv_put(u, ext_ref, col0, nb, tt):
    n = u.shape[-1]
    ext_ref[:, 8:8 + tt, col0:col0 + n] = u.reshape(nb, tt, n)


def _conv_get(ext_ref, col0, n, w_ref, width, nb, tt):
    hw = width - 1
    cols = slice(col0, col0 + n)
    y = ext_ref[:, 8:8 + tt, cols] * w_ref[hw:hw + 1, cols]
    for j in range(hw):
        y = y + ext_ref[:, 8 - hw + j:8 - hw + j + tt, cols] * w_ref[j:j + 1, cols]
    ext_ref[:, 8 - hw:8, cols] = ext_ref[:, 8 + tt - hw:8 + tt, cols]
    return y


FFN_COLS = 256


def _ffn_kernel(x_ref, mod_ref, ng_ref, win_ref, cw_ref, cb_ref, wout_ref, buf_ref,
                y_ref, nbuf_ref, ext_ref, *, nb, tt):
    m = nb * tt

    @pl.when(pl.program_id(1) == 0)
    def _():
        ext_ref[:, 6:8, :] = buf_ref[...]

    x = x_ref[...]
    h = _norm_mod(x, ng_ref[...], mod_ref[:, 4:5, :], mod_ref[:, 3:4, :])
    h = h.reshape(m, D_MODEL).astype(BF16)
    acc = jnp.zeros((m, D_MODEL), F32)
    nchunk = D_FF // FFN_COLS

    for c in range(nchunk):
        for col0 in (c * FFN_COLS, D_FF + c * FFN_COLS):
            _conv_put(jnp.dot(h, win_ref[:, col0:col0 + FFN_COLS], preferred_element_type=F32),
                      ext_ref, col0, nb, tt)
    for c in range(nchunk):
        g0 = c * FFN_COLS
        v0 = D_FF + c * FFN_COLS
        gate = _silu(_conv_get(ext_ref, g0, FFN_COLS, cw_ref, 3, nb, tt) + cb_ref[:, g0:g0 + FFN_COLS])
        val = _conv_get(ext_ref, v0, FFN_COLS, cw_ref, 3, nb, tt) + cb_ref[:, v0:v0 + FFN_COLS]
        a = (gate * val).reshape(m, FFN_COLS).astype(BF16)
        acc = acc + jnp.dot(a, wout_ref[g0:g0 + FFN_COLS, :], preferred_element_type=F32)
    y_ref[...] = x + mod_ref[:, 5:6, :] * acc.reshape(nb, tt, D_MODEL)
    nbuf_ref[...] = ext_ref[:, 6:8, :]


def _ffn(x, mod, ng, win, cw, cb, wout, buf, nb, tt):
    b, t, d = x.shape
    n2 = 2 * D_FF
    kern = functools.partial(_ffn_kernel, nb=nb, tt=tt)
    return pl.pallas_call(
        kern,
        grid=(b // nb, t // tt),
        in_specs=[pl.BlockSpec((nb, tt, d), lambda i, j: (i, j, 0)),
                  pl.BlockSpec((nb, 8, d), lambda i, j: (i, 0, 0)),
                  pl.BlockSpec((1, d), lambda i, j: (0, 0)),
                  _wspec((d, n2)),
                  pl.BlockSpec((3, n2), lambda i, j: (0, 0)),
                  pl.BlockSpec((1, n2), lambda i, j: (0, 0)),
                  _wspec((D_FF, d)),
                  pl.BlockSpec((nb, 2, n2), lambda i, j: (i, 0, 0))],
        out_specs=[pl.BlockSpec((nb, tt, d), lambda i, j: (i, j, 0)),
                   pl.BlockSpec((nb, 2, n2), lambda i, j: (i, 0, 0))],
        out_shape=[jax.ShapeDtypeStruct((b, t, d), F32),
                   jax.ShapeDtypeStruct((b, 2, n2), F32)],
        scratch_shapes=[pltpu.VMEM((nb, 8 + tt, n2), F32)],
        compiler_params=_cparams("arbitrary", "arbitrary"),
        name="conv_ffn",
    )(x, mod, ng, win, cw, cb, wout, buf)


def _gdn_proj_kernel(x_ref, mod_ref, ng_ref, wqkv_ref, wz_ref, wab_ref, cw_ref, pv_ref, buf_ref,
                     q_ref, k_ref, v_ref, z_ref, gb_ref, nbuf_ref, ext_ref, *, nb, tt):
    m = nb * tt
    d = D_MODEL

    @pl.when(pl.program_id(1) == 0)
    def _():
        ext_ref[:, 5:8, :] = buf_ref[...]

    h = _norm_mod(x_ref[...], ng_ref[...], mod_ref[:, 1:2, :], mod_ref[:, 0:1, :])
    h = h.reshape(m, d).astype(BF16)
    z_ref[...] = jnp.dot(h, wz_ref[...], preferred_element_type=F32).reshape(nb, tt, d)
    ab = jnp.dot(h, wab_ref[...], preferred_element_type=F32)
    lane = _iota2(ab.shape, 1)
    gb = jnp.where(lane < GDN_HEADS,
                   pv_ref[0:1, :] * _softplus(ab + pv_ref[1:2, :]),
                   _sigmoid(ab))
    gb_ref[...] = gb.reshape(nb, tt, LANES)
    outs = (q_ref, k_ref, v_ref)
    for s in range(GDN_QKV // MXU_N):
        c0 = s * MXU_N
        _conv_put(jnp.dot(h, wqkv_ref[:, c0:c0 + MXU_N], preferred_element_type=F32), ext_ref, c0, nb, tt)
    for s in range(GDN_QKV // MXU_N):
        c0 = s * MXU_N
        y = _silu(_conv_get(ext_ref, c0, MXU_N, cw_ref, 4, nb, tt))
        which, o0 = divmod(c0, d)
        if which < 2:
            halves = []
            for hh in range(MXU_N // GDN_DK):
                yh = y[:, :, hh * GDN_DK:(hh + 1) * GDN_DK]
                r = lax.rsqrt(jnp.sum(yh * yh, axis=-1, keepdims=True) + NORM_EPS)
                if which == 0:
                    r = r * (GDN_DK ** -0.5)
                halves.append(yh * r)
            y = jnp.concatenate(halves, axis=-1)
        outs[which][:, :, o0:o0 + MXU_N] = y
    nbuf_ref[...] = ext_ref[:, 5:8, :]


def _gdn_proj(x, mod, ng, wqkv, wz, wab, cw, pv, buf, nb, tt):
    b, t, d = x.shape
    kern = functools.partial(_gdn_proj_kernel, nb=nb, tt=tt)
    row = lambda i, j: (i, j, 0)
    const2 = lambda i, j: (0, 0)
    act = jax.ShapeDtypeStruct((b, t, d), F32)
    return pl.pallas_call(
        kern,
        grid=(b // nb, t // tt),
        in_specs=[pl.BlockSpec((nb, tt, d), row),
                  pl.BlockSpec((nb, 8, d), lambda i, j: (i, 0, 0)),
                  pl.BlockSpec((1, d), const2),
                  _wspec((d, GDN_QKV)),
                  _wspec((d, d)),
                  pl.BlockSpec((d, LANES), const2),
                  pl.BlockSpec((4, GDN_QKV), const2),
                  pl.BlockSpec((8, LANES), const2),
                  pl.BlockSpec((nb, 3, GDN_QKV), lambda i, j: (i, 0, 0))],
        out_specs=[pl.BlockSpec((nb, tt, d), row)] * 4
                  + [pl.BlockSpec((nb, tt, LANES), row),
                     pl.BlockSpec((nb, 3, GDN_QKV), lambda i, j: (i, 0, 0))],
        out_shape=[act, act, act, act,
                   jax.ShapeDtypeStruct((b, t, LANES), F32),
                   jax.ShapeDtypeStruct((b, 3, GDN_QKV), F32)],
        scratch_shapes=[pltpu.VMEM((nb, 8 + tt, GDN_QKV), F32)],
        compiler_params=_cparams("arbitrary", "arbitrary"),
        name="gdn_proj",
    )(x, mod, ng, wqkv, wz, wab, cw, pv, buf)


def _tri_inverses(lmats, chunk):
    n = lmats[0].shape[0]
    row = _iota2((n, n), 0)
    col = _iota2((n, n), 1)
    level = 31 - lax.clz(jnp.bitwise_xor(row, col))
    eye = (row == col).astype(F32)
    xs = [eye - jnp.where(level == 0, lm, 0.0) for lm in lmats]
    k = 1
    while (1 << k) < chunk:
        ps = [_dot(x, jnp.where(level == k, lm, 0.0)) for x, lm in zip(xs, lmats)]
        xs = [x - _dot(p, x) for x, p in zip(xs, ps)]
        k += 1
    return xs


def _gdn_core_kernel(q_ref, k_ref, v_ref, z_ref, gb_ref, x_ref, mod_ref, og_ref, wout_ref, s0_ref,
                     y_ref, sout_ref, s_ref, o_ref, *, tt, chunk):
    @pl.when(pl.program_id(1) == 0)
    def _():
        s_ref[...] = s0_ref[0]

    nck = tt // chunk
    row = _iota2((tt, tt), 0)
    col = _iota2((tt, tt), 1)
    if nck > 1:
        same = (row // chunk) == (col // chunk)
        incl = (row >= col) & same
        strict = (row > col) & same
    else:
        same = None
        incl = row >= col
        strict = row > col
    gb = gb_ref[0]
    gcol = _sel_dot(incl.astype(BF16), gb)
    if nck > 1:
        glast = _sel_dot(same.astype(BF16), gb)
    else:
        glast = jnp.broadcast_to(jnp.sum(gb, axis=0, keepdims=True), gb.shape)
    grow = gcol.T
    eg = jnp.exp(gcol)
    ekd = jnp.exp(glast - gcol)
    egl = jnp.exp(glast)

    heads = range(GDN_HEADS)
    cols = [slice(hd * GDN_DK, (hd + 1) * GDN_DK) for hd in heads]
    kbs, lmats, qks = [], [], []
    for hd in heads:
        kh = k_ref[0, :, cols[hd]]
        diff = gcol[:, hd:hd + 1] - grow[hd:hd + 1, :]
        decay = jnp.exp(jnp.where(incl, diff, -jnp.inf))
        kb = kh * gb[:, GDN_HEADS + hd:GDN_HEADS + hd + 1]
        kbs.append(kb)
        lmats.append(jnp.where(strict, _dot_nt(kb, kh) * decay, 0.0))
        qks.append((_dot_nt(q_ref[0, :, cols[hd]], kh) * decay).astype(BF16))
    tinvs = [t.astype(BF16) for t in _tri_inverses(lmats, chunk)]
    us, ws = [], []
    for hd in heads:
        beta = gb[:, GDN_HEADS + hd:GDN_HEADS + hd + 1]
        us.append(_dot(tinvs[hd], v_ref[0, :, cols[hd]] * beta))
        ws.append(_dot(tinvs[hd], kbs[hd] * eg[:, hd:hd + 1]).astype(BF16))
    ss = [s_ref[hd] for hd in heads]
    vnews = [[] for _ in heads]
    ocross = [[] for _ in heads]
    for c in range(nck):
        rows = slice(c * chunk, (c + 1) * chunk)
        for hd in heads:
            sb = ss[hd].astype(BF16)
            vn = us[hd][rows] - _dot(ws[hd][rows], sb)
            ocross[hd].append(_dot(q_ref[0, rows, cols[hd]] * eg[rows, hd:hd + 1], sb))
            kd = k_ref[0, rows, cols[hd]] * ekd[rows, hd:hd + 1]
            ss[hd] = ss[hd] * egl[c * chunk:c * chunk + 1, hd:hd + 1] + _dot_tn(kd, vn)
            vnews[hd].append(vn.astype(BF16))
    for hd in heads:
        s_ref[hd] = ss[hd]
        vnew = jnp.concatenate(vnews[hd], axis=0) if nck > 1 else vnews[hd][0]
        oc = jnp.concatenate(ocross[hd], axis=0) if nck > 1 else ocross[hd][0]
        o = oc + _dot(qks[hd], vnew)
        o = o * lax.rsqrt(jnp.mean(o * o, axis=-1, keepdims=True) + NORM_EPS) * og_ref[...]
        o_ref[:, cols[hd]] = (o * _silu(z_ref[0, :, cols[hd]])).astype(BF16)

    out = jnp.dot(o_ref[...], wout_ref[...], preferred_element_type=F32)
    y_ref[0] = x_ref[0] + mod_ref[0, 2:3, :] * out
    sout_ref[0] = s_ref[...]


def _gdn_core(q, k, v, z, gb, x, mod, og, wout, s0, tt, chunk):
    b, t, d = x.shape
    kern = functools.partial(_gdn_core_kernel, tt=tt, chunk=chunk)
    row = lambda i, j: (i, j, 0)
    return pl.pallas_call(
        kern,
        grid=(b, t // tt),
        in_specs=[pl.BlockSpec((1, tt, d), row)] * 4
                 + [pl.BlockSpec((1, tt, LANES), row),
                    pl.BlockSpec((1, tt, d), row),
                    pl.BlockSpec((1, 8, d), lambda i, j: (i, 0, 0)),
                    pl.BlockSpec((1, GDN_DK), lambda i, j: (0, 0)),
                    _wspec((d, d)),
                    pl.BlockSpec((1, GDN_HEADS, GDN_DK, GDN_DK), lambda i, j: (i, 0, 0, 0))],
        out_specs=[pl.BlockSpec((1, tt, d), row),
                   pl.BlockSpec((1, GDN_HEADS, GDN_DK, GDN_DK), lambda i, j: (i, 0, 0, 0))],
        out_shape=[jax.ShapeDtypeStruct((b, t, d), F32),
                   jax.ShapeDtypeStruct((b, GDN_HEADS, GDN_DK, GDN_DK), F32)],
        scratch_shapes=[pltpu.VMEM((GDN_HEADS, GDN_DK, GDN_DK), F32),
                        pltpu.VMEM((tt, d), BF16)],
        compiler_params=_cparams("arbitrary", "arbitrary"),
        name="gdn_core",
    )(q, k, v, z, gb, x, mod, og, wout, s0)


def _sb_proj_kernel(x_ref, mod_ref, ng_ref, w_ref, qkg_ref,
                    kf_ref, vf_ref, qb_ref, kb_ref, vb_ref, u_ref, *, nb, tt):
    m = nb * tt
    d = D_MODEL
    h = _norm_mod(x_ref[...], ng_ref[...], mod_ref[:, 1:2, :], mod_ref[:, 0:1, :])
    h = h.reshape(m, d).astype(BF16)
    grp = ((_iota2((MXU_N, MXU_N), 0) // SB_DH) == (_iota2((MXU_N, MXU_N), 1) // SB_DH)).astype(BF16)
    for s in range(3 * d // MXU_N):
        c0 = s * MXU_N
        u_ref[:, c0:c0 + MXU_N] = jnp.dot(h, w_ref[:, c0:c0 + MXU_N], preferred_element_type=F32)
    for s in range(3 * d // MXU_N):
        c0 = s * MXU_N
        which, o0 = divmod(c0, d)
        u = u_ref[:, c0:c0 + MXU_N]
        if which < 2:
            ms = _dot_sel(u * u, grp) * (1.0 / SB_DH)
            u = u * lax.rsqrt(ms + NORM_EPS) * qkg_ref[which:which + 1, o0:o0 + MXU_N]
        u3 = u.reshape(nb, tt, MXU_N)
        if which == 0:
            qb_ref[:, :, o0:o0 + MXU_N] = u3.astype(BF16)
        elif which == 1:
            kf_ref[:, :, o0:o0 + MXU_N] = u3
            kb_ref[:, :, o0:o0 + MXU_N] = u3.astype(BF16)
        else:
            vf_ref[:, :, o0:o0 + MXU_N] = u3
            vb_ref[:, :, o0:o0 + MXU_N] = u3.astype(BF16)


def _sb_proj(x, mod, ng, w, qkg, nb, tt):
    b, t, d = x.shape
    kern = functools.partial(_sb_proj_kernel, nb=nb, tt=tt)
    row = lambda i, j: (i, j, 0)
    const2 = lambda i, j: (0, 0)
    f = jax.ShapeDtypeStruct((b, t, d), F32)
    h = jax.ShapeDtypeStruct((b, t, d), BF16)
    return pl.pallas_call(
        kern,
        grid=(b // nb, t // tt),
        in_specs=[pl.BlockSpec((nb, tt, d), row),
                  pl.BlockSpec((nb, 8, d), lambda i, j: (i, 0, 0)),
                  pl.BlockSpec((1, d), const2),
                  _wspec((d, 3 * d)),
                  pl.BlockSpec((8, d), const2)],
        out_specs=[pl.BlockSpec((nb, tt, d), row)] * 5,
        out_shape=[f, f, h, h, h],
        scratch_shapes=[pltpu.VMEM((nb * tt, 3 * d), F32)],
        compiler_params=_cparams("arbitrary", "arbitrary"),
        name="sb_proj",
    )(x, mod, ng, w, qkg)


def _suffix_selector(bw):
    r = jnp.bitwise_and(_iota2((2 * bw, bw + LANES), 0), bw - 1)
    c = _iota2((2 * bw, bw + LANES), 1)
    return jnp.where(c < bw, (r > c).astype(F32), 1.0).astype(BF16)


def _dot_nt_bf(a, b):
    return lax.dot_general(a, b, (((1,), (1,)), ((), ())), preferred_element_type=F32)


def _sb_tile(qh, kt, vt, carry, acc, mask, sel):
    n = kt.shape[0]
    bw = min(LANES, n)
    z = _dot_nt_bf(qh, kt)
    sp = jnp.maximum(z, 0.0) + jnp.log(1.0 + jnp.exp(-jnp.abs(z)))
    spm = sp if mask is None else jnp.where(mask, sp, 0.0)
    ws = []
    for mb in reversed(range(n // bw)):
        sl = slice(mb * bw, (mb + 1) * bw)
        hi = spm[:, sl].astype(BF16)
        lo = (spm[:, sl] - hi.astype(F32)).astype(BF16)
        r = jnp.dot(jnp.concatenate([hi, lo], axis=1), sel[bw], preferred_element_type=F32)
        w = jnp.exp(z[:, sl] - sp[:, sl] - carry[:, :bw] - r[:, :bw])
        if mask is not None:
            w = jnp.where(mask[:, sl], w, 0.0)
        ws.append(w.astype(BF16))
        carry = carry + r[:, bw:]
    wts = jnp.concatenate(ws[::-1], axis=-1) if len(ws) > 1 else ws[0]
    acc = acc + jnp.dot(wts, vt, preferred_element_type=F32)
    return carry, acc


def _sb_head_queries(q):
    q = q * (SB_DH ** -0.5)
    lane = _iota2(q.shape, 1)
    return [jnp.where((lane // SB_DH) == hd, q, jnp.zeros_like(q)) for hd in range(2)]


def _sb_fast_kernel(q_ref, kd_ref, vd_ref, kp_ref, vp_ref, o_ref, need_ref, st_ref, *, tq, more_before_prev):
    tp = kp_ref.shape[1]
    npair = D_MODEL // LANES
    sel = {bw: _suffix_selector(bw) for bw in {min(LANES, tq), min(LANES, tp)}}
    dmask = _iota2((tq, tq), 1) < _iota2((tq, tq), 0)
    zero = jnp.zeros((tq, LANES), F32)
    pairs = [slice(hp * LANES, (hp + 1) * LANES) for hp in range(npair)]

    for hp in range(npair):
        qhs = _sb_head_queries(q_ref[0, :, pairs[hp]])
        for hd in range(2):
            c, a = _sb_tile(qhs[hd], kd_ref[0, :, pairs[hp]], vd_ref[0, :, pairs[hp]], zero, zero, dmask, sel)
            st_ref[hp, 2 * hd] = c
            st_ref[hp, 2 * hd + 1] = a

    for hp in range(npair):
        qhs = _sb_head_queries(q_ref[0, :, pairs[hp]])
        for hd in range(2):
            c, a = _sb_tile(qhs[hd], kp_ref[0, :, pairs[hp]], vp_ref[0, :, pairs[hp]],
                            st_ref[hp, 2 * hd], st_ref[hp, 2 * hd + 1], None, sel)
            st_ref[hp, 2 * hd] = c
            st_ref[hp, 2 * hd + 1] = a

    lane = _iota2((tq, LANES), 1)
    cmin = None
    for hp in range(npair):
        o_ref[0, :, pairs[hp]] = jnp.where((lane // SB_DH) == 0, st_ref[hp, 1], st_ref[hp, 3]).astype(BF16)
        c = jnp.minimum(st_ref[hp, 0], st_ref[hp, 2])
        cmin = c if cmin is None else jnp.minimum(cmin, c)
    alive = jnp.where(jnp.min(cmin, axis=0, keepdims=True) < SB_DEAD, 1.0, 0.0)
    if not more_before_prev:
        alive = jnp.zeros_like(alive)
    need_ref[0, 0] = jnp.broadcast_to(alive, (8, LANES))


def _sb_fast(qb, kb, vb, prev_k, prev_v, tq, more_before_prev):
    b, t, d = qb.shape
    tp = prev_k.shape[1]
    kern = functools.partial(_sb_fast_kernel, tq=tq, more_before_prev=more_before_prev)
    cur = pl.BlockSpec((1, tq, d), lambda i, j: (i, j, 0))
    prev = pl.BlockSpec((1, tp, d), lambda i, j: (i, 0, 0))
    return pl.pallas_call(
        kern,
        grid=(b, t // tq),
        in_specs=[cur, cur, cur, prev, prev],
        out_specs=[cur, pl.BlockSpec((1, 1, 8, LANES), lambda i, j: (i, j, 0, 0))],
        out_shape=[jax.ShapeDtypeStruct((b, t, d), BF16),
                   jax.ShapeDtypeStruct((b, t // tq, 8, LANES), F32)],
        scratch_shapes=[pltpu.VMEM((d // LANES, 4, tq, LANES), F32)],
        compiler_params=_cparams("arbitrary", "arbitrary"),
        name="sb_fast",
    )(qb, kb, vb, prev_k, prev_v)


def _sb_attention(qb, kb, vb, past_k, past_v, tq):
    b, t, d = qb.shape
    if past_k is None:
        return _sb_attn(qb, kb, vb, None, None, tq)
    assert t == tq
    plen = past_k.shape[1]
    tp = min(256, plen)
    tail = lambda c: c[:, plen - tp:].reshape(b, tp, d).astype(BF16)
    o, need = _sb_fast(qb, kb, vb, tail(past_k), tail(past_v), tq, plen > tp)
    full = lambda: _sb_attn(qb, kb, vb, past_k.reshape(b, plen, d), past_v.reshape(b, plen, d), tq)
    return lax.cond(jnp.max(need) > 0.0, full, lambda: o)


def _sb_attn_kernel(*refs, tq, tk, past_len):
    if past_len:
        q_ref, k_ref, v_ref, pk_ref, pv_ref, o_ref = refs
    else:
        q_ref, k_ref, v_ref, o_ref = refs
        pk_ref = pv_ref = None
    qi = pl.program_id(2)
    lane = _iota2((tq, LANES), 1)
    qhs = _sb_head_queries(q_ref[0])
    sel = {bw: _suffix_selector(bw) for bw in {min(LANES, tq), min(LANES, tk)}}

    def both(kt, vt, state, mask):
        c0, a0 = _sb_tile(qhs[0], kt, vt, state[0], state[1], mask, sel)
        c1, a1 = _sb_tile(qhs[1], kt, vt, state[2], state[3], mask, sel)
        return c0, a0, c1, a1

    zero = jnp.zeros((tq, LANES), F32)
    r0 = pl.multiple_of(qi * tq, tq)
    dmask = _iota2((tq, tq), 1) < _iota2((tq, tq), 0)
    state = both(k_ref[0, pl.ds(r0, tq), :], v_ref[0, pl.ds(r0, tq), :], (zero, zero, zero, zero), dmask)

    def alive(st):
        return jnp.min(jnp.minimum(st[0], st[2])) < SB_DEAD

    def key_loop(ntiles, load, st):
        def cond(c):
            return jnp.logical_and(c[0] < ntiles, c[1])

        def body(c):
            kt, vt = load(pl.multiple_of((ntiles - 1 - c[0]) * tk, tk))
            nst = both(kt, vt, c[2:], None)
            return (c[0] + 1, alive(nst)) + nst

        return lax.while_loop(cond, body, (jnp.int32(0), alive(st)) + st)[2:]

    state = key_loop(qi, lambda k0: (k_ref[0, pl.ds(k0, tk), :], v_ref[0, pl.ds(k0, tk), :]), state)
    if past_len:
        state = key_loop(past_len // tk,
                         lambda k0: (pk_ref[0, pl.ds(k0, tk), :].astype(BF16),
                                     pv_ref[0, pl.ds(k0, tk), :].astype(BF16)), state)
    o_ref[0] = jnp.where((lane // SB_DH) == 0, state[1], state[3]).astype(BF16)


def _sb_attn(qb, kb, vb, past_k, past_v, tq):
    b, t, d = qb.shape
    past_len = 0 if past_k is None else past_k.shape[1]
    tk = tq if not past_len else min(256, past_len)
    kern = functools.partial(_sb_attn_kernel, tq=tq, tk=tk, past_len=past_len)
    qspec = pl.BlockSpec((1, tq, LANES), lambda i, h, j: (i, j, h))
    kvspec = pl.BlockSpec((1, t, LANES), lambda i, h, j: (i, 0, h))
    in_specs = [qspec, kvspec, kvspec]
    args = [qb, kb, vb]
    if past_len:
        pspec = pl.BlockSpec((1, past_len, LANES), lambda i, h, j: (i, 0, h))
        in_specs += [pspec, pspec]
        args += [past_k, past_v]
    return pl.pallas_call(
        kern,
        grid=(b, d // LANES, t // tq),
        in_specs=in_specs,
        out_specs=qspec,
        out_shape=jax.ShapeDtypeStruct((b, t, d), BF16),
        compiler_params=_cparams("arbitrary", "arbitrary", "arbitrary"),
        name="sb_attn",
    )(*args)


def _out_proj_kernel(o_ref, x_ref, mod_ref, w_ref, y_ref, *, nb, tt):
    out = jnp.dot(o_ref[...].reshape(nb * tt, -1), w_ref[...], preferred_element_type=F32)
    y_ref[...] = x_ref[...] + mod_ref[:, 2:3, :] * out.reshape(nb, tt, D_MODEL)


def _out_proj(o, x, mod, w, nb, tt):
    b, t, d = x.shape
    kin = o.shape[-1]
    kern = functools.partial(_out_proj_kernel, nb=nb, tt=tt)
    row = lambda i, j: (i, j, 0)
    return pl.pallas_call(
        kern,
        grid=(b // nb, t // tt),
        in_specs=[pl.BlockSpec((nb, tt, kin), row),
                  pl.BlockSpec((nb, tt, d), row),
                  pl.BlockSpec((nb, 8, d), lambda i, j: (i, 0, 0)),
                  _wspec((kin, d))],
        out_specs=pl.BlockSpec((nb, tt, d), row),
        out_shape=jax.ShapeDtypeStruct((b, t, d), F32),
        compiler_params=_cparams("arbitrary", "arbitrary"),
        name="out_proj",
    )(o, x, mod, w)


def _ret_log_gamma(hd):
    return math.log1p(-(2.0 ** (-5.0 - hd)))


def _ret_proj_kernel(x_ref, mod_ref, ng_ref, w_ref, cos_ref, sin_ref,
                     q_ref, qd_ref, k_ref, kd_ref, v_ref, gt_ref, *, nb, tt, chunk):
    m = nb * tt
    d = D_MODEL
    hk = RET_HEADS * RET_DK
    hv = RET_HEADS * RET_DV
    h = _norm_mod(x_ref[...], ng_ref[...], mod_ref[:, 1:2, :], mod_ref[:, 0:1, :])
    h = h.reshape(m, d).astype(BF16)
    cos = cos_ref[...][None]
    sin = sin_ref[...][None]
    tpos = pl.program_id(1) * tt + _iota2((1, tt, 1), 1)
    assert chunk & (chunk - 1) == 0
    idx = jnp.bitwise_and(tpos, chunk - 1).astype(F32)
    half = RET_DK // 2
    for s in range(2 * hk // MXU_N):
        c0 = s * MXU_N
        which, o0 = divmod(c0, hk)
        hd = o0 // RET_DK
        lg = _ret_log_gamma(hd)
        u = jnp.dot(h, w_ref[:, c0:c0 + MXU_N], preferred_element_type=F32).reshape(nb, tt, MXU_N)
        x1 = u[:, :, :half]
        x2 = u[:, :, half:]
        r = jnp.concatenate([x1 * cos - x2 * sin, x1 * sin + x2 * cos], axis=-1)
        if which == 0:
            q_ref[:, :, o0:o0 + MXU_N] = r.astype(BF16)
            qd_ref[:, :, o0:o0 + MXU_N] = (r * jnp.exp(lg * (idx + 1.0))).astype(BF16)
        else:
            r = r * (RET_DK ** -0.5)
            k_ref[:, :, o0:o0 + MXU_N] = r.astype(BF16)
            kd_ref[:, :, o0:o0 + MXU_N] = (r * jnp.exp(lg * (chunk - 1.0 - idx))).astype(BF16)
    for s in range(2 * hv // MXU_N):
        c0 = 2 * hk + s * MXU_N
        u = jnp.dot(h, w_ref[:, c0:c0 + MXU_N], preferred_element_type=F32).reshape(nb, tt, MXU_N)
        o0 = s * MXU_N
        if o0 < hv:
            v_ref[:, :, o0:o0 + MXU_N] = u.astype(BF16)
        else:
            gt_ref[:, :, o0 - hv:o0 - hv + MXU_N] = u


def _ret_proj(x, mod, ng, w, cos, sin, nb, tt, chunk):
    b, t, d = x.shape
    hk = RET_HEADS * RET_DK
    hv = RET_HEADS * RET_DV
    kern = functools.partial(_ret_proj_kernel, nb=nb, tt=tt, chunk=chunk)
    row = lambda i, j: (i, j, 0)
    const2 = lambda i, j: (0, 0)
    qs = jax.ShapeDtypeStruct((b, t, hk), BF16)
    return pl.pallas_call(
        kern,
        grid=(b // nb, t // tt),
        in_specs=[pl.BlockSpec((nb, tt, d), row),
                  pl.BlockSpec((nb, 8, d), lambda i, j: (i, 0, 0)),
                  pl.BlockSpec((1, d), const2),
                  _wspec((d, 2 * hk + 2 * hv)),
                  pl.BlockSpec((tt, RET_DK // 2), lambda i, j: (j, 0)),
                  pl.BlockSpec((tt, RET_DK // 2), lambda i, j: (j, 0))],
        out_specs=[pl.BlockSpec((nb, tt, hk), row)] * 4
                  + [pl.BlockSpec((nb, tt, hv), row)] * 2,
        out_shape=[qs, qs, qs, qs,
                   jax.ShapeDtypeStruct((b, t, hv), BF16),
                   jax.ShapeDtypeStruct((b, t, hv), F32)],
        compiler_params=_cparams("arbitrary", "arbitrary"),
        name="ret_proj",
    )(x, mod, ng, w, cos, sin)


def _ret_core_kernel(q_ref, qd_ref, k_ref, kd_ref, v_ref, gt_ref, x_ref, mod_ref, og_ref, wout_ref, r0_ref,
                     y_ref, rout_ref, r_ref, o_ref, *, chunk):
    @pl.when(pl.program_id(1) == 0)
    def _():
        r_ref[...] = r0_ref[0]

    rel = (_iota2((chunk, chunk), 0) - _iota2((chunk, chunk), 1)).astype(F32)
    for hd in range(RET_HEADS):
        lg = _ret_log_gamma(hd)
        k0 = hd * RET_DK
        v0 = hd * RET_DV
        dmask = jnp.where(rel >= 0, jnp.exp(lg * jnp.maximum(rel, 0.0)), 0.0)
        vh = v_ref[0, :, v0:v0 + RET_DV]
        s = _dot_nt_bf(q_ref[0, :, k0:k0 + RET_DK], k_ref[0, :, k0:k0 + RET_DK]) * dmask
        r = r_ref[hd]
        o = (jnp.dot(s.astype(BF16), vh, preferred_element_type=F32)
             + jnp.dot(qd_ref[0, :, k0:k0 + RET_DK], r.astype(BF16), preferred_element_type=F32))
        r_ref[hd] = r * math.exp(lg * chunk) + lax.dot_general(
            kd_ref[0, :, k0:k0 + RET_DK], vh, (((0,), (0,)), ((), ())), preferred_element_type=F32)
        mu = jnp.mean(o, axis=-1, keepdims=True)
        dlt = o - mu
        var = jnp.mean(dlt * dlt, axis=-1, keepdims=True)
        o = dlt * lax.rsqrt(var + NORM_EPS) * og_ref[:, v0:v0 + RET_DV]
        o_ref[:, v0:v0 + RET_DV] = (o * _silu(gt_ref[0, :, v0:v0 + RET_DV])).astype(BF16)
    out = jnp.dot(o_ref[...], wout_ref[...], preferred_element_type=F32)
    y_ref[0] = x_ref[0] + mod_ref[0, 2:3, :] * out
    rout_ref[0] = r_ref[...]


def _ret_core(q, qd, k, kd, v, gt, x, mod, og, wout, r0, chunk):
    b, t, d = x.shape
    hk = RET_HEADS * RET_DK
    hv = RET_HEADS * RET_DV
    kern = functools.partial(_ret_core_kernel, chunk=chunk)
    row = lambda i, j: (i, j, 0)
    st = pl.BlockSpec((1, RET_HEADS, RET_DK, RET_DV), lambda i, j: (i, 0, 0, 0))
    return pl.pallas_call(
        kern,
        grid=(b, t // chunk),
        in_specs=[pl.BlockSpec((1, chunk, hk), row)] * 4
                 + [pl.BlockSpec((1, chunk, hv), row)] * 2
                 + [pl.BlockSpec((1, chunk, d), row),
                    pl.BlockSpec((1, 8, d), lambda i, j: (i, 0, 0)),
                    pl.BlockSpec((1, hv), lambda i, j: (0, 0)),
                    _wspec((hv, d)),
                    st],
        out_specs=[pl.BlockSpec((1, chunk, d), row), st],
        out_shape=[jax.ShapeDtypeStruct((b, t, d), F32),
                   jax.ShapeDtypeStruct((b, RET_HEADS, RET_DK, RET_DV), F32)],
        scratch_shapes=[pltpu.VMEM((RET_HEADS, RET_DK, RET_DV), F32),
                        pltpu.VMEM((chunk, hv), BF16)],
        compiler_params=_cparams("arbitrary", "arbitrary"),
        name="ret_core",
    )(q, qd, k, kd, v, gt, x, mod, og, wout, r0)


def _run_group(x, mod_all, states, ffn_bufs, pos0, wts):
    b, t, d = x.shape
    if t >= ROW_TILE:
        nb, tt = 1, ROW_TILE
    else:
        nb, tt = b, t
    new_states, new_ffn = [], []
    for i in range(DEPTH):
        kind, j = i % 3, i // 3
        mod = mod_all[i]
        if kind == 0:
            s0, cbuf = states[i]
            gw = wts["gdn"][j]
            q, k, v, z, gb, ncbuf = _gdn_proj(x, mod, wts["norm_mix_g"][i], gw["wqkv"], gw["wz"], gw["wab"],
                                              gw["conv_w"], gw["pv"], cbuf, nb, tt)
            x, s_new = _gdn_core(q, k, v, z, gb, x, mod, gw["norm_g"], gw["wout"], s0,
                                 min(GDN_TILE, t), min(CHUNK, t))
            new_states.append((s_new, ncbuf))
        elif kind == 1:
            pk, pv = states[i]
            sw = wts["sb"][j]
            kf, vf, qb, kb, vb = _sb_proj(x, mod, wts["norm_mix_g"][i], sw["win"], sw["qkg"], nb, tt)
            o = _sb_attention(qb, kb, vb, pk, pv, min(SB_TILE, t))
            x = _out_proj(o, x, mod, sw["wout"], nb, tt)
            new_states.append((kf.reshape(b, t, SB_HEADS, SB_DH), vf.reshape(b, t, SB_HEADS, SB_DH)))
        else:
            (r0,) = states[i]
            rw = wts["ret"][j]
            chunk = min(RET_CHUNK, t)
            half = RET_DK // 2
            inv_freq = RET_ROPE_BASE ** (-jnp.arange(half, dtype=F32) / half)
            ang = (pos0 + jnp.arange(t)).astype(F32)[:, None] * inv_freq[None, :]
            q, qd, k, kd, v, gt = _ret_proj(x, mod, wts["norm_mix_g"][i], rw["win"], jnp.cos(ang), jnp.sin(ang),
                                            nb, tt, chunk)
            x, r_new = _ret_core(q, qd, k, kd, v, gt, x, mod, rw["norm_g"], rw["wout"], r0, chunk)
            new_states.append((r_new,))
        fw = wts["ffn"][i]
        x, fbuf = _ffn(x, mod, wts["norm_ffn_g"][i], fw["win"], fw["conv_w"], fw["conv_b"], fw["wout"],
                       ffn_bufs[i], nb, tt)
        new_ffn.append(fbuf)
    return x, new_states, jnp.stack(new_ffn)


def kernel(x_prompt, x_sample, state_l0_gdn_S, state_l0_gdn_conv, cache_l1_sb_k, cache_l1_sb_v, state_l2_ret, state_l3_gdn_S, state_l3_gdn_conv, state_ffn_conv, c_prompt, c_sample, ada_w, ada_b, norm_mix_g, norm_ffn_g, gdn_w_in, gdn_conv_w, gdn_A_log, gdn_dt_bias, gdn_norm_g, gdn_w_out, sb_w_in, sb_q_norm_g, sb_k_norm_g, sb_w_out, ret_w_in, ret_norm_g, ret_w_out, ffn_w_in, ffn_conv_w, ffn_conv_b, ffn_w_out):
    d = D_MODEL
    bp, tp, _ = x_prompt.shape
    bs, ts, _ = x_sample.shape

    hv = GDN_HEADS * GDN_DK
    wts = {"norm_mix_g": [norm_mix_g[i].reshape(1, d) for i in range(DEPTH)],
           "norm_ffn_g": [norm_ffn_g[i].reshape(1, d) for i in range(DEPTH)],
           "gdn": [], "sb": [], "ret": [], "ffn": []}
    for j in range(gdn_w_in.shape[0]):
        w = gdn_w_in[j]
        wab = jnp.pad(w[:, GDN_QKV + hv:], ((0, 0), (0, LANES - 2 * GDN_HEADS)))
        pv = jnp.zeros((8, LANES), F32)
        pv = pv.at[0, :GDN_HEADS].set(-jnp.exp(gdn_A_log[j].astype(F32)))
        pv = pv.at[1, :GDN_HEADS].set(gdn_dt_bias[j].astype(F32))
        wts["gdn"].append({"wqkv": w[:, :GDN_QKV].astype(BF16),
                           "wz": w[:, GDN_QKV:GDN_QKV + hv].astype(BF16),
                           "wab": wab.astype(BF16),
                           "conv_w": gdn_conv_w[j], "pv": pv,
                           "norm_g": gdn_norm_g[j].reshape(1, GDN_DK),
                           "wout": gdn_w_out[j].astype(BF16)})
    for j in range(sb_w_in.shape[0]):
        qkg = jnp.zeros((8, d), F32)
        qkg = qkg.at[0].set(jnp.tile(sb_q_norm_g[j], SB_HEADS)).at[1].set(jnp.tile(sb_k_norm_g[j], SB_HEADS))
        wts["sb"].append({"win": sb_w_in[j].astype(BF16), "qkg": qkg, "wout": sb_w_out[j].astype(BF16)})
    for j in range(ret_w_in.shape[0]):
        wts["ret"].append({"win": ret_w_in[j].astype(BF16),
                           "norm_g": ret_norm_g[j].reshape(1, RET_HEADS * RET_DV),
                           "wout": ret_w_out[j].astype(BF16)})
    for i in range(DEPTH):
        wts["ffn"].append({"win": ffn_w_in[i].astype(BF16), "conv_w": ffn_conv_w[i],
                           "conv_b": ffn_conv_b[i].reshape(1, 2 * D_FF), "wout": ffn_w_out[i].astype(BF16)})

    nrow = bp + bs
    rows = -(-nrow // 8) * 8
    c_all = jnp.pad(jnp.concatenate([c_prompt, c_sample], axis=0), ((0, rows - nrow), (0, 0)))
    mod = _ada_mod(c_all, ada_w, ada_b)
    mod = jnp.pad(mod.reshape(DEPTH, rows, 6, d), ((0, 0), (0, 0), (0, 2), (0, 0)))
    mod_p, mod_s = mod[:, :bp], mod[:, bp:nrow]

    dt = x_prompt.dtype
    zero_states = [(jnp.zeros((bp, GDN_HEADS, GDN_DK, GDN_DK), dt), jnp.zeros((bp, 3, GDN_QKV), dt)),
                   (None, None),
                   (jnp.zeros((bp, RET_HEADS, RET_DK, RET_DV), dt),),
                   (jnp.zeros((bp, GDN_HEADS, GDN_DK, GDN_DK), dt), jnp.zeros((bp, 3, GDN_QKV), dt))]
    zero_ffn = jnp.zeros((DEPTH, bp, 2, 2 * D_FF), dt)
    y_prompt, p_states, p_ffn_conv = _run_group(x_prompt, mod_p, zero_states, zero_ffn, 0, wts)

    past_len = cache_l1_sb_k.shape[1]
    sample_states = [(state_l0_gdn_S, state_l0_gdn_conv),
                     (cache_l1_sb_k, cache_l1_sb_v),
                     (state_l2_ret,), (state_l3_gdn_S, state_l3_gdn_conv)]
    y_sample, s_states, s_ffn_conv = _run_group(x_sample, mod_s, sample_states, state_ffn_conv, past_len, wts)

    (p_l0_S, p_l0_conv), (p_l1_k, p_l1_v), (p_l2_R,), (p_l3_S, p_l3_conv) = p_states
    (s_l0_S, s_l0_conv), (s_l1_k, s_l1_v), (s_l2_R,), (s_l3_S, s_l3_conv) = s_states
    return (y_prompt, y_sample,
            p_l0_S, p_l0_conv, p_l1_k, p_l1_v, p_l2_R, p_l3_S, p_l3_conv, p_ffn_conv,
            s_l0_S, s_l0_conv, s_l1_k, s_l1_v, s_l2_R, s_l3_S, s_l3_conv, s_ffn_conv)
```

```python
import functools
import math

import jax
import jax.numpy as jnp
from jax import lax
from jax.experimental import pallas as pl
from jax.experimental.pallas import tpu as pltpu

F32 = jnp.float32
BF16 = jnp.bfloat16

D_MODEL = 1024
DEPTH = 4
CHUNK = 64
GDN_HEADS = 8
GDN_DK = 128
GDN_QKV = 3 * GDN_HEADS * GDN_DK
SB_HEADS = 16
SB_DH = 64
RET_HEADS = 4
RET_DK = 256
RET_DV = 512
RET_ROPE_BASE = 10000.0
D_FF = 2816
NORM_EPS = 1e-6

LANES = 128
MXU_N = 256
VMEM_LIMIT = 56 * 1024 * 1024

ROW_TILE = 512
GDN_TILE = 256
RET_CHUNK = 256
SB_TILE = 256
SB_DEAD = 105.0
SB_PAIRS = 2


def _cparams(*sem):
    return pltpu.CompilerParams(dimension_semantics=sem, vmem_limit_bytes=VMEM_LIMIT)


def _wspec(shape):
    return pl.BlockSpec(shape, lambda i, j: (0,) * len(shape), pipeline_mode=pl.Buffered(1))


def _dot(a, b):
    return jnp.dot(a.astype(BF16), b.astype(BF16), preferred_element_type=F32)


def _dot_nt(a, b):
    return lax.dot_general(a.astype(BF16), b.astype(BF16), (((1,), (1,)), ((), ())),
                           preferred_element_type=F32)


def _dot_tn(a, b):
    return lax.dot_general(a.astype(BF16), b.astype(BF16), (((0,), (0,)), ((), ())),
                           preferred_element_type=F32)


def _split(x):
    hi = x.astype(BF16)
    lo = (x - hi.astype(F32)).astype(BF16)
    return hi, lo


def _dot_sel(a, sel):
    hi, lo = _split(a)
    return (jnp.dot(hi, sel, preferred_element_type=F32)
            + jnp.dot(lo, sel, preferred_element_type=F32))


def _sel_dot(sel, b):
    hi, lo = _split(b)
    return (jnp.dot(sel, hi, preferred_element_type=F32)
            + jnp.dot(sel, lo, preferred_element_type=F32))


def _dot3(a, b):
    ah, al = _split(a)
    bh, bl = _split(b)
    return (jnp.dot(ah, bh, preferred_element_type=F32)
            + jnp.dot(ah, bl, preferred_element_type=F32)
            + jnp.dot(al, bh, preferred_element_type=F32))


def _sigmoid(x):
    return 1.0 / (1.0 + jnp.exp(-x))


def _silu(x):
    return x * _sigmoid(x)


def _softplus(x):
    return jnp.maximum(x, 0.0) + jnp.log(1.0 + jnp.exp(-jnp.abs(x)))


def _norm_mod(x, gain, scale, shift):
    ms = jnp.mean(x * x, axis=-1, keepdims=True)
    y = x * lax.rsqrt(ms + NORM_EPS) * gain
    return y * (1.0 + scale) + shift


def _iota2(shape, dim):
    return lax.broadcasted_iota(jnp.int32, shape, dim)


def _ada_kernel(c_ref, w_ref, b_ref, o_ref):
    o_ref[0] = _dot(_silu(c_ref[...]), w_ref[0]) + b_ref[0]


def _ada_mod(c_all, ada_w, ada_b):
    rows = c_all.shape[0]
    n = ada_w.shape[2]
    tn = 1536
    return pl.pallas_call(
        _ada_kernel,
        grid=(DEPTH, n // tn),
        in_specs=[pl.BlockSpec((rows, D_MODEL), lambda i, j: (0, 0)),
                  pl.BlockSpec((1, D_MODEL, tn), lambda i, j: (i, 0, j)),
                  pl.BlockSpec((1, 1, tn), lambda i, j: (i, 0, j))],
        out_specs=pl.BlockSpec((1, rows, tn), lambda i, j: (i, 0, j)),
        out_shape=jax.ShapeDtypeStruct((DEPTH, rows, n), F32),
        compiler_params=_cparams("arbitrary", "arbitrary"),
        name="ada_mod",
    )(c_all, ada_w, ada_b.reshape(DEPTH, 1, n))


def _conv_put(u, ext_ref, col0, nb, tt):
    n = u.shape[-1]
    ext_ref[:, 8:8 + tt, col0:col0 + n] = u.reshape(nb, tt, n)


def _conv_get(ext_ref, col0, n, w_ref, width, nb, tt):
    hw = width - 1
    cols = slice(col0, col0 + n)
    y = ext_ref[:, 8:8 + tt, cols] * w_ref[hw:hw + 1, cols]
    for j in range(hw):
        y = y + ext_ref[:, 8 - hw + j:8 - hw + j + tt, cols] * w_ref[j:j + 1, cols]
    ext_ref[:, 8 - hw:8, cols] = ext_ref[:, 8 + tt - hw:8 + tt, cols]
    return y


FFN_COLS = 256


def _ffn_kernel(x_ref, mod_ref, ng_ref, win_ref, cw_ref, cb_ref, wout_ref, buf_ref,
                y_ref, nbuf_ref, ext_ref, *, nb, tt):
    m = nb * tt

    @pl.when(pl.program_id(1) == 0)
    def _():
        ext_ref[:, 6:8, :] = buf_ref[...]

    x = x_ref[...]
    h = _norm_mod(x, ng_ref[...], mod_ref[:, 4:5, :], mod_ref[:, 3:4, :])
    h = h.reshape(m, D_MODEL).astype(BF16)
    acc = jnp.zeros((m, D_MODEL), F32)
    nchunk = D_FF // FFN_COLS

    for c in range(nchunk):
        for col0 in (c * FFN_COLS, D_FF + c * FFN_COLS):
            _conv_put(jnp.dot(h, win_ref[:, col0:col0 + FFN_COLS], preferred_element_type=F32),
                      ext_ref, col0, nb, tt)
    for c in range(nchunk):
        g0 = c * FFN_COLS
        v0 = D_FF + c * FFN_COLS
        gate = _silu(_conv_get(ext_ref, g0, FFN_COLS, cw_ref, 3, nb, tt) + cb_ref[:, g0:g0 + FFN_COLS])
        val = _conv_get(ext_ref, v0, FFN_COLS, cw_ref, 3, nb, tt) + cb_ref[:, v0:v0 + FFN_COLS]
        a = (gate * val).reshape(m, FFN_COLS).astype(BF16)
        acc = acc + jnp.dot(a, wout_ref[g0:g0 + FFN_COLS, :], preferred_element_type=F32)
    y_ref[...] = x + mod_ref[:, 5:6, :] * acc.reshape(nb, tt, D_MODEL)
    nbuf_ref[...] = ext_ref[:, 6:8, :]


def _ffn(x, mod, ng, win, cw, cb, wout, buf, nb, tt):
    b, t, d = x.shape
    n2 = 2 * D_FF
    kern = functools.partial(_ffn_kernel, nb=nb, tt=tt)
    return pl.pallas_call(
        kern,
        grid=(b // nb, t // tt),
        in_specs=[pl.BlockSpec((nb, tt, d), lambda i, j: (i, j, 0)),
                  pl.BlockSpec((nb, 8, d), lambda i, j: (i, 0, 0)),
                  pl.BlockSpec((1, d), lambda i, j: (0, 0)),
                  _wspec((d, n2)),
                  pl.BlockSpec((3, n2), lambda i, j: (0, 0)),
                  pl.BlockSpec((1, n2), lambda i, j: (0, 0)),
                  _wspec((D_FF, d)),
                  pl.BlockSpec((nb, 2, n2), lambda i, j: (i, 0, 0))],
        out_specs=[pl.BlockSpec((nb, tt, d), lambda i, j: (i, j, 0)),
                   pl.BlockSpec((nb, 2, n2), lambda i, j: (i, 0, 0))],
        out_shape=[jax.ShapeDtypeStruct((b, t, d), F32),
                   jax.ShapeDtypeStruct((b, 2, n2), F32)],
        scratch_shapes=[pltpu.VMEM((nb, 8 + tt, n2), F32)],
        compiler_params=_cparams("arbitrary", "arbitrary"),
        name="conv_ffn",
    )(x, mod, ng, win, cw, cb, wout, buf)


def _gdn_proj_kernel(x_ref, mod_ref, ng_ref, wqkv_ref, wz_ref, wab_ref, cw_ref, pv_ref, buf_ref,
                     q_ref, k_ref, v_ref, z_ref, gb_ref, nbuf_ref, ext_ref, *, nb, tt):
    m = nb * tt
    d = D_MODEL

    @pl.when(pl.program_id(1) == 0)
    def _():
        ext_ref[:, 5:8, :] = buf_ref[...]

    h = _norm_mod(x_ref[...], ng_ref[...], mod_ref[:, 1:2, :], mod_ref[:, 0:1, :])
    h = h.reshape(m, d).astype(BF16)
    z_ref[...] = jnp.dot(h, wz_ref[...], preferred_element_type=F32).reshape(nb, tt, d)
    ab = jnp.dot(h, wab_ref[...], preferred_element_type=F32)
    lane = _iota2(ab.shape, 1)
    gb = jnp.where(lane < GDN_HEADS,
                   pv_ref[0:1, :] * _softplus(ab + pv_ref[1:2, :]),
                   _sigmoid(ab))
    gb_ref[...] = gb.reshape(nb, tt, LANES)
    outs = (q_ref, k_ref, v_ref)
    for s in range(GDN_QKV // MXU_N):
        c0 = s * MXU_N
        _conv_put(jnp.dot(h, wqkv_ref[:, c0:c0 + MXU_N], preferred_element_type=F32), ext_ref, c0, nb, tt)
    for s in range(GDN_QKV // MXU_N):
        c0 = s * MXU_N
        y = _silu(_conv_get(ext_ref, c0, MXU_N, cw_ref, 4, nb, tt))
        which, o0 = divmod(c0, d)
        if which < 2:
            halves = []
            for hh in range(MXU_N // GDN_DK):
                yh = y[:, :, hh * GDN_DK:(hh + 1) * GDN_DK]
                r = lax.rsqrt(jnp.sum(yh * yh, axis=-1, keepdims=True) + NORM_EPS)
                if which == 0:
                    r = r * (GDN_DK ** -0.5)
                halves.append(yh * r)
            y = jnp.concatenate(halves, axis=-1)
        outs[which][:, :, o0:o0 + MXU_N] = y
    nbuf_ref[...] = ext_ref[:, 5:8, :]


def _gdn_proj(x, mod, ng, wqkv, wz, wab, cw, pv, buf, nb, tt):
    b, t, d = x.shape
    kern = functools.partial(_gdn_proj_kernel, nb=nb, tt=tt)
    row = lambda i, j: (i, j, 0)
    const2 = lambda i, j: (0, 0)
    act = jax.ShapeDtypeStruct((b, t, d), F32)
    return pl.pallas_call(
        kern,
        grid=(b // nb, t // tt),
        in_specs=[pl.BlockSpec((nb, tt, d), row),
                  pl.BlockSpec((nb, 8, d), lambda i, j: (i, 0, 0)),
                  pl.BlockSpec((1, d), const2),
                  _wspec((d, GDN_QKV)),
                  _wspec((d, d)),
                  pl.BlockSpec((d, LANES), const2),
                  pl.BlockSpec((4, GDN_QKV), const2),
                  pl.BlockSpec((8, LANES), const2),
                  pl.BlockSpec((nb, 3, GDN_QKV), lambda i, j: (i, 0, 0))],
        out_specs=[pl.BlockSpec((nb, tt, d), row)] * 4
                  + [pl.BlockSpec((nb, tt, LANES), row),
                     pl.BlockSpec((nb, 3, GDN_QKV), lambda i, j: (i, 0, 0))],
        out_shape=[act, act, act, act,
                   jax.ShapeDtypeStruct((b, t, LANES), F32),
                   jax.ShapeDtypeStruct((b, 3, GDN_QKV), F32)],
        scratch_shapes=[pltpu.VMEM((nb, 8 + tt, GDN_QKV), F32)],
        compiler_params=_cparams("arbitrary", "arbitrary"),
        name="gdn_proj",
    )(x, mod, ng, wqkv, wz, wab, cw, pv, buf)


def _tri_inverses(lmats, chunk):
    n = lmats[0].shape[0]
    row = _iota2((n, n), 0)
    col = _iota2((n, n), 1)
    level = 31 - lax.clz(jnp.bitwise_xor(row, col))
    eye = (row == col).astype(F32)
    xs = [eye - jnp.where(level == 0, lm, 0.0) for lm in lmats]
    k = 1
    while (1 << k) < chunk:
        xbs = [x.astype(BF16) for x in xs]
        ps = [_dot(xb, jnp.where(level == k, lm, 0.0)).astype(BF16) for xb, lm in zip(xbs, lmats)]
        xs = [x - _dot(p, xb) for x, p, xb in zip(xs, ps, xbs)]
        k += 1
    return xs


def _gdn_core_kernel(q_ref, k_ref, v_ref, z_ref, gb_ref, x_ref, mod_ref, og_ref, wout_ref, s0_ref,
                     y_ref, sout_ref, s_ref, o_ref, *, tt, chunk):
    @pl.when(pl.program_id(1) == 0)
    def _():
        s_ref[...] = s0_ref[0]

    nck = tt // chunk
    row = _iota2((tt, tt), 0)
    col = _iota2((tt, tt), 1)
    if nck > 1:
        same = (row // chunk) == (col // chunk)
        incl = (row >= col) & same
        strict = (row > col) & same
    else:
        same = None
        incl = row >= col
        strict = row > col
    gb = gb_ref[0]
    gcol = _sel_dot(incl.astype(BF16), gb)
    if nck > 1:
        glast = _sel_dot(same.astype(BF16), gb)
    else:
        glast = jnp.broadcast_to(jnp.sum(gb, axis=0, keepdims=True), gb.shape)
    grow = gcol.T
    eg = jnp.exp(gcol)
    ekd = jnp.exp(glast - gcol)
    egl = jnp.exp(glast)

    heads = range(GDN_HEADS)
    cols = [slice(hd * GDN_DK, (hd + 1) * GDN_DK) for hd in heads]
    kbs, lmats, qks = [], [], []
    for hd in heads:
        kh = k_ref[0, :, cols[hd]]
        diff = gcol[:, hd:hd + 1] - grow[hd:hd + 1, :]
        decay = jnp.exp(jnp.where(incl, diff, -jnp.inf))
        kb = kh * gb[:, GDN_HEADS + hd:GDN_HEADS + hd + 1]
        kbs.append(kb)
        lmats.append(jnp.where(strict, _dot_nt(kb, kh) * decay, 0.0))
        qks.append((_dot_nt(q_ref[0, :, cols[hd]], kh) * decay).astype(BF16))
    tinvs = [t.astype(BF16) for t in _tri_inverses(lmats, chunk)]
    us, ws = [], []
    for hd in heads:
        beta = gb[:, GDN_HEADS + hd:GDN_HEADS + hd + 1]
        uw = _dot(tinvs[hd], jnp.concatenate([v_ref[0, :, cols[hd]] * beta, kbs[hd] * eg[:, hd:hd + 1]], axis=1))
        us.append(uw[:, :GDN_DK])
        ws.append(uw[:, GDN_DK:].astype(BF16))
    ss = [s_ref[hd] for hd in heads]
    vnews = [[] for _ in heads]
    ocross = [[] for _ in heads]
    for c in range(nck):
        rows = slice(c * chunk, (c + 1) * chunk)
        for hd in heads:
            qd = (q_ref[0, rows, cols[hd]] * eg[rows, hd:hd + 1]).astype(BF16)
            ws_qs = _dot(jnp.concatenate([ws[hd][rows], qd], axis=0), ss[hd])
            vn = us[hd][rows] - ws_qs[:chunk]
            ocross[hd].append(ws_qs[chunk:])
            kd = k_ref[0, rows, cols[hd]] * ekd[rows, hd:hd + 1]
            ss[hd] = ss[hd] * egl[c * chunk:c * chunk + 1, hd:hd + 1] + _dot_tn(kd, vn)
            vnews[hd].append(vn.astype(BF16))
    for hd in heads:
        s_ref[hd] = ss[hd]
        vnew = jnp.concatenate(vnews[hd], axis=0) if nck > 1 else vnews[hd][0]
        oc = jnp.concatenate(ocross[hd], axis=0) if nck > 1 else ocross[hd][0]
        o = oc + _dot(qks[hd], vnew)
        o = o * lax.rsqrt(jnp.mean(o * o, axis=-1, keepdims=True) + NORM_EPS) * og_ref[...]
        o_ref[:, cols[hd]] = (o * _silu(z_ref[0, :, cols[hd]])).astype(BF16)

    out = jnp.dot(o_ref[...], wout_ref[...], preferred_element_type=F32)
    y_ref[0] = x_ref[0] + mod_ref[0, 2:3, :] * out
    sout_ref[0] = s_ref[...]


def _gdn_core(q, k, v, z, gb, x, mod, og, wout, s0, tt, chunk):
    b, t, d = x.shape
    kern = functools.partial(_gdn_core_kernel, tt=tt, chunk=chunk)
    row = lambda i, j: (i, j, 0)
    return pl.pallas_call(
        kern,
        grid=(b, t // tt),
        in_specs=[pl.BlockSpec((1, tt, d), row)] * 4
                 + [pl.BlockSpec((1, tt, LANES), row),
                    pl.BlockSpec((1, tt, d), row),
                    pl.BlockSpec((1, 8, d), lambda i, j: (i, 0, 0)),
                    pl.BlockSpec((1, GDN_DK), lambda i, j: (0, 0)),
                    _wspec((d, d)),
                    pl.BlockSpec((1, GDN_HEADS, GDN_DK, GDN_DK), lambda i, j: (i, 0, 0, 0))],
        out_specs=[pl.BlockSpec((1, tt, d), row),
                   pl.BlockSpec((1, GDN_HEADS, GDN_DK, GDN_DK), lambda i, j: (i, 0, 0, 0))],
        out_shape=[jax.ShapeDtypeStruct((b, t, d), F32),
                   jax.ShapeDtypeStruct((b, GDN_HEADS, GDN_DK, GDN_DK), F32)],
        scratch_shapes=[pltpu.VMEM((GDN_HEADS, GDN_DK, GDN_DK), F32),
                        pltpu.VMEM((tt, d), BF16)],
        compiler_params=_cparams("arbitrary", "arbitrary"),
        name="gdn_core",
    )(q, k, v, z, gb, x, mod, og, wout, s0)


def _sb_proj_kernel(x_ref, mod_ref, ng_ref, w_ref, qkg_ref,
                    kf_ref, vf_ref, qb_ref, kb_ref, vb_ref, u_ref, *, nb, tt):
    m = nb * tt
    d = D_MODEL
    h = _norm_mod(x_ref[...], ng_ref[...], mod_ref[:, 1:2, :], mod_ref[:, 0:1, :])
    h = h.reshape(m, d).astype(BF16)
    grp = ((_iota2((MXU_N, MXU_N), 0) // SB_DH) == (_iota2((MXU_N, MXU_N), 1) // SB_DH)).astype(BF16)
    for s in range(3 * d // MXU_N):
        c0 = s * MXU_N
        u_ref[:, c0:c0 + MXU_N] = jnp.dot(h, w_ref[:, c0:c0 + MXU_N], preferred_element_type=F32)
    for s in range(3 * d // MXU_N):
        c0 = s * MXU_N
        which, o0 = divmod(c0, d)
        u = u_ref[:, c0:c0 + MXU_N]
        if which < 2:
            ms = _dot_sel(u * u, grp) * (1.0 / SB_DH)
            u = u * lax.rsqrt(ms + NORM_EPS) * qkg_ref[which:which + 1, o0:o0 + MXU_N]
        u3 = u.reshape(nb, tt, MXU_N)
        if which == 0:
            qb_ref[:, :, o0:o0 + MXU_N] = u3.astype(BF16)
        elif which == 1:
            kf_ref[:, :, o0:o0 + MXU_N] = u3
            kb_ref[:, :, o0:o0 + MXU_N] = u3.astype(BF16)
        else:
            vf_ref[:, :, o0:o0 + MXU_N] = u3
            vb_ref[:, :, o0:o0 + MXU_N] = u3.astype(BF16)


def _sb_proj(x, mod, ng, w, qkg, nb, tt):
    b, t, d = x.shape
    kern = functools.partial(_sb_proj_kernel, nb=nb, tt=tt)
    row = lambda i, j: (i, j, 0)
    const2 = lambda i, j: (0, 0)
    f = jax.ShapeDtypeStruct((b, t, d), F32)
    h = jax.ShapeDtypeStruct((b, t, d), BF16)
    return pl.pallas_call(
        kern,
        grid=(b // nb, t // tt),
        in_specs=[pl.BlockSpec((nb, tt, d), row),
                  pl.BlockSpec((nb, 8, d), lambda i, j: (i, 0, 0)),
                  pl.BlockSpec((1, d), const2),
                  _wspec((d, 3 * d)),
                  pl.BlockSpec((8, d), const2)],
        out_specs=[pl.BlockSpec((nb, tt, d), row)] * 5,
        out_shape=[f, f, h, h, h],
        scratch_shapes=[pltpu.VMEM((nb * tt, 3 * d), F32)],
        compiler_params=_cparams("arbitrary", "arbitrary"),
        name="sb_proj",
    )(x, mod, ng, w, qkg)


def _suffix_selector(bw):
    r = jnp.bitwise_and(_iota2((2 * bw, bw + LANES), 0), bw - 1)
    c = _iota2((2 * bw, bw + LANES), 1)
    return jnp.where(c < bw, (r > c).astype(F32), 1.0).astype(BF16)


def _dot_nt_bf(a, b):
    return lax.dot_general(a, b, (((1,), (1,)), ((), ())), preferred_element_type=F32)


def _sb_tile(qh, kt, vt, carry, acc, mask, sel):
    n = kt.shape[0]
    bw = min(LANES, n)
    z = _dot_nt_bf(qh, kt)
    sp = jnp.maximum(z, 0.0) + jnp.log(1.0 + jnp.exp(-jnp.abs(z)))
    spm = sp if mask is None else jnp.where(mask, sp, 0.0)
    ws = []
    for mb in reversed(range(n // bw)):
        sl = slice(mb * bw, (mb + 1) * bw)
        hi = spm[:, sl].astype(BF16)
        lo = (spm[:, sl] - hi.astype(F32)).astype(BF16)
        r = jnp.dot(jnp.concatenate([hi, lo], axis=1), sel[bw], preferred_element_type=F32)
        w = jnp.exp(z[:, sl] - sp[:, sl] - carry[:, :bw] - r[:, :bw])
        if mask is not None:
            w = jnp.where(mask[:, sl], w, 0.0)
        ws.append(w.astype(BF16))
        carry = carry + r[:, bw:]
    wts = jnp.concatenate(ws[::-1], axis=-1) if len(ws) > 1 else ws[0]
    acc = acc + jnp.dot(wts, vt, preferred_element_type=F32)
    return carry, acc


def _sb_head_queries(q):
    q = q * (SB_DH ** -0.5)
    lane = _iota2(q.shape, 1)
    return [jnp.where((lane // SB_DH) == hd, q, jnp.zeros_like(q)) for hd in range(2)]


def _sb_fast_kernel(q_ref, kd_ref, vd_ref, kp_ref, vp_ref, o_ref, need_ref, st_ref, *, tq, more_before_prev):
    tp = kp_ref.shape[1]
    npair = D_MODEL // LANES
    sel = {bw: _suffix_selector(bw) for bw in {min(LANES, tq), min(LANES, tp)}}
    dmask = _iota2((tq, tq), 1) < _iota2((tq, tq), 0)
    zero = jnp.zeros((tq, LANES), F32)
    pairs = [slice(hp * LANES, (hp + 1) * LANES) for hp in range(npair)]

    for hp in range(npair):
        qhs = _sb_head_queries(q_ref[0, :, pairs[hp]])
        for hd in range(2):
            c, a = _sb_tile(qhs[hd], kd_ref[0, :, pairs[hp]], vd_ref[0, :, pairs[hp]], zero, zero, dmask, sel)
            st_ref[hp, 2 * hd] = c
            st_ref[hp, 2 * hd + 1] = a

    for hp in range(npair):
        qhs = _sb_head_queries(q_ref[0, :, pairs[hp]])
        for hd in range(2):
            c, a = _sb_tile(qhs[hd], kp_ref[0, :, pairs[hp]], vp_ref[0, :, pairs[hp]],
                            st_ref[hp, 2 * hd], st_ref[hp, 2 * hd + 1], None, sel)
            st_ref[hp, 2 * hd] = c
            st_ref[hp, 2 * hd + 1] = a

    lane = _iota2((tq, LANES), 1)
    cmin = None
    for hp in range(npair):
        o_ref[0, :, pairs[hp]] = jnp.where((lane // SB_DH) == 0, st_ref[hp, 1], st_ref[hp, 3]).astype(BF16)
        c = jnp.minimum(st_ref[hp, 0], st_ref[hp, 2])
        cmin = c if cmin is None else jnp.minimum(cmin, c)
    alive = jnp.where(jnp.min(cmin, axis=0, keepdims=True) < SB_DEAD, 1.0, 0.0)
    if not more_before_prev:
        alive = jnp.zeros_like(alive)
    need_ref[0, 0] = jnp.broadcast_to(alive, (8, LANES))


def _sb_fast(qb, kb, vb, prev_k, prev_v, tq, more_before_prev):
    b, t, d = qb.shape
    tp = prev_k.shape[1]
    kern = functools.partial(_sb_fast_kernel, tq=tq, more_before_prev=more_before_prev)
    cur = pl.BlockSpec((1, tq, d), lambda i, j: (i, j, 0))
    prev = pl.BlockSpec((1, tp, d), lambda i, j: (i, 0, 0))
    return pl.pallas_call(
        kern,
        grid=(b, t // tq),
        in_specs=[cur, cur, cur, prev, prev],
        out_specs=[cur, pl.BlockSpec((1, 1, 8, LANES), lambda i, j: (i, j, 0, 0))],
        out_shape=[jax.ShapeDtypeStruct((b, t, d), BF16),
                   jax.ShapeDtypeStruct((b, t // tq, 8, LANES), F32)],
        scratch_shapes=[pltpu.VMEM((d // LANES, 4, tq, LANES), F32)],
        compiler_params=_cparams("arbitrary", "arbitrary"),
        name="sb_fast",
    )(qb, kb, vb, prev_k, prev_v)


def _sb_attention(qb, kb, vb, past_k, past_v, tq):
    b, t, d = qb.shape
    if past_k is None:
        return _sb_attn(qb, kb, vb, None, None, tq)
    assert t == tq
    plen = past_k.shape[1]
    tp = min(256, plen)
    tail = lambda c: c[:, plen - tp:].reshape(b, tp, d).astype(BF16)
    o, need = _sb_fast(qb, kb, vb, tail(past_k), tail(past_v), tq, plen > tp)
    full = lambda: _sb_attn(qb, kb, vb, past_k.reshape(b, plen, d), past_v.reshape(b, plen, d), tq)
    return lax.cond(jnp.max(need) > 0.0, full, lambda: o)


def _sb_attn_kernel(*refs, tq, tk, past_len):
    if past_len:
        q_ref, k_ref, v_ref, pk_ref, pv_ref, o_ref = refs
    else:
        q_ref, k_ref, v_ref, o_ref = refs
        pk_ref = pv_ref = None
    qi = pl.program_id(2)
    lane = _iota2((tq, LANES), 1)
    pairs = [slice(p * LANES, (p + 1) * LANES) for p in range(SB_PAIRS)]
    qhs = [qh for p in range(SB_PAIRS) for qh in _sb_head_queries(q_ref[0, :, pairs[p]])]
    ts = tq
    sel = {bw: _suffix_selector(bw) for bw in {min(LANES, ts), min(LANES, tk)}}

    def both(kt, vt, state, mask):
        new = ()
        for hd in range(2 * SB_PAIRS):
            new += _sb_tile(qhs[hd], kt[:, pairs[hd // 2]], vt[:, pairs[hd // 2]],
                            state[2 * hd], state[2 * hd + 1], mask, sel)
        return new

    zero = jnp.zeros((tq, LANES), F32)
    r0 = pl.multiple_of(qi * tq, tq)
    dmask = _iota2((tq, tq), 1) < _iota2((tq, tq), 0)
    state = both(k_ref[0, pl.ds(r0, tq), :], v_ref[0, pl.ds(r0, tq), :], (zero,) * (4 * SB_PAIRS), dmask)

    def alive(st):
        cmin = st[0]
        for hd in range(1, 2 * SB_PAIRS):
            cmin = jnp.minimum(cmin, st[2 * hd])
        return jnp.min(cmin) < SB_DEAD

    def key_loop(ntiles, size, load, st):
        def cond(c):
            return jnp.logical_and(c[0] < ntiles, c[1])

        def body(c):
            kt, vt = load(pl.multiple_of((ntiles - 1 - c[0]) * size, size))
            nst = both(kt, vt, c[2:], None)
            return (c[0] + 1, alive(nst)) + nst

        return lax.while_loop(cond, body, (jnp.int32(0), alive(st)) + st)[2:]

    state = key_loop(qi * (tq // ts), ts,
                     lambda k0: (k_ref[0, pl.ds(k0, ts), :], v_ref[0, pl.ds(k0, ts), :]), state)
    if past_len:
        state = key_loop(past_len // tk, tk,
                         lambda k0: (pk_ref[0, pl.ds(k0, tk), :].astype(BF16),
                                     pv_ref[0, pl.ds(k0, tk), :].astype(BF16)), state)
    for p in range(SB_PAIRS):
        o_ref[0, :, pairs[p]] = jnp.where((lane // SB_DH) == 0, state[4 * p + 1], state[4 * p + 3]).astype(BF16)


def _sb_attn(qb, kb, vb, past_k, past_v, tq):
    b, t, d = qb.shape
    past_len = 0 if past_k is None else past_k.shape[1]
    tk = tq if not past_len else min(256, past_len)
    kern = functools.partial(_sb_attn_kernel, tq=tq, tk=tk, past_len=past_len)
    width = SB_PAIRS * LANES
    qspec = pl.BlockSpec((1, tq, width), lambda i, h, j: (i, j, h))
    kvspec = pl.BlockSpec((1, t, width), lambda i, h, j: (i, 0, h))
    in_specs = [qspec, kvspec, kvspec]
    args = [qb, kb, vb]
    if past_len:
        pspec = pl.BlockSpec((1, past_len, width), lambda i, h, j: (i, 0, h))
        in_specs += [pspec, pspec]
        args += [past_k, past_v]
    return pl.pallas_call(
        kern,
        grid=(b, d // width, t // tq),
        in_specs=in_specs,
        out_specs=qspec,
        out_shape=jax.ShapeDtypeStruct((b, t, d), BF16),
        compiler_params=_cparams("arbitrary", "arbitrary", "arbitrary"),
        name="sb_attn",
    )(*args)


def _out_proj_kernel(o_ref, x_ref, mod_ref, w_ref, y_ref, *, nb, tt):
    out = jnp.dot(o_ref[...].reshape(nb * tt, -1), w_ref[...], preferred_element_type=F32)
    y_ref[...] = x_ref[...] + mod_ref[:, 2:3, :] * out.reshape(nb, tt, D_MODEL)


def _out_proj(o, x, mod, w, nb, tt):
    b, t, d = x.shape
    kin = o.shape[-1]
    kern = functools.partial(_out_proj_kernel, nb=nb, tt=tt)
    row = lambda i, j: (i, j, 0)
    return pl.pallas_call(
        kern,
        grid=(b // nb, t // tt),
        in_specs=[pl.BlockSpec((nb, tt, kin), row),
                  pl.BlockSpec((nb, tt, d), row),
                  pl.BlockSpec((nb, 8, d), lambda i, j: (i, 0, 0)),
                  _wspec((kin, d))],
        out_specs=pl.BlockSpec((nb, tt, d), row),
        out_shape=jax.ShapeDtypeStruct((b, t, d), F32),
        compiler_params=_cparams("arbitrary", "arbitrary"),
        name="out_proj",
    )(o, x, mod, w)


def _ret_log_gamma(hd):
    return math.log1p(-(2.0 ** (-5.0 - hd)))


def _ret_proj_kernel(x_ref, mod_ref, ng_ref, w_ref, cos_ref, sin_ref,
                     q_ref, qd_ref, k_ref, kd_ref, v_ref, gt_ref, *, nb, tt, chunk):
    m = nb * tt
    d = D_MODEL
    hk = RET_HEADS * RET_DK
    hv = RET_HEADS * RET_DV
    h = _norm_mod(x_ref[...], ng_ref[...], mod_ref[:, 1:2, :], mod_ref[:, 0:1, :])
    h = h.reshape(m, d).astype(BF16)
    cos = cos_ref[...][None]
    sin = sin_ref[...][None]
    tpos = pl.program_id(1) * tt + _iota2((1, tt, 1), 1)
    assert chunk & (chunk - 1) == 0
    idx = jnp.bitwise_and(tpos, chunk - 1).astype(F32)
    half = RET_DK // 2
    for s in range(2 * hk // MXU_N):
        c0 = s * MXU_N
        which, o0 = divmod(c0, hk)
        hd = o0 // RET_DK
        lg = _ret_log_gamma(hd)
        u = jnp.dot(h, w_ref[:, c0:c0 + MXU_N], preferred_element_type=F32).reshape(nb, tt, MXU_N)
        x1 = u[:, :, :half]
        x2 = u[:, :, half:]
        r = jnp.concatenate([x1 * cos - x2 * sin, x1 * sin + x2 * cos], axis=-1)
        if which == 0:
            q_ref[:, :, o0:o0 + MXU_N] = r.astype(BF16)
            qd_ref[:, :, o0:o0 + MXU_N] = (r * jnp.exp(lg * (idx + 1.0))).astype(BF16)
        else:
            r = r * (RET_DK ** -0.5)
            k_ref[:, :, o0:o0 + MXU_N] = r.astype(BF16)
            kd_ref[:, :, o0:o0 + MXU_N] = (r * jnp.exp(lg * (chunk - 1.0 - idx))).astype(BF16)
    for s in range(2 * hv // MXU_N):
        c0 = 2 * hk + s * MXU_N
        u = jnp.dot(h, w_ref[:, c0:c0 + MXU_N], preferred_element_type=F32).reshape(nb, tt, MXU_N)
        o0 = s * MXU_N
        if o0 < hv:
            v_ref[:, :, o0:o0 + MXU_N] = u.astype(BF16)
        else:
            gt_ref[:, :, o0 - hv:o0 - hv + MXU_N] = u


def _ret_proj(x, mod, ng, w, cos, sin, nb, tt, chunk):
    b, t, d = x.shape
    hk = RET_HEADS * RET_DK
    hv = RET_HEADS * RET_DV
    kern = functools.partial(_ret_proj_kernel, nb=nb, tt=tt, chunk=chunk)
    row = lambda i, j: (i, j, 0)
    const2 = lambda i, j: (0, 0)
    qs = jax.ShapeDtypeStruct((b, t, hk), BF16)
    return pl.pallas_call(
        kern,
        grid=(b // nb, t // tt),
        in_specs=[pl.BlockSpec((nb, tt, d), row),
                  pl.BlockSpec((nb, 8, d), lambda i, j: (i, 0, 0)),
                  pl.BlockSpec((1, d), const2),
                  _wspec((d, 2 * hk + 2 * hv)),
                  pl.BlockSpec((tt, RET_DK // 2), lambda i, j: (j, 0)),
                  pl.BlockSpec((tt, RET_DK // 2), lambda i, j: (j, 0))],
        out_specs=[pl.BlockSpec((nb, tt, hk), row)] * 4
                  + [pl.BlockSpec((nb, tt, hv), row)] * 2,
        out_shape=[qs, qs, qs, qs,
                   jax.ShapeDtypeStruct((b, t, hv), BF16),
                   jax.ShapeDtypeStruct((b, t, hv), F32)],
        compiler_params=_cparams("arbitrary", "arbitrary"),
        name="ret_proj",
    )(x, mod, ng, w, cos, sin)


def _ret_core_kernel(q_ref, qd_ref, k_ref, kd_ref, v_ref, gt_ref, x_ref, mod_ref, og_ref, wout_ref, r0_ref,
                     y_ref, rout_ref, r_ref, o_ref, *, chunk):
    @pl.when(pl.program_id(1) == 0)
    def _():
        r_ref[...] = r0_ref[0]

    rel = (_iota2((chunk, chunk), 0) - _iota2((chunk, chunk), 1)).astype(F32)
    for hd in range(RET_HEADS):
        lg = _ret_log_gamma(hd)
        k0 = hd * RET_DK
        v0 = hd * RET_DV
        dmask = jnp.where(rel >= 0, jnp.exp(lg * jnp.maximum(rel, 0.0)), 0.0)
        vh = v_ref[0, :, v0:v0 + RET_DV]
        s = _dot_nt_bf(q_ref[0, :, k0:k0 + RET_DK], k_ref[0, :, k0:k0 + RET_DK]) * dmask
        r = r_ref[hd]
        o = (jnp.dot(s.astype(BF16), vh, preferred_element_type=F32)
             + jnp.dot(qd_ref[0, :, k0:k0 + RET_DK], r.astype(BF16), preferred_element_type=F32))
        r_ref[hd] = r * math.exp(lg * chunk) + lax.dot_general(
            kd_ref[0, :, k0:k0 + RET_DK], vh, (((0,), (0,)), ((), ())), preferred_element_type=F32)
        mu = jnp.mean(o, axis=-1, keepdims=True)
        dlt = o - mu
        var = jnp.mean(dlt * dlt, axis=-1, keepdims=True)
        o = dlt * lax.rsqrt(var + NORM_EPS) * og_ref[:, v0:v0 + RET_DV]
        o_ref[:, v0:v0 + RET_DV] = (o * _silu(gt_ref[0, :, v0:v0 + RET_DV])).astype(BF16)
    out = jnp.dot(o_ref[...], wout_ref[...], preferred_element_type=F32)
    y_ref[0] = x_ref[0] + mod_ref[0, 2:3, :] * out
    rout_ref[0] = r_ref[...]


def _ret_core(q, qd, k, kd, v, gt, x, mod, og, wout, r0, chunk):
    b, t, d = x.shape
    hk = RET_HEADS * RET_DK
    hv = RET_HEADS * RET_DV
    kern = functools.partial(_ret_core_kernel, chunk=chunk)
    row = lambda i, j: (i, j, 0)
    st = pl.BlockSpec((1, RET_HEADS, RET_DK, RET_DV), lambda i, j: (i, 0, 0, 0))
    return pl.pallas_call(
        kern,
        grid=(b, t // chunk),
        in_specs=[pl.BlockSpec((1, chunk, hk), row)] * 4
                 + [pl.BlockSpec((1, chunk, hv), row)] * 2
                 + [pl.BlockSpec((1, chunk, d), row),
                    pl.BlockSpec((1, 8, d), lambda i, j: (i, 0, 0)),
                    pl.BlockSpec((1, hv), lambda i, j: (0, 0)),
                    _wspec((hv, d)),
                    st],
        out_specs=[pl.BlockSpec((1, chunk, d), row), st],
        out_shape=[jax.ShapeDtypeStruct((b, t, d), F32),
                   jax.ShapeDtypeStruct((b, RET_HEADS, RET_DK, RET_DV), F32)],
        scratch_shapes=[pltpu.VMEM((RET_HEADS, RET_DK, RET_DV), F32),
                        pltpu.VMEM((chunk, hv), BF16)],
        compiler_params=_cparams("arbitrary", "arbitrary"),
        name="ret_core",
    )(q, qd, k, kd, v, gt, x, mod, og, wout, r0)


def _run_group(x, mod_all, states, ffn_bufs, pos0, wts):
    b, t, d = x.shape
    if t >= ROW_TILE:
        nb, tt = 1, ROW_TILE
    else:
        nb, tt = b, t
    new_states, new_ffn = [], []
    for i in range(DEPTH):
        kind, j = i % 3, i // 3
        mod = mod_all[i]
        if kind == 0:
            s0, cbuf = states[i]
            gw = wts["gdn"][j]
            q, k, v, z, gb, ncbuf = _gdn_proj(x, mod, wts["norm_mix_g"][i], gw["wqkv"], gw["wz"], gw["wab"],
                                              gw["conv_w"], gw["pv"], cbuf, nb, tt)
            x, s_new = _gdn_core(q, k, v, z, gb, x, mod, gw["norm_g"], gw["wout"], s0,
                                 min(GDN_TILE, t), min(CHUNK, t))
            new_states.append((s_new, ncbuf))
        elif kind == 1:
            pk, pv = states[i]
            sw = wts["sb"][j]
            kf, vf, qb, kb, vb = _sb_proj(x, mod, wts["norm_mix_g"][i], sw["win"], sw["qkg"], nb, tt)
            o = _sb_attention(qb, kb, vb, pk, pv, min(SB_TILE, t))
            x = _out_proj(o, x, mod, sw["wout"], nb, tt)
            new_states.append((kf.reshape(b, t, SB_HEADS, SB_DH), vf.reshape(b, t, SB_HEADS, SB_DH)))
        else:
            (r0,) = states[i]
            rw = wts["ret"][j]
            chunk = min(RET_CHUNK, t)
            half = RET_DK // 2
            inv_freq = RET_ROPE_BASE ** (-jnp.arange(half, dtype=F32) / half)
            ang = (pos0 + jnp.arange(t)).astype(F32)[:, None] * inv_freq[None, :]
            q, qd, k, kd, v, gt = _ret_proj(x, mod, wts["norm_mix_g"][i], rw["win"], jnp.cos(ang), jnp.sin(ang),
                                            nb, tt, chunk)
            x, r_new = _ret_core(q, qd, k, kd, v, gt, x, mod, rw["norm_g"], rw["wout"], r0, chunk)
            new_states.append((r_new,))
        fw = wts["ffn"][i]
        x, fbuf = _ffn(x, mod, wts["norm_ffn_g"][i], fw["win"], fw["conv_w"], fw["conv_b"], fw["wout"],
                       ffn_bufs[i], nb, tt)
        new_ffn.append(fbuf)
    return x, new_states, jnp.stack(new_ffn)


def kernel(x_prompt, x_sample, state_l0_gdn_S, state_l0_gdn_conv, cache_l1_sb_k, cache_l1_sb_v, state_l2_ret, state_l3_gdn_S, state_l3_gdn_conv, state_ffn_conv, c_prompt, c_sample, ada_w, ada_b, norm_mix_g, norm_ffn_g, gdn_w_in, gdn_conv_w, gdn_A_log, gdn_dt_bias, gdn_norm_g, gdn_w_out, sb_w_in, sb_q_norm_g, sb_k_norm_g, sb_w_out, ret_w_in, ret_norm_g, ret_w_out, ffn_w_in, ffn_conv_w, ffn_conv_b, ffn_w_out):
    d = D_MODEL
    bp, tp, _ = x_prompt.shape
    bs, ts, _ = x_sample.shape

    hv = GDN_HEADS * GDN_DK
    wts = {"norm_mix_g": [norm_mix_g[i].reshape(1, d) for i in range(DEPTH)],
           "norm_ffn_g": [norm_ffn_g[i].reshape(1, d) for i in range(DEPTH)],
           "gdn": [], "sb": [], "ret": [], "ffn": []}
    for j in range(gdn_w_in.shape[0]):
        w = gdn_w_in[j]
        wab = jnp.pad(w[:, GDN_QKV + hv:], ((0, 0), (0, LANES - 2 * GDN_HEADS)))
        pv = jnp.zeros((8, LANES), F32)
        pv = pv.at[0, :GDN_HEADS].set(-jnp.exp(gdn_A_log[j].astype(F32)))
        pv = pv.at[1, :GDN_HEADS].set(gdn_dt_bias[j].astype(F32))
        wts["gdn"].append({"wqkv": w[:, :GDN_QKV].astype(BF16),
                           "wz": w[:, GDN_QKV:GDN_QKV + hv].astype(BF16),
                           "wab": wab.astype(BF16),
                           "conv_w": gdn_conv_w[j], "pv": pv,
                           "norm_g": gdn_norm_g[j].reshape(1, GDN_DK),
                           "wout": gdn_w_out[j].astype(BF16)})
    for j in range(sb_w_in.shape[0]):
        qkg = jnp.zeros((8, d), F32)
        qkg = qkg.at[0].set(jnp.tile(sb_q_norm_g[j], SB_HEADS)).at[1].set(jnp.tile(sb_k_norm_g[j], SB_HEADS))
        wts["sb"].append({"win": sb_w_in[j].astype(BF16), "qkg": qkg, "wout": sb_w_out[j].astype(BF16)})
    for j in range(ret_w_in.shape[0]):
        wts["ret"].append({"win": ret_w_in[j].astype(BF16),
                           "norm_g": ret_norm_g[j].reshape(1, RET_HEADS * RET_DV),
                           "wout": ret_w_out[j].astype(BF16)})
    for i in range(DEPTH):
        wts["ffn"].append({"win": ffn_w_in[i].astype(BF16), "conv_w": ffn_conv_w[i],
                           "conv_b": ffn_conv_b[i].reshape(1, 2 * D_FF), "wout": ffn_w_out[i].astype(BF16)})

    nrow = bp + bs
    rows = -(-nrow // 8) * 8
    c_all = jnp.pad(jnp.concatenate([c_prompt, c_sample], axis=0), ((0, rows - nrow), (0, 0)))
    mod = _ada_mod(c_all, ada_w, ada_b)
    mod = jnp.pad(mod.reshape(DEPTH, rows, 6, d), ((0, 0), (0, 0), (0, 2), (0, 0)))
    mod_p, mod_s = mod[:, :bp], mod[:, bp:nrow]

    dt = x_prompt.dtype
    zero_states = [(jnp.zeros((bp, GDN_HEADS, GDN_DK, GDN_DK), dt), jnp.zeros((bp, 3, GDN_QKV), dt)),
                   (None, None),
                   (jnp.zeros((bp, RET_HEADS, RET_DK, RET_DV), dt),),
                   (jnp.zeros((bp, GDN_HEADS, GDN_DK, GDN_DK), dt), jnp.zeros((bp, 3, GDN_QKV), dt))]
    zero_ffn = jnp.zeros((DEPTH, bp, 2, 2 * D_FF), dt)
    y_prompt, p_states, p_ffn_conv = _run_group(x_prompt, mod_p, zero_states, zero_ffn, 0, wts)

    past_len = cache_l1_sb_k.shape[1]
    sample_states = [(state_l0_gdn_S, state_l0_gdn_conv),
                     (cache_l1_sb_k, cache_l1_sb_v),
                     (state_l2_ret,), (state_l3_gdn_S, state_l3_gdn_conv)]
    y_sample, s_states, s_ffn_conv = _run_group(x_sample, mod_s, sample_states, state_ffn_conv, past_len, wts)

    (p_l0_S, p_l0_conv), (p_l1_k, p_l1_v), (p_l2_R,), (p_l3_S, p_l3_conv) = p_states
    (s_l0_S, s_l0_conv), (s_l1_k, s_l1_v), (s_l2_R,), (s_l3_S, s_l3_conv) = s_states
    return (y_prompt, y_sample,
            p_l0_S, p_l0_conv, p_l1_k, p_l1_v, p_l2_R, p_l3_S, p_l3_conv, p_ffn_conv,
            s_l0_S, s_l0_conv, s_l1_k, s_l1_v, s_l2_R, s_l3_S, s_l3_conv, s_ffn_conv)
```

```python
import functools
import math

import jax
import jax.numpy as jnp
from jax import lax
from jax.experimental import pallas as pl
from jax.experimental.pallas import tpu as pltpu

F32 = jnp.float32
BF16 = jnp.bfloat16

D_MODEL = 1024
DEPTH = 4
CHUNK = 64
GDN_HEADS = 8
GDN_DK = 128
GDN_QKV = 3 * GDN_HEADS * GDN_DK
GDN_CONV_TAPS = 4
CONV_STRIDE = 4
SB_HEADS = 16
SB_DH = 64
RET_HEADS = 4
RET_DK = 256
RET_DV = 512
RET_ROPE_BASE = 10000.0
D_FF = 2816
NORM_EPS = 1e-6

LANES = 128
MXU_N = 256
VMEM_LIMIT = 56 * 1024 * 1024

ROW_TILE = 512
GDN_TILE = 256
RET_CHUNK = 256
SB_TILE = 256
SB_DEAD = 105.0
SB_PAIRS = 4


def _cparams(*sem):
    return pltpu.CompilerParams(dimension_semantics=sem, vmem_limit_bytes=VMEM_LIMIT)


def _wspec(shape):
    return pl.BlockSpec(shape, lambda i, j: (0,) * len(shape), pipeline_mode=pl.Buffered(1))


def _dot(a, b):
    return jnp.dot(a.astype(BF16), b.astype(BF16), preferred_element_type=F32)


def _dot_nt(a, b):
    return lax.dot_general(a.astype(BF16), b.astype(BF16), (((1,), (1,)), ((), ())),
                           preferred_element_type=F32)


def _dot_tn(a, b):
    return lax.dot_general(a.astype(BF16), b.astype(BF16), (((0,), (0,)), ((), ())),
                           preferred_element_type=F32)


def _split(x):
    hi = x.astype(BF16)
    lo = (x - hi.astype(F32)).astype(BF16)
    return hi, lo


def _dot_sel(a, sel):
    hi, lo = _split(a)
    return (jnp.dot(hi, sel, preferred_element_type=F32)
            + jnp.dot(lo, sel, preferred_element_type=F32))


def _sel_dot(sel, b):
    hi, lo = _split(b)
    return (jnp.dot(sel, hi, preferred_element_type=F32)
            + jnp.dot(sel, lo, preferred_element_type=F32))


def _dot3(a, b):
    ah, al = _split(a)
    bh, bl = _split(b)
    return (jnp.dot(ah, bh, preferred_element_type=F32)
            + jnp.dot(ah, bl, preferred_element_type=F32)
            + jnp.dot(al, bh, preferred_element_type=F32))


def _sigmoid(x):
    return 1.0 / (1.0 + jnp.exp(-x))


def _silu(x):
    return x * _sigmoid(x)


def _softplus(x):
    return jnp.maximum(x, 0.0) + jnp.log(1.0 + jnp.exp(-jnp.abs(x)))


def _norm_mod(x, gain, scale, shift):
    ms = jnp.mean(x * x, axis=-1, keepdims=True)
    y = x * lax.rsqrt(ms + NORM_EPS) * gain
    return y * (1.0 + scale) + shift


def _iota2(shape, dim):
    return lax.broadcasted_iota(jnp.int32, shape, dim)


def _ada_kernel(c_ref, w_ref, b_ref, o_ref):
    o_ref[0] = _dot(_silu(c_ref[...]), w_ref[0]) + b_ref[0]


def _ada_mod(c_all, ada_w, ada_b):
    rows = c_all.shape[0]
    n = ada_w.shape[2]
    tn = 1536
    return pl.pallas_call(
        _ada_kernel,
        grid=(DEPTH, n // tn),
        in_specs=[pl.BlockSpec((rows, D_MODEL), lambda i, j: (0, 0)),
                  pl.BlockSpec((1, D_MODEL, tn), lambda i, j: (i, 0, j)),
                  pl.BlockSpec((1, 1, tn), lambda i, j: (i, 0, j))],
        out_specs=pl.BlockSpec((1, rows, tn), lambda i, j: (i, 0, j)),
        out_shape=jax.ShapeDtypeStruct((DEPTH, rows, n), F32),
        compiler_params=_cparams("arbitrary", "arbitrary"),
        name="ada_mod",
    )(c_all, ada_w, ada_b.reshape(DEPTH, 1, n))


def _conv_put(u, ext_ref, col0, nb, tt):
    n = u.shape[-1]
    ext_ref[:, 8:8 + tt, col0:col0 + n] = u.reshape(nb, tt, n)


def _conv_get(ext_ref, col0, n, w_ref, width, nb, tt):
    hw = width - 1
    cols = slice(col0, col0 + n)
    y = ext_ref[:, 8:8 + tt, cols] * w_ref[hw:hw + 1, cols]
    for j in range(hw):
        y = y + ext_ref[:, 8 - hw + j:8 - hw + j + tt, cols] * w_ref[j:j + 1, cols]
    ext_ref[:, 8 - hw:8, cols] = ext_ref[:, 8 + tt - hw:8 + tt, cols]
    return y


FFN_COLS = 256


def _ffn_kernel(x_ref, mod_ref, ng_ref, win_ref, cw_ref, cb_ref, wout_ref, buf_ref,
                y_ref, nbuf_ref, ext_ref, *, nb, tt):
    m = nb * tt

    @pl.when(pl.program_id(1) == 0)
    def _():
        ext_ref[:, 6:8, :] = buf_ref[...]

    x = x_ref[...]
    h = _norm_mod(x, ng_ref[...], mod_ref[:, 4:5, :], mod_ref[:, 3:4, :])
    h = h.reshape(m, D_MODEL).astype(BF16)
    acc = jnp.zeros((m, D_MODEL), F32)
    nchunk = D_FF // FFN_COLS

    for c in range(nchunk):
        for col0 in (c * FFN_COLS, D_FF + c * FFN_COLS):
            _conv_put(jnp.dot(h, win_ref[:, col0:col0 + FFN_COLS], preferred_element_type=F32),
                      ext_ref, col0, nb, tt)
    for c in range(nchunk):
        g0 = c * FFN_COLS
        v0 = D_FF + c * FFN_COLS
        gate = _silu(_conv_get(ext_ref, g0, FFN_COLS, cw_ref, 3, nb, tt) + cb_ref[:, g0:g0 + FFN_COLS])
        val = _conv_get(ext_ref, v0, FFN_COLS, cw_ref, 3, nb, tt) + cb_ref[:, v0:v0 + FFN_COLS]
        a = (gate * val).reshape(m, FFN_COLS).astype(BF16)
        acc = acc + jnp.dot(a, wout_ref[g0:g0 + FFN_COLS, :], preferred_element_type=F32)
    y_ref[...] = x + mod_ref[:, 5:6, :] * acc.reshape(nb, tt, D_MODEL)
    nbuf_ref[...] = ext_ref[:, 6:8, :]


def _ffn(x, mod, ng, win, cw, cb, wout, buf, nb, tt):
    b, t, d = x.shape
    n2 = 2 * D_FF
    kern = functools.partial(_ffn_kernel, nb=nb, tt=tt)
    return pl.pallas_call(
        kern,
        grid=(b // nb, t // tt),
        in_specs=[pl.BlockSpec((nb, tt, d), lambda i, j: (i, j, 0)),
                  pl.BlockSpec((nb, 8, d), lambda i, j: (i, 0, 0)),
                  pl.BlockSpec((1, d), lambda i, j: (0, 0)),
                  _wspec((d, n2)),
                  pl.BlockSpec((3, n2), lambda i, j: (0, 0)),
                  pl.BlockSpec((1, n2), lambda i, j: (0, 0)),
                  _wspec((D_FF, d)),
                  pl.BlockSpec((nb, 2, n2), lambda i, j: (i, 0, 0))],
        out_specs=[pl.BlockSpec((nb, tt, d), lambda i, j: (i, j, 0)),
                   pl.BlockSpec((nb, 2, n2), lambda i, j: (i, 0, 0))],
        out_shape=[jax.ShapeDtypeStruct((b, t, d), F32),
                   jax.ShapeDtypeStruct((b, 2, n2), F32)],
        scratch_shapes=[pltpu.VMEM((nb, 8 + tt, n2), F32)],
        compiler_params=_cparams("arbitrary", "arbitrary"),
        name="conv_ffn",
    )(x, mod, ng, win, cw, cb, wout, buf)


def _gdn_proj_kernel(x_ref, mod_ref, ng_ref, wqkv_ref, wz_ref, wab_ref, cw_ref, pv_ref, buf_ref,
                     q_ref, k_ref, v_ref, z_ref, gb_ref, nbuf_ref, ext_ref, *, nb, tt):
    m = nb * tt
    d = D_MODEL
    nslab = GDN_QKV // LANES
    hw = GDN_CONV_TAPS - 1
    slab = lambda sl: slice(sl * LANES, (sl + 1) * LANES)

    @pl.when(pl.program_id(1) == 0)
    def _():
        for sl in range(nslab):
            ext_ref[sl, :, 8 - hw:8, :] = buf_ref[:, :, slab(sl)]

    h = _norm_mod(x_ref[...], ng_ref[...], mod_ref[:, 1:2, :], mod_ref[:, 0:1, :])
    h = h.reshape(m, d).astype(BF16)
    for s in range(GDN_QKV // MXU_N):
        c0 = s * MXU_N
        u = jnp.dot(h, wqkv_ref[:, c0:c0 + MXU_N], preferred_element_type=F32).reshape(nb, tt, MXU_N)
        for half in range(MXU_N // LANES):
            ext_ref[c0 // LANES + half, :, 8:8 + tt, :] = u[:, :, slab(half)]
    z_ref[...] = jnp.dot(h, wz_ref[...], preferred_element_type=F32).reshape(nb, tt, d)
    ab = jnp.dot(h, wab_ref[...], preferred_element_type=F32)
    lane = _iota2(ab.shape, 1)
    gb = jnp.where(lane < GDN_HEADS,
                   pv_ref[0:1, :] * _softplus(ab + pv_ref[1:2, :]),
                   _sigmoid(ab))
    gb_ref[...] = gb.reshape(nb, tt, LANES)
    outs = (q_ref, k_ref, v_ref)
    n4 = tt // CONV_STRIDE
    for sl in range(nslab):
        which, hd = divmod(sl, GDN_HEADS)
        for a in range(CONV_STRIDE):
            y = ext_ref[sl, :, pl.ds(8 + a, n4, stride=CONV_STRIDE), :] * cw_ref[hw:hw + 1, slab(sl)]
            for j in range(hw):
                y = y + (ext_ref[sl, :, pl.ds(8 + a - hw + j, n4, stride=CONV_STRIDE), :]
                         * cw_ref[j:j + 1, slab(sl)])
            y = _silu(y)
            if which < 2:
                r = lax.rsqrt(jnp.sum(y * y, axis=-1, keepdims=True) + NORM_EPS)
                if which == 0:
                    r = r * (GDN_DK ** -0.5)
                y = y * r
            outs[which][:, hd, pl.ds(a, n4, stride=CONV_STRIDE), :] = y
        nbuf_ref[:, :, slab(sl)] = ext_ref[sl, :, 8 + tt - hw:8 + tt, :]
        ext_ref[sl, :, 8 - hw:8, :] = ext_ref[sl, :, 8 + tt - hw:8 + tt, :]


def _gdn_proj(x, mod, ng, wqkv, wz, wab, cw, pv, buf, nb, tt):
    b, t, d = x.shape
    kern = functools.partial(_gdn_proj_kernel, nb=nb, tt=tt)
    row = lambda i, j: (i, j, 0)
    const2 = lambda i, j: (0, 0)
    act = jax.ShapeDtypeStruct((b, t, d), F32)
    heads = jax.ShapeDtypeStruct((b, GDN_HEADS, t, GDN_DK), F32)
    return pl.pallas_call(
        kern,
        grid=(b // nb, t // tt),
        in_specs=[pl.BlockSpec((nb, tt, d), row),
                  pl.BlockSpec((nb, 8, d), lambda i, j: (i, 0, 0)),
                  pl.BlockSpec((1, d), const2),
                  _wspec((d, GDN_QKV)),
                  _wspec((d, d)),
                  pl.BlockSpec((d, LANES), const2),
                  pl.BlockSpec((4, GDN_QKV), const2),
                  pl.BlockSpec((8, LANES), const2),
                  pl.BlockSpec((nb, 3, GDN_QKV), lambda i, j: (i, 0, 0))],
        out_specs=[pl.BlockSpec((nb, GDN_HEADS, tt, GDN_DK), lambda i, j: (i, 0, j, 0))] * 3
                  + [pl.BlockSpec((nb, tt, d), row),
                     pl.BlockSpec((nb, tt, LANES), row),
                     pl.BlockSpec((nb, 3, GDN_QKV), lambda i, j: (i, 0, 0))],
        out_shape=[heads, heads, heads, act,
                   jax.ShapeDtypeStruct((b, t, LANES), F32),
                   jax.ShapeDtypeStruct((b, 3, GDN_QKV), F32)],
        scratch_shapes=[pltpu.VMEM((GDN_QKV // LANES, nb, 8 + tt, LANES), F32)],
        compiler_params=_cparams("arbitrary", "arbitrary"),
        name="gdn_proj",
    )(x, mod, ng, wqkv, wz, wab, cw, pv, buf)


def _tri_inverses(lmats, chunk):
    n = lmats[0].shape[0]
    row = _iota2((n, n), 0)
    col = _iota2((n, n), 1)
    level = 31 - lax.clz(jnp.bitwise_xor(row, col))
    eye = (row == col).astype(F32)
    xs = [eye - jnp.where(level == 0, lm, 0.0) for lm in lmats]
    k = 1
    while (1 << k) < chunk:
        xbs = [x.astype(BF16) for x in xs]
        ps = [_dot(xb, jnp.where(level == k, lm, 0.0)).astype(BF16) for xb, lm in zip(xbs, lmats)]
        xs = [x - _dot(p, xb) for x, p, xb in zip(xs, ps, xbs)]
        k += 1
    return xs


def _gdn_core_kernel(q_ref, k_ref, v_ref, z_ref, gb_ref, x_ref, mod_ref, og_ref, wout_ref, s0_ref,
                     y_ref, sout_ref, s_ref, o_ref, *, tt, chunk):
    @pl.when(pl.program_id(1) == 0)
    def _():
        s_ref[...] = s0_ref[0]

    nck = tt // chunk
    row = _iota2((tt, tt), 0)
    col = _iota2((tt, tt), 1)
    if nck > 1:
        same = (row // chunk) == (col // chunk)
        incl = (row >= col) & same
        strict = (row > col) & same
    else:
        same = None
        incl = row >= col
        strict = row > col
    gb = gb_ref[0]
    gcol = _sel_dot(incl.astype(BF16), gb)
    if nck > 1:
        glast = _sel_dot(same.astype(BF16), gb)
    else:
        glast = jnp.broadcast_to(jnp.sum(gb, axis=0, keepdims=True), gb.shape)
    grow = gcol.T
    eg = jnp.exp(gcol)
    ekd = jnp.exp(glast - gcol)
    egl = jnp.exp(glast)

    heads = range(GDN_HEADS)
    cols = [slice(hd * GDN_DK, (hd + 1) * GDN_DK) for hd in heads]
    kbs, lmats, qks = [], [], []
    for hd in heads:
        kh = k_ref[0, hd]
        diff = gcol[:, hd:hd + 1] - grow[hd:hd + 1, :]
        decay = jnp.exp(jnp.where(incl, diff, -jnp.inf))
        kb = kh * gb[:, GDN_HEADS + hd:GDN_HEADS + hd + 1]
        kbs.append(kb)
        lmats.append(jnp.where(strict, _dot_nt(kb, kh) * decay, 0.0))
        qks.append((_dot_nt(q_ref[0, hd], kh) * decay).astype(BF16))
    tinvs = [t.astype(BF16) for t in _tri_inverses(lmats, chunk)]
    us, ws = [], []
    for hd in heads:
        beta = gb[:, GDN_HEADS + hd:GDN_HEADS + hd + 1]
        uw = _dot(tinvs[hd], jnp.concatenate([v_ref[0, hd] * beta, kbs[hd] * eg[:, hd:hd + 1]], axis=1))
        us.append(uw[:, :GDN_DK])
        ws.append(uw[:, GDN_DK:].astype(BF16))
    ss = [s_ref[hd] for hd in heads]
    vnews = [[] for _ in heads]
    ocross = [[] for _ in heads]
    for c in range(nck):
        rows = slice(c * chunk, (c + 1) * chunk)
        for hd in heads:
            qd = (q_ref[0, hd, rows, :] * eg[rows, hd:hd + 1]).astype(BF16)
            ws_qs = _dot(jnp.concatenate([ws[hd][rows], qd], axis=0), ss[hd])
            vn = us[hd][rows] - ws_qs[:chunk]
            ocross[hd].append(ws_qs[chunk:])
            kd = k_ref[0, hd, rows, :] * ekd[rows, hd:hd + 1]
            ss[hd] = ss[hd] * egl[c * chunk:c * chunk + 1, hd:hd + 1] + _dot_tn(kd, vn)
            vnews[hd].append(vn.astype(BF16))
    for hd in heads:
        s_ref[hd] = ss[hd]
        vnew = jnp.concatenate(vnews[hd], axis=0) if nck > 1 else vnews[hd][0]
        oc = jnp.concatenate(ocross[hd], axis=0) if nck > 1 else ocross[hd][0]
        o = oc + _dot(qks[hd], vnew)
        o = o * lax.rsqrt(jnp.mean(o * o, axis=-1, keepdims=True) + NORM_EPS) * og_ref[...]
        o_ref[:, cols[hd]] = (o * _silu(z_ref[0, :, cols[hd]])).astype(BF16)

    out = jnp.dot(o_ref[...], wout_ref[...], preferred_element_type=F32)
    y_ref[0] = x_ref[0] + mod_ref[0, 2:3, :] * out
    sout_ref[0] = s_ref[...]


def _gdn_core(q, k, v, z, gb, x, mod, og, wout, s0, tt, chunk):
    b, t, d = x.shape
    kern = functools.partial(_gdn_core_kernel, tt=tt, chunk=chunk)
    row = lambda i, j: (i, j, 0)
    return pl.pallas_call(
        kern,
        grid=(b, t // tt),
        in_specs=[pl.BlockSpec((1, GDN_HEADS, tt, GDN_DK), lambda i, j: (i, 0, j, 0))] * 3
                 + [pl.BlockSpec((1, tt, d), row),
                    pl.BlockSpec((1, tt, LANES), row),
                    pl.BlockSpec((1, tt, d), row),
                    pl.BlockSpec((1, 8, d), lambda i, j: (i, 0, 0)),
                    pl.BlockSpec((1, GDN_DK), lambda i, j: (0, 0)),
                    _wspec((d, d)),
                    pl.BlockSpec((1, GDN_HEADS, GDN_DK, GDN_DK), lambda i, j: (i, 0, 0, 0))],
        out_specs=[pl.BlockSpec((1, tt, d), row),
                   pl.BlockSpec((1, GDN_HEADS, GDN_DK, GDN_DK), lambda i, j: (i, 0, 0, 0))],
        out_shape=[jax.ShapeDtypeStruct((b, t, d), F32),
                   jax.ShapeDtypeStruct((b, GDN_HEADS, GDN_DK, GDN_DK), F32)],
        scratch_shapes=[pltpu.VMEM((GDN_HEADS, GDN_DK, GDN_DK), F32),
                        pltpu.VMEM((tt, d), BF16)],
        compiler_params=_cparams("arbitrary", "arbitrary"),
        name="gdn_core",
    )(q, k, v, z, gb, x, mod, og, wout, s0)


def _sb_proj_kernel(x_ref, mod_ref, ng_ref, w_ref, qkg_ref,
                    kf_ref, vf_ref, qb_ref, kb_ref, vb_ref, u_ref, *, nb, tt):
    m = nb * tt
    d = D_MODEL
    h = _norm_mod(x_ref[...], ng_ref[...], mod_ref[:, 1:2, :], mod_ref[:, 0:1, :])
    h = h.reshape(m, d).astype(BF16)
    grp = ((_iota2((MXU_N, MXU_N), 0) // SB_DH) == (_iota2((MXU_N, MXU_N), 1) // SB_DH)).astype(BF16)
    for s in range(3 * d // MXU_N):
        c0 = s * MXU_N
        u_ref[:, c0:c0 + MXU_N] = jnp.dot(h, w_ref[:, c0:c0 + MXU_N], preferred_element_type=F32)
    for s in range(3 * d // MXU_N):
        c0 = s * MXU_N
        which, o0 = divmod(c0, d)
        u = u_ref[:, c0:c0 + MXU_N]
        if which < 2:
            ms = _dot_sel(u * u, grp) * (1.0 / SB_DH)
            u = u * lax.rsqrt(ms + NORM_EPS) * qkg_ref[which:which + 1, o0:o0 + MXU_N]
        u3 = u.reshape(nb, tt, MXU_N)
        if which == 0:
            qb_ref[:, :, o0:o0 + MXU_N] = u3.astype(BF16)
        elif which == 1:
            kf_ref[:, :, o0:o0 + MXU_N] = u3
            kb_ref[:, :, o0:o0 + MXU_N] = u3.astype(BF16)
        else:
            vf_ref[:, :, o0:o0 + MXU_N] = u3
            vb_ref[:, :, o0:o0 + MXU_N] = u3.astype(BF16)


def _sb_proj(x, mod, ng, w, qkg, nb, tt):
    b, t, d = x.shape
    kern = functools.partial(_sb_proj_kernel, nb=nb, tt=tt)
    row = lambda i, j: (i, j, 0)
    const2 = lambda i, j: (0, 0)
    f = jax.ShapeDtypeStruct((b, t, d), F32)
    h = jax.ShapeDtypeStruct((b, t, d), BF16)
    return pl.pallas_call(
        kern,
        grid=(b // nb, t // tt),
        in_specs=[pl.BlockSpec((nb, tt, d), row),
                  pl.BlockSpec((nb, 8, d), lambda i, j: (i, 0, 0)),
                  pl.BlockSpec((1, d), const2),
                  _wspec((d, 3 * d)),
                  pl.BlockSpec((8, d), const2)],
        out_specs=[pl.BlockSpec((nb, tt, d), row)] * 5,
        out_shape=[f, f, h, h, h],
        scratch_shapes=[pltpu.VMEM((nb * tt, 3 * d), F32)],
        compiler_params=_cparams("arbitrary", "arbitrary"),
        name="sb_proj",
    )(x, mod, ng, w, qkg)


def _suffix_selector(bw):
    r = jnp.bitwise_and(_iota2((2 * bw, bw + LANES), 0), bw - 1)
    c = _iota2((2 * bw, bw + LANES), 1)
    return jnp.where(c < bw, (r > c).astype(F32), 1.0).astype(BF16)


def _dot_nt_bf(a, b):
    return lax.dot_general(a, b, (((1,), (1,)), ((), ())), preferred_element_type=F32)


def _sb_tile(qh, kt, vt, carry, acc, mask, sel):
    n = kt.shape[0]
    bw = min(LANES, n)
    z = _dot_nt_bf(qh, kt)
    sp = jnp.maximum(z, 0.0) + jnp.log(1.0 + jnp.exp(-jnp.abs(z)))
    spm = sp if mask is None else jnp.where(mask, sp, 0.0)
    ws = []
    for mb in reversed(range(n // bw)):
        sl = slice(mb * bw, (mb + 1) * bw)
        hi = spm[:, sl].astype(BF16)
        lo = (spm[:, sl] - hi.astype(F32)).astype(BF16)
        r = jnp.dot(jnp.concatenate([hi, lo], axis=1), sel[bw], preferred_element_type=F32)
        w = jnp.exp(z[:, sl] - sp[:, sl] - carry[:, :bw] - r[:, :bw])
        if mask is not None:
            w = jnp.where(mask[:, sl], w, 0.0)
        ws.append(w.astype(BF16))
        carry = carry + r[:, bw:]
    wts = jnp.concatenate(ws[::-1], axis=-1) if len(ws) > 1 else ws[0]
    acc = acc + jnp.dot(wts, vt, preferred_element_type=F32)
    return carry, acc


def _sb_head_queries(q):
    q = q * (SB_DH ** -0.5)
    lane = _iota2(q.shape, 1)
    return [jnp.where((lane // SB_DH) == hd, q, jnp.zeros_like(q)) for hd in range(2)]


def _sb_fast_kernel(q_ref, kd_ref, vd_ref, kp_ref, vp_ref, o_ref, need_ref, st_ref, *, tq, more_before_prev):
    tp = kp_ref.shape[1]
    npair = D_MODEL // LANES
    sel = {bw: _suffix_selector(bw) for bw in {min(LANES, tq), min(LANES, tp)}}
    dmask = _iota2((tq, tq), 1) < _iota2((tq, tq), 0)
    zero = jnp.zeros((tq, LANES), F32)
    pairs = [slice(hp * LANES, (hp + 1) * LANES) for hp in range(npair)]

    for hp in range(npair):
        qhs = _sb_head_queries(q_ref[0, :, pairs[hp]])
        for hd in range(2):
            c, a = _sb_tile(qhs[hd], kd_ref[0, :, pairs[hp]], vd_ref[0, :, pairs[hp]], zero, zero, dmask, sel)
            st_ref[hp, 2 * hd] = c
            st_ref[hp, 2 * hd + 1] = a

    for hp in range(npair):
        qhs = _sb_head_queries(q_ref[0, :, pairs[hp]])
        for hd in range(2):
            c, a = _sb_tile(qhs[hd], kp_ref[0, :, pairs[hp]], vp_ref[0, :, pairs[hp]],
                            st_ref[hp, 2 * hd], st_ref[hp, 2 * hd + 1], None, sel)
            st_ref[hp, 2 * hd] = c
            st_ref[hp, 2 * hd + 1] = a

    lane = _iota2((tq, LANES), 1)
    cmin = None
    for hp in range(npair):
        o_ref[0, :, pairs[hp]] = jnp.where((lane // SB_DH) == 0, st_ref[hp, 1], st_ref[hp, 3]).astype(BF16)
        c = jnp.minimum(st_ref[hp, 0], st_ref[hp, 2])
        cmin = c if cmin is None else jnp.minimum(cmin, c)
    alive = jnp.where(jnp.min(cmin, axis=0, keepdims=True) < SB_DEAD, 1.0, 0.0)
    if not more_before_prev:
        alive = jnp.zeros_like(alive)
    need_ref[0, 0] = jnp.broadcast_to(alive, (8, LANES))


def _sb_fast(qb, kb, vb, prev_k, prev_v, tq, more_before_prev):
    b, t, d = qb.shape
    tp = prev_k.shape[1]
    kern = functools.partial(_sb_fast_kernel, tq=tq, more_before_prev=more_before_prev)
    cur = pl.BlockSpec((1, tq, d), lambda i, j: (i, j, 0))
    prev = pl.BlockSpec((1, tp, d), lambda i, j: (i, 0, 0))
    return pl.pallas_call(
        kern,
        grid=(b, t // tq),
        in_specs=[cur, cur, cur, prev, prev],
        out_specs=[cur, pl.BlockSpec((1, 1, 8, LANES), lambda i, j: (i, j, 0, 0))],
        out_shape=[jax.ShapeDtypeStruct((b, t, d), BF16),
                   jax.ShapeDtypeStruct((b, t // tq, 8, LANES), F32)],
        scratch_shapes=[pltpu.VMEM((d // LANES, 4, tq, LANES), F32)],
        compiler_params=_cparams("arbitrary", "arbitrary"),
        name="sb_fast",
    )(qb, kb, vb, prev_k, prev_v)


def _sb_attention(qb, kb, vb, past_k, past_v, tq):
    b, t, d = qb.shape
    if past_k is None:
        return _sb_attn(qb, kb, vb, None, None, tq)
    assert t == tq
    plen = past_k.shape[1]
    tp = min(256, plen)
    tail = lambda c: c[:, plen - tp:].reshape(b, tp, d).astype(BF16)
    o, need = _sb_fast(qb, kb, vb, tail(past_k), tail(past_v), tq, plen > tp)
    full = lambda: _sb_attn(qb, kb, vb, past_k.reshape(b, plen, d), past_v.reshape(b, plen, d), tq)
    return lax.cond(jnp.max(need) > 0.0, full, lambda: o)


def _sb_attn_kernel(*refs, tq, tk, past_len):
    if past_len:
        q_ref, k_ref, v_ref, pk_ref, pv_ref, o_ref = refs
    else:
        q_ref, k_ref, v_ref, o_ref = refs
        pk_ref = pv_ref = None
    qi = pl.program_id(2)
    lane = _iota2((tq, LANES), 1)
    pairs = [slice(p * LANES, (p + 1) * LANES) for p in range(SB_PAIRS)]
    qhs = [qh for p in range(SB_PAIRS) for qh in _sb_head_queries(q_ref[0, :, pairs[p]])]
    ts = tq
    sel = {bw: _suffix_selector(bw) for bw in {min(LANES, ts), min(LANES, tk)}}

    def both(kt, vt, state, mask):
        new = ()
        for hd in range(2 * SB_PAIRS):
            new += _sb_tile(qhs[hd], kt[:, pairs[hd // 2]], vt[:, pairs[hd // 2]],
                            state[2 * hd], state[2 * hd + 1], mask, sel)
        return new

    zero = jnp.zeros((tq, LANES), F32)
    r0 = pl.multiple_of(qi * tq, tq)
    dmask = _iota2((tq, tq), 1) < _iota2((tq, tq), 0)
    state = both(k_ref[0, pl.ds(r0, tq), :], v_ref[0, pl.ds(r0, tq), :], (zero,) * (4 * SB_PAIRS), dmask)

    def alive(st):
        cmin = st[0]
        for hd in range(1, 2 * SB_PAIRS):
            cmin = jnp.minimum(cmin, st[2 * hd])
        return jnp.min(cmin) < SB_DEAD

    def key_loop(ntiles, size, load, st):
        def cond(c):
            return jnp.logical_and(c[0] < ntiles, c[1])

        def body(c):
            kt, vt = load(pl.multiple_of((ntiles - 1 - c[0]) * size, size))
            nst = both(kt, vt, c[2:], None)
            return (c[0] + 1, alive(nst)) + nst

        return lax.while_loop(cond, body, (jnp.int32(0), alive(st)) + st)[2:]

    state = key_loop(qi * (tq // ts), ts,
                     lambda k0: (k_ref[0, pl.ds(k0, ts), :], v_ref[0, pl.ds(k0, ts), :]), state)
    if past_len:
        state = key_loop(past_len // tk, tk,
                         lambda k0: (pk_ref[0, pl.ds(k0, tk), :].astype(BF16),
                                     pv_ref[0, pl.ds(k0, tk), :].astype(BF16)), state)
    for p in range(SB_PAIRS):
        o_ref[0, :, pairs[p]] = jnp.where((lane // SB_DH) == 0, state[4 * p + 1], state[4 * p + 3]).astype(BF16)


def _sb_attn(qb, kb, vb, past_k, past_v, tq):
    b, t, d = qb.shape
    past_len = 0 if past_k is None else past_k.shape[1]
    tk = tq if not past_len else min(256, past_len)
    kern = functools.partial(_sb_attn_kernel, tq=tq, tk=tk, past_len=past_len)
    width = SB_PAIRS * LANES
    qspec = pl.BlockSpec((1, tq, width), lambda i, h, j: (i, j, h))
    kvspec = pl.BlockSpec((1, t, width), lambda i, h, j: (i, 0, h), pipeline_mode=pl.Buffered(1))
    in_specs = [qspec, kvspec, kvspec]
    args = [qb, kb, vb]
    if past_len:
        pspec = pl.BlockSpec((1, past_len, width), lambda i, h, j: (i, 0, h))
        in_specs += [pspec, pspec]
        args += [past_k, past_v]
    return pl.pallas_call(
        kern,
        grid=(b, d // width, t // tq),
        in_specs=in_specs,
        out_specs=qspec,
        out_shape=jax.ShapeDtypeStruct((b, t, d), BF16),
        compiler_params=_cparams("arbitrary", "arbitrary", "arbitrary"),
        name="sb_attn",
    )(*args)


def _out_proj_kernel(o_ref, x_ref, mod_ref, w_ref, y_ref, *, nb, tt):
    out = jnp.dot(o_ref[...].reshape(nb * tt, -1), w_ref[...], preferred_element_type=F32)
    y_ref[...] = x_ref[...] + mod_ref[:, 2:3, :] * out.reshape(nb, tt, D_MODEL)


def _out_proj(o, x, mod, w, nb, tt):
    b, t, d = x.shape
    kin = o.shape[-1]
    kern = functools.partial(_out_proj_kernel, nb=nb, tt=tt)
    row = lambda i, j: (i, j, 0)
    return pl.pallas_call(
        kern,
        grid=(b // nb, t // tt),
        in_specs=[pl.BlockSpec((nb, tt, kin), row),
                  pl.BlockSpec((nb, tt, d), row),
                  pl.BlockSpec((nb, 8, d), lambda i, j: (i, 0, 0)),
                  _wspec((kin, d))],
        out_specs=pl.BlockSpec((nb, tt, d), row),
        out_shape=jax.ShapeDtypeStruct((b, t, d), F32),
        compiler_params=_cparams("arbitrary", "arbitrary"),
        name="out_proj",
    )(o, x, mod, w)


def _ret_log_gamma(hd):
    return math.log1p(-(2.0 ** (-5.0 - hd)))


def _ret_proj_kernel(x_ref, mod_ref, ng_ref, w_ref, cos_ref, sin_ref,
                     q_ref, qd_ref, k_ref, kd_ref, v_ref, gt_ref, *, nb, tt, chunk):
    m = nb * tt
    d = D_MODEL
    hk = RET_HEADS * RET_DK
    hv = RET_HEADS * RET_DV
    h = _norm_mod(x_ref[...], ng_ref[...], mod_ref[:, 1:2, :], mod_ref[:, 0:1, :])
    h = h.reshape(m, d).astype(BF16)
    cos = cos_ref[...][None]
    sin = sin_ref[...][None]
    tpos = pl.program_id(1) * tt + _iota2((1, tt, 1), 1)
    assert chunk & (chunk - 1) == 0
    idx = jnp.bitwise_and(tpos, chunk - 1).astype(F32)
    half = RET_DK // 2
    for s in range(2 * hk // MXU_N):
        c0 = s * MXU_N
        which, o0 = divmod(c0, hk)
        hd = o0 // RET_DK
        lg = _ret_log_gamma(hd)
        u = jnp.dot(h, w_ref[:, c0:c0 + MXU_N], preferred_element_type=F32).reshape(nb, tt, MXU_N)
        x1 = u[:, :, :half]
        x2 = u[:, :, half:]
        r = jnp.concatenate([x1 * cos - x2 * sin, x1 * sin + x2 * cos], axis=-1)
        if which == 0:
            q_ref[:, :, o0:o0 + MXU_N] = r.astype(BF16)
            qd_ref[:, :, o0:o0 + MXU_N] = (r * jnp.exp(lg * (idx + 1.0))).astype(BF16)
        else:
            r = r * (RET_DK ** -0.5)
            k_ref[:, :, o0:o0 + MXU_N] = r.astype(BF16)
            kd_ref[:, :, o0:o0 + MXU_N] = (r * jnp.exp(lg * (chunk - 1.0 - idx))).astype(BF16)
    for s in range(2 * hv // MXU_N):
        c0 = 2 * hk + s * MXU_N
        u = jnp.dot(h, w_ref[:, c0:c0 + MXU_N], preferred_element_type=F32).reshape(nb, tt, MXU_N)
        o0 = s * MXU_N
        if o0 < hv:
            v_ref[:, :, o0:o0 + MXU_N] = u.astype(BF16)
        else:
            gt_ref[:, :, o0 - hv:o0 - hv + MXU_N] = u


def _ret_proj(x, mod, ng, w, cos, sin, nb, tt, chunk):
    b, t, d = x.shape
    hk = RET_HEADS * RET_DK
    hv = RET_HEADS * RET_DV
    kern = functools.partial(_ret_proj_kernel, nb=nb, tt=tt, chunk=chunk)
    row = lambda i, j: (i, j, 0)
    const2 = lambda i, j: (0, 0)
    qs = jax.ShapeDtypeStruct((b, t, hk), BF16)
    return pl.pallas_call(
        kern,
        grid=(b // nb, t // tt),
        in_specs=[pl.BlockSpec((nb, tt, d), row),
                  pl.BlockSpec((nb, 8, d), lambda i, j: (i, 0, 0)),
                  pl.BlockSpec((1, d), const2),
                  _wspec((d, 2 * hk + 2 * hv)),
                  pl.BlockSpec((tt, RET_DK // 2), lambda i, j: (j, 0)),
                  pl.BlockSpec((tt, RET_DK // 2), lambda i, j: (j, 0))],
        out_specs=[pl.BlockSpec((nb, tt, hk), row)] * 4
                  + [pl.BlockSpec((nb, tt, hv), row)] * 2,
        out_shape=[qs, qs, qs, qs,
                   jax.ShapeDtypeStruct((b, t, hv), BF16),
                   jax.ShapeDtypeStruct((b, t, hv), F32)],
        compiler_params=_cparams("arbitrary", "arbitrary"),
        name="ret_proj",
    )(x, mod, ng, w, cos, sin)


def _ret_core_kernel(q_ref, qd_ref, k_ref, kd_ref, v_ref, gt_ref, x_ref, mod_ref, og_ref, wout_ref, r0_ref,
                     y_ref, rout_ref, r_ref, o_ref, *, chunk):
    @pl.when(pl.program_id(1) == 0)
    def _():
        r_ref[...] = r0_ref[0]

    rel = (_iota2((chunk, chunk), 0) - _iota2((chunk, chunk), 1)).astype(F32)
    for hd in range(RET_HEADS):
        lg = _ret_log_gamma(hd)
        k0 = hd * RET_DK
        v0 = hd * RET_DV
        dmask = jnp.where(rel >= 0, jnp.exp(lg * jnp.maximum(rel, 0.0)), 0.0)
        vh = v_ref[0, :, v0:v0 + RET_DV]
        s = _dot_nt_bf(q_ref[0, :, k0:k0 + RET_DK], k_ref[0, :, k0:k0 + RET_DK]) * dmask
        r = r_ref[hd]
        o = (jnp.dot(s.astype(BF16), vh, preferred_element_type=F32)
             + jnp.dot(qd_ref[0, :, k0:k0 + RET_DK], r.astype(BF16), preferred_element_type=F32))
        r_ref[hd] = r * math.exp(lg * chunk) + lax.dot_general(
            kd_ref[0, :, k0:k0 + RET_DK], vh, (((0,), (0,)), ((), ())), preferred_element_type=F32)
        mu = jnp.mean(o, axis=-1, keepdims=True)
        dlt = o - mu
        var = jnp.mean(dlt * dlt, axis=-1, keepdims=True)
        o = dlt * lax.rsqrt(var + NORM_EPS) * og_ref[:, v0:v0 + RET_DV]
        o_ref[:, v0:v0 + RET_DV] = (o * _silu(gt_ref[0, :, v0:v0 + RET_DV])).astype(BF16)
    out = jnp.dot(o_ref[...], wout_ref[...], preferred_element_type=F32)
    y_ref[0] = x_ref[0] + mod_ref[0, 2:3, :] * out
    rout_ref[0] = r_ref[...]


def _ret_core(q, qd, k, kd, v, gt, x, mod, og, wout, r0, chunk):
    b, t, d = x.shape
    hk = RET_HEADS * RET_DK
    hv = RET_HEADS * RET_DV
    kern = functools.partial(_ret_core_kernel, chunk=chunk)
    row = lambda i, j: (i, j, 0)
    st = pl.BlockSpec((1, RET_HEADS, RET_DK, RET_DV), lambda i, j: (i, 0, 0, 0))
    return pl.pallas_call(
        kern,
        grid=(b, t // chunk),
        in_specs=[pl.BlockSpec((1, chunk, hk), row)] * 4
                 + [pl.BlockSpec((1, chunk, hv), row)] * 2
                 + [pl.BlockSpec((1, chunk, d), row),
                    pl.BlockSpec((1, 8, d), lambda i, j: (i, 0, 0)),
                    pl.BlockSpec((1, hv), lambda i, j: (0, 0)),
                    _wspec((hv, d)),
                    st],
        out_specs=[pl.BlockSpec((1, chunk, d), row), st],
        out_shape=[jax.ShapeDtypeStruct((b, t, d), F32),
                   jax.ShapeDtypeStruct((b, RET_HEADS, RET_DK, RET_DV), F32)],
        scratch_shapes=[pltpu.VMEM((RET_HEADS, RET_DK, RET_DV), F32),
                        pltpu.VMEM((chunk, hv), BF16)],
        compiler_params=_cparams("arbitrary", "arbitrary"),
        name="ret_core",
    )(q, qd, k, kd, v, gt, x, mod, og, wout, r0)


def _run_group(x, mod_all, states, ffn_bufs, pos0, wts):
    b, t, d = x.shape
    if t >= ROW_TILE:
        nb, tt = 1, ROW_TILE
    else:
        nb, tt = b, t
    new_states, new_ffn = [], []
    for i in range(DEPTH):
        kind, j = i % 3, i // 3
        mod = mod_all[i]
        if kind == 0:
            s0, cbuf = states[i]
            gw = wts["gdn"][j]
            q, k, v, z, gb, ncbuf = _gdn_proj(x, mod, wts["norm_mix_g"][i], gw["wqkv"], gw["wz"], gw["wab"],
                                              gw["conv_w"], gw["pv"], cbuf, nb, tt)
            x, s_new = _gdn_core(q, k, v, z, gb, x, mod, gw["norm_g"], gw["wout"], s0,
                                 min(GDN_TILE, t), min(CHUNK, t))
            new_states.append((s_new, ncbuf))
        elif kind == 1:
            pk, pv = states[i]
            sw = wts["sb"][j]
            kf, vf, qb, kb, vb = _sb_proj(x, mod, wts["norm_mix_g"][i], sw["win"], sw["qkg"], nb, tt)
            o = _sb_attention(qb, kb, vb, pk, pv, min(SB_TILE, t))
            x = _out_proj(o, x, mod, sw["wout"], nb, tt)
            new_states.append((kf.reshape(b, t, SB_HEADS, SB_DH), vf.reshape(b, t, SB_HEADS, SB_DH)))
        else:
            (r0,) = states[i]
            rw = wts["ret"][j]
            chunk = min(RET_CHUNK, t)
            half = RET_DK // 2
            inv_freq = RET_ROPE_BASE ** (-jnp.arange(half, dtype=F32) / half)
            ang = (pos0 + jnp.arange(t)).astype(F32)[:, None] * inv_freq[None, :]
            q, qd, k, kd, v, gt = _ret_proj(x, mod, wts["norm_mix_g"][i], rw["win"], jnp.cos(ang), jnp.sin(ang),
                                            nb, tt, chunk)
            x, r_new = _ret_core(q, qd, k, kd, v, gt, x, mod, rw["norm_g"], rw["wout"], r0, chunk)
            new_states.append((r_new,))
        fw = wts["ffn"][i]
        x, fbuf = _ffn(x, mod, wts["norm_ffn_g"][i], fw["win"], fw["conv_w"], fw["conv_b"], fw["wout"],
                       ffn_bufs[i], nb, tt)
        new_ffn.append(fbuf)
    return x, new_states, jnp.stack(new_ffn)


def kernel(x_prompt, x_sample, state_l0_gdn_S, state_l0_gdn_conv, cache_l1_sb_k, cache_l1_sb_v, state_l2_ret, state_l3_gdn_S, state_l3_gdn_conv, state_ffn_conv, c_prompt, c_sample, ada_w, ada_b, norm_mix_g, norm_ffn_g, gdn_w_in, gdn_conv_w, gdn_A_log, gdn_dt_bias, gdn_norm_g, gdn_w_out, sb_w_in, sb_q_norm_g, sb_k_norm_g, sb_w_out, ret_w_in, ret_norm_g, ret_w_out, ffn_w_in, ffn_conv_w, ffn_conv_b, ffn_w_out):
    d = D_MODEL
    bp, tp, _ = x_prompt.shape
    bs, ts, _ = x_sample.shape

    hv = GDN_HEADS * GDN_DK
    wts = {"norm_mix_g": [norm_mix_g[i].reshape(1, d) for i in range(DEPTH)],
           "norm_ffn_g": [norm_ffn_g[i].reshape(1, d) for i in range(DEPTH)],
           "gdn": [], "sb": [], "ret": [], "ffn": []}
    for j in range(gdn_w_in.shape[0]):
        w = gdn_w_in[j]
        wab = jnp.pad(w[:, GDN_QKV + hv:], ((0, 0), (0, LANES - 2 * GDN_HEADS)))
        pv = jnp.zeros((8, LANES), F32)
        pv = pv.at[0, :GDN_HEADS].set(-jnp.exp(gdn_A_log[j].astype(F32)))
        pv = pv.at[1, :GDN_HEADS].set(gdn_dt_bias[j].astype(F32))
        wts["gdn"].append({"wqkv": w[:, :GDN_QKV].astype(BF16),
                           "wz": w[:, GDN_QKV:GDN_QKV + hv].astype(BF16),
                           "wab": wab.astype(BF16),
                           "conv_w": gdn_conv_w[j], "pv": pv,
                           "norm_g": gdn_norm_g[j].reshape(1, GDN_DK),
                           "wout": gdn_w_out[j].astype(BF16)})
    for j in range(sb_w_in.shape[0]):
        qkg = jnp.zeros((8, d), F32)
        qkg = qkg.at[0].set(jnp.tile(sb_q_norm_g[j], SB_HEADS)).at[1].set(jnp.tile(sb_k_norm_g[j], SB_HEADS))
        wts["sb"].append({"win": sb_w_in[j].astype(BF16), "qkg": qkg, "wout": sb_w_out[j].astype(BF16)})
    for j in range(ret_w_in.shape[0]):
        wts["ret"].append({"win": ret_w_in[j].astype(BF16),
                           "norm_g": ret_norm_g[j].reshape(1, RET_HEADS * RET_DV),
                           "wout": ret_w_out[j].astype(BF16)})
    for i in range(DEPTH):
        wts["ffn"].append({"win": ffn_w_in[i].astype(BF16), "conv_w": ffn_conv_w[i],
                           "conv_b": ffn_conv_b[i].reshape(1, 2 * D_FF), "wout": ffn_w_out[i].astype(BF16)})

    nrow = bp + bs
    rows = -(-nrow // 8) * 8
    c_all = jnp.pad(jnp.concatenate([c_prompt, c_sample], axis=0), ((0, rows - nrow), (0, 0)))
    mod = _ada_mod(c_all, ada_w, ada_b)
    mod = jnp.pad(mod.reshape(DEPTH, rows, 6, d), ((0, 0), (0, 0), (0, 2), (0, 0)))
    mod_p, mod_s = mod[:, :bp], mod[:, bp:nrow]

    dt = x_prompt.dtype
    zero_states = [(jnp.zeros((bp, GDN_HEADS, GDN_DK, GDN_DK), dt), jnp.zeros((bp, 3, GDN_QKV), dt)),
                   (None, None),
                   (jnp.zeros((bp, RET_HEADS, RET_DK, RET_DV), dt),),
                   (jnp.zeros((bp, GDN_HEADS, GDN_DK, GDN_DK), dt), jnp.zeros((bp, 3, GDN_QKV), dt))]
    zero_ffn = jnp.zeros((DEPTH, bp, 2, 2 * D_FF), dt)
    y_prompt, p_states, p_ffn_conv = _run_group(x_prompt, mod_p, zero_states, zero_ffn, 0, wts)

    past_len = cache_l1_sb_k.shape[1]
    sample_states = [(state_l0_gdn_S, state_l0_gdn_conv),
                     (cache_l1_sb_k, cache_l1_sb_v),
                     (state_l2_ret,), (state_l3_gdn_S, state_l3_gdn_conv)]
    y_sample, s_states, s_ffn_conv = _run_group(x_sample, mod_s, sample_states, state_ffn_conv, past_len, wts)

    (p_l0_S, p_l0_conv), (p_l1_k, p_l1_v), (p_l2_R,), (p_l3_S, p_l3_conv) = p_states
    (s_l0_S, s_l0_conv), (s_l1_k, s_l1_v), (s_l2_R,), (s_l3_S, s_l3_conv) = s_states
    return (y_prompt, y_sample,
            p_l0_S, p_l0_conv, p_l1_k, p_l1_v, p_l2_R, p_l3_S, p_l3_conv, p_ffn_conv,
            s_l0_S, s_l0_conv, s_l1_k, s_l1_v, s_l2_R, s_l3_S, s_l3_conv, s_ffn_conv)
```

```python
import functools
import math

import jax
import jax.numpy as jnp
from jax import lax
from jax.experimental import pallas as pl
from jax.experimental.pallas import tpu as pltpu

F32 = jnp.float32
BF16 = jnp.bfloat16

D_MODEL = 1024
DEPTH = 4
CHUNK = 64
GDN_HEADS = 8
GDN_DK = 128
GDN_QKV = 3 * GDN_HEADS * GDN_DK
GDN_CONV_TAPS = 4
GDN_AHEAD = 12
CONV_STRIDE = 4
SB_HEADS = 16
SB_DH = 64
RET_HEADS = 4
RET_DK = 256
RET_DV = 512
RET_ROPE_BASE = 10000.0
D_FF = 2816
NORM_EPS = 1e-6

LANES = 128
TILE_ROW0 = 8
MXU_N = 256
VMEM_LIMIT = 56 * 1024 * 1024

ROW_TILE = 512
GDN_TILE = 256
RET_CHUNK = 256
SB_TILE = 256
SB_DEAD = 105.0
SB_PAIRS = 4


def _cparams(*sem):
    return pltpu.CompilerParams(dimension_semantics=sem, vmem_limit_bytes=VMEM_LIMIT)


def _wspec(shape):
    return pl.BlockSpec(shape, lambda i, j: (0,) * len(shape), pipeline_mode=pl.Buffered(1))


def _dot(a, b):
    return jnp.dot(a.astype(BF16), b.astype(BF16), preferred_element_type=F32)


def _dot_nt(a, b):
    return lax.dot_general(a.astype(BF16), b.astype(BF16), (((1,), (1,)), ((), ())),
                           preferred_element_type=F32)


def _dot_tn(a, b):
    return lax.dot_general(a.astype(BF16), b.astype(BF16), (((0,), (0,)), ((), ())),
                           preferred_element_type=F32)


def _split(x):
    hi = x.astype(BF16)
    lo = (x - hi.astype(F32)).astype(BF16)
    return hi, lo


def _dot_sel(a, sel):
    hi, lo = _split(a)
    return (jnp.dot(hi, sel, preferred_element_type=F32)
            + jnp.dot(lo, sel, preferred_element_type=F32))


def _sel_dot(sel, b):
    hi, lo = _split(b)
    return (jnp.dot(sel, hi, preferred_element_type=F32)
            + jnp.dot(sel, lo, preferred_element_type=F32))


def _dot3(a, b):
    ah, al = _split(a)
    bh, bl = _split(b)
    return (jnp.dot(ah, bh, preferred_element_type=F32)
            + jnp.dot(ah, bl, preferred_element_type=F32)
            + jnp.dot(al, bh, preferred_element_type=F32))


def _sigmoid(x):
    return 1.0 / (1.0 + jnp.exp(-x))


def _silu(x):
    return x * _sigmoid(x)


def _softplus(x):
    return jnp.maximum(x, 0.0) + jnp.log(1.0 + jnp.exp(-jnp.abs(x)))


def _norm_mod(x, gain, scale, shift):
    ms = jnp.mean(x * x, axis=-1, keepdims=True)
    y = x * lax.rsqrt(ms + NORM_EPS) * gain
    return y * (1.0 + scale) + shift


def _iota2(shape, dim):
    return lax.broadcasted_iota(jnp.int32, shape, dim)


def _ada_kernel(c_ref, w_ref, b_ref, o_ref):
    o_ref[0] = _dot(_silu(c_ref[...]), w_ref[0]) + b_ref[0]


def _ada_mod(c_all, ada_w, ada_b):
    rows = c_all.shape[0]
    n = ada_w.shape[2]
    tn = 1536
    return pl.pallas_call(
        _ada_kernel,
        grid=(DEPTH, n // tn),
        in_specs=[pl.BlockSpec((rows, D_MODEL), lambda i, j: (0, 0)),
                  pl.BlockSpec((1, D_MODEL, tn), lambda i, j: (i, 0, j)),
                  pl.BlockSpec((1, 1, tn), lambda i, j: (i, 0, j))],
        out_specs=pl.BlockSpec((1, rows, tn), lambda i, j: (i, 0, j)),
        out_shape=jax.ShapeDtypeStruct((DEPTH, rows, n), F32),
        compiler_params=_cparams("arbitrary", "arbitrary"),
        name="ada_mod",
    )(c_all, ada_w, ada_b.reshape(DEPTH, 1, n))


def _slab(sl):
    return slice(sl * LANES, (sl + 1) * LANES)


def _conv_class(ext_ref, sl, a, w_ref, width, n4):
    hw = width - 1
    y = ext_ref[sl, :, pl.ds(TILE_ROW0 + a, n4, stride=CONV_STRIDE), :] * w_ref[hw:hw + 1, _slab(sl)]
    for j in range(hw):
        y = y + ext_ref[sl, :, pl.ds(TILE_ROW0 + a - hw + j, n4, stride=CONV_STRIDE), :] * w_ref[j:j + 1, _slab(sl)]
    return y


FFN_COLS = 256
FFN_CONV_TAPS = 3


def _ffn_kernel(x_ref, mod_ref, ng_ref, win_ref, cw_ref, cb_ref, wout_ref, buf_ref,
                y_ref, nbuf_ref, ext_ref, act_ref, *, nb, tt):
    m = nb * tt
    hw = FFN_CONV_TAPS - 1
    nslab = 2 * D_FF // LANES
    nchunk = D_FF // FFN_COLS
    per = FFN_COLS // LANES
    n4 = tt // CONV_STRIDE

    @pl.when(pl.program_id(1) == 0)
    def _():
        for sl in range(nslab):
            ext_ref[sl, :, TILE_ROW0 - hw:TILE_ROW0, :] = buf_ref[:, :, _slab(sl)]

    x = x_ref[...]
    h = _norm_mod(x, ng_ref[...], mod_ref[:, 4:5, :], mod_ref[:, 3:4, :])
    h = h.reshape(m, D_MODEL).astype(BF16)
    acc = jnp.zeros((m, D_MODEL), F32)

    for c in range(nchunk):
        for col0 in (c * FFN_COLS, D_FF + c * FFN_COLS):
            u = jnp.dot(h, win_ref[:, col0:col0 + FFN_COLS], preferred_element_type=F32).reshape(nb, tt, FFN_COLS)
            for half in range(per):
                ext_ref[col0 // LANES + half, :, TILE_ROW0:TILE_ROW0 + tt, :] = u[:, :, _slab(half)]
    for c in range(nchunk):
        for half in range(per):
            sg = c * per + half
            sv = sg + nslab // 2
            for a in range(CONV_STRIDE):
                gate = _silu(_conv_class(ext_ref, sg, a, cw_ref, FFN_CONV_TAPS, n4) + cb_ref[:, _slab(sg)])
                val = _conv_class(ext_ref, sv, a, cw_ref, FFN_CONV_TAPS, n4) + cb_ref[:, _slab(sv)]
                act_ref[sg, :, pl.ds(a, n4, stride=CONV_STRIDE), :] = gate * val
            for sl in (sg, sv):
                nbuf_ref[:, :, _slab(sl)] = ext_ref[sl, :, TILE_ROW0 + tt - hw:TILE_ROW0 + tt, :]
                ext_ref[sl, :, TILE_ROW0 - hw:TILE_ROW0, :] = ext_ref[sl, :, TILE_ROW0 + tt - hw:TILE_ROW0 + tt, :]
        a = jnp.concatenate([act_ref[c * per + half] for half in range(per)], axis=-1)
        acc = acc + jnp.dot(a.reshape(m, FFN_COLS).astype(BF16), wout_ref[c * FFN_COLS:(c + 1) * FFN_COLS, :],
                            preferred_element_type=F32)
    y_ref[...] = x + mod_ref[:, 5:6, :] * acc.reshape(nb, tt, D_MODEL)


def _ffn(x, mod, ng, win, cw, cb, wout, buf, nb, tt):
    b, t, d = x.shape
    n2 = 2 * D_FF
    kern = functools.partial(_ffn_kernel, nb=nb, tt=tt)
    return pl.pallas_call(
        kern,
        grid=(b // nb, t // tt),
        in_specs=[pl.BlockSpec((nb, tt, d), lambda i, j: (i, j, 0)),
                  pl.BlockSpec((nb, 8, d), lambda i, j: (i, 0, 0)),
                  pl.BlockSpec((1, d), lambda i, j: (0, 0)),
                  _wspec((d, n2)),
                  pl.BlockSpec((3, n2), lambda i, j: (0, 0)),
                  pl.BlockSpec((1, n2), lambda i, j: (0, 0)),
                  _wspec((D_FF, d)),
                  pl.BlockSpec((nb, 2, n2), lambda i, j: (i, 0, 0))],
        out_specs=[pl.BlockSpec((nb, tt, d), lambda i, j: (i, j, 0)),
                   pl.BlockSpec((nb, 2, n2), lambda i, j: (i, 0, 0))],
        out_shape=[jax.ShapeDtypeStruct((b, t, d), F32),
                   jax.ShapeDtypeStruct((b, 2, n2), F32)],
        scratch_shapes=[pltpu.VMEM((n2 // LANES, nb, TILE_ROW0 + tt, LANES), F32),
                        pltpu.VMEM((D_FF // LANES, nb, tt, LANES), F32)],
        compiler_params=_cparams("arbitrary", "arbitrary"),
        name="conv_ffn",
    )(x, mod, ng, win, cw, cb, wout, buf)


def _gdn_proj_kernel(x_ref, mod_ref, ng_ref, wqkv_ref, wz_ref, wab_ref, cw_ref, pv_ref, buf_ref,
                     q_ref, k_ref, v_ref, z_ref, gb_ref, nbuf_ref, ext_ref, *, nb, tt):
    m = nb * tt
    d = D_MODEL
    nslab = GDN_QKV // LANES
    hw = GDN_CONV_TAPS - 1
    slab = _slab

    @pl.when(pl.program_id(1) == 0)
    def _():
        for sl in range(nslab):
            ext_ref[sl, :, TILE_ROW0 - hw:TILE_ROW0, :] = buf_ref[:, :, slab(sl)]

    h = _norm_mod(x_ref[...], ng_ref[...], mod_ref[:, 1:2, :], mod_ref[:, 0:1, :])
    h = h.reshape(m, d).astype(BF16)
    nslice = GDN_QKV // MXU_N
    per = MXU_N // LANES

    def project(s):
        c0 = s * MXU_N
        u = jnp.dot(h, wqkv_ref[:, c0:c0 + MXU_N], preferred_element_type=F32).reshape(nb, tt, MXU_N)
        for half in range(per):
            ext_ref[s * per + half, :, TILE_ROW0:TILE_ROW0 + tt, :] = u[:, :, slab(half)]

    for s in range(min(GDN_AHEAD, nslice)):
        project(s)
    ab = jnp.dot(h, wab_ref[...], preferred_element_type=F32)
    lane = _iota2(ab.shape, 1)
    gb = jnp.where(lane < GDN_HEADS,
                   pv_ref[0:1, :] * _softplus(ab + pv_ref[1:2, :]),
                   _sigmoid(ab))
    gb_ref[...] = gb.reshape(nb, tt, LANES)
    outs = (q_ref, k_ref, v_ref)
    n4 = tt // CONV_STRIDE
    for sl in range(nslab):
        if sl % per == 0 and sl // per + GDN_AHEAD < nslice:
            project(sl // per + GDN_AHEAD)
        if sl == nslab - per:
            z_ref[...] = jnp.dot(h, wz_ref[...], preferred_element_type=F32).reshape(nb, tt, d)
        which, hd = divmod(sl, GDN_HEADS)
        for a in range(CONV_STRIDE):
            y = _silu(_conv_class(ext_ref, sl, a, cw_ref, GDN_CONV_TAPS, n4))
            if which < 2:
                r = lax.rsqrt(jnp.sum(y * y, axis=-1, keepdims=True) + NORM_EPS)
                if which == 0:
                    r = r * (GDN_DK ** -0.5)
                y = y * r
            outs[which][:, hd, pl.ds(a, n4, stride=CONV_STRIDE), :] = y
        nbuf_ref[:, :, slab(sl)] = ext_ref[sl, :, TILE_ROW0 + tt - hw:TILE_ROW0 + tt, :]
        ext_ref[sl, :, TILE_ROW0 - hw:TILE_ROW0, :] = ext_ref[sl, :, TILE_ROW0 + tt - hw:TILE_ROW0 + tt, :]


def _gdn_proj(x, mod, ng, wqkv, wz, wab, cw, pv, buf, nb, tt):
    b, t, d = x.shape
    kern = functools.partial(_gdn_proj_kernel, nb=nb, tt=tt)
    row = lambda i, j: (i, j, 0)
    const2 = lambda i, j: (0, 0)
    act = jax.ShapeDtypeStruct((b, t, d), F32)
    heads = jax.ShapeDtypeStruct((b, GDN_HEADS, t, GDN_DK), F32)
    return pl.pallas_call(
        kern,
        grid=(b // nb, t // tt),
        in_specs=[pl.BlockSpec((nb, tt, d), row),
                  pl.BlockSpec((nb, 8, d), lambda i, j: (i, 0, 0)),
                  pl.BlockSpec((1, d), const2),
                  _wspec((d, GDN_QKV)),
                  _wspec((d, d)),
                  pl.BlockSpec((d, LANES), const2),
                  pl.BlockSpec((4, GDN_QKV), const2),
                  pl.BlockSpec((8, LANES), const2),
                  pl.BlockSpec((nb, 3, GDN_QKV), lambda i, j: (i, 0, 0))],
        out_specs=[pl.BlockSpec((nb, GDN_HEADS, tt, GDN_DK), lambda i, j: (i, 0, j, 0))] * 3
                  + [pl.BlockSpec((nb, tt, d), row),
                     pl.BlockSpec((nb, tt, LANES), row),
                     pl.BlockSpec((nb, 3, GDN_QKV), lambda i, j: (i, 0, 0))],
        out_shape=[heads, heads, heads, act,
                   jax.ShapeDtypeStruct((b, t, LANES), F32),
                   jax.ShapeDtypeStruct((b, 3, GDN_QKV), F32)],
        scratch_shapes=[pltpu.VMEM((GDN_QKV // LANES, nb, TILE_ROW0 + tt, LANES), F32)],
        compiler_params=_cparams("arbitrary", "arbitrary"),
        name="gdn_proj",
    )(x, mod, ng, wqkv, wz, wab, cw, pv, buf)


def _tri_inverses(lmats, chunk):
    n = lmats[0].shape[0]
    row = _iota2((n, n), 0)
    col = _iota2((n, n), 1)
    level = 31 - lax.clz(jnp.bitwise_xor(row, col))
    eye = (row == col).astype(F32)
    xs = [eye - jnp.where(level == 0, lm, 0.0) for lm in lmats]
    k = 1
    while (1 << k) < chunk:
        xbs = [x.astype(BF16) for x in xs]
        ps = [_dot(xb, jnp.where(level == k, lm, 0.0)).astype(BF16) for xb, lm in zip(xbs, lmats)]
        xs = [x - _dot(p, xb) for x, p, xb in zip(xs, ps, xbs)]
        k += 1
    return xs


def _gdn_core_kernel(q_ref, k_ref, v_ref, z_ref, gb_ref, x_ref, mod_ref, og_ref, wout_ref, s0_ref,
                     y_ref, sout_ref, s_ref, o_ref, *, tt, chunk):
    @pl.when(pl.program_id(1) == 0)
    def _():
        s_ref[...] = s0_ref[0]

    nck = tt // chunk
    row = _iota2((tt, tt), 0)
    col = _iota2((tt, tt), 1)
    if nck > 1:
        same = (row // chunk) == (col // chunk)
        incl = (row >= col) & same
        strict = (row > col) & same
    else:
        same = None
        incl = row >= col
        strict = row > col
    gb = gb_ref[0]
    gcol = _sel_dot(incl.astype(BF16), gb)
    if nck > 1:
        glast = _sel_dot(same.astype(BF16), gb)
    else:
        glast = jnp.broadcast_to(jnp.sum(gb, axis=0, keepdims=True), gb.shape)
    grow = gcol.T
    eg = jnp.exp(gcol)
    ekd = jnp.exp(glast - gcol)
    egl = jnp.exp(glast)

    heads = range(GDN_HEADS)
    cols = [slice(hd * GDN_DK, (hd + 1) * GDN_DK) for hd in heads]
    kbs, lmats, qks = [], [], []
    for hd in heads:
        kh = k_ref[0, hd]
        diff = gcol[:, hd:hd + 1] - grow[hd:hd + 1, :]
        decay = jnp.exp(jnp.where(incl, diff, -jnp.inf))
        kb = kh * gb[:, GDN_HEADS + hd:GDN_HEADS + hd + 1]
        kbs.append(kb)
        lmats.append(jnp.where(strict, _dot_nt(kb, kh) * decay, 0.0))
        qks.append((_dot_nt(q_ref[0, hd], kh) * decay).astype(BF16))
    tinvs = [t.astype(BF16) for t in _tri_inverses(lmats, chunk)]
    us, ws = [], []
    for hd in heads:
        beta = gb[:, GDN_HEADS + hd:GDN_HEADS + hd + 1]
        uw = _dot(tinvs[hd], jnp.concatenate([v_ref[0, hd] * beta, kbs[hd] * eg[:, hd:hd + 1]], axis=1))
        us.append(uw[:, :GDN_DK])
        ws.append(uw[:, GDN_DK:].astype(BF16))
    ss = [s_ref[hd] for hd in heads]
    vnews = [[] for _ in heads]
    ocross = [[] for _ in heads]
    for c in range(nck):
        rows = slice(c * chunk, (c + 1) * chunk)
        for hd in heads:
            qd = (q_ref[0, hd, rows, :] * eg[rows, hd:hd + 1]).astype(BF16)
            ws_qs = _dot(jnp.concatenate([ws[hd][rows], qd], axis=0), ss[hd])
            vn = us[hd][rows] - ws_qs[:chunk]
            ocross[hd].append(ws_qs[chunk:])
            kd = k_ref[0, hd, rows, :] * ekd[rows, hd:hd + 1]
            ss[hd] = ss[hd] * egl[c * chunk:c * chunk + 1, hd:hd + 1] + _dot_tn(kd, vn)
            vnews[hd].append(vn.astype(BF16))
    for hd in heads:
        s_ref[hd] = ss[hd]
        vnew = jnp.concatenate(vnews[hd], axis=0) if nck > 1 else vnews[hd][0]
        oc = jnp.concatenate(ocross[hd], axis=0) if nck > 1 else ocross[hd][0]
        o = oc + _dot(qks[hd], vnew)
        o = o * lax.rsqrt(jnp.mean(o * o, axis=-1, keepdims=True) + NORM_EPS) * og_ref[...]
        o_ref[:, cols[hd]] = (o * _silu(z_ref[0, :, cols[hd]])).astype(BF16)

    out = jnp.dot(o_ref[...], wout_ref[...], preferred_element_type=F32)
    y_ref[0] = x_ref[0] + mod_ref[0, 2:3, :] * out
    sout_ref[0] = s_ref[...]


def _gdn_core(q, k, v, z, gb, x, mod, og, wout, s0, tt, chunk):
    b, t, d = x.shape
    kern = functools.partial(_gdn_core_kernel, tt=tt, chunk=chunk)
    row = lambda i, j: (i, j, 0)
    return pl.pallas_call(
        kern,
        grid=(b, t // tt),
        in_specs=[pl.BlockSpec((1, GDN_HEADS, tt, GDN_DK), lambda i, j: (i, 0, j, 0))] * 3
                 + [pl.BlockSpec((1, tt, d), row),
                    pl.BlockSpec((1, tt, LANES), row),
                    pl.BlockSpec((1, tt, d), row),
                    pl.BlockSpec((1, 8, d), lambda i, j: (i, 0, 0)),
                    pl.BlockSpec((1, GDN_DK), lambda i, j: (0, 0)),
                    _wspec((d, d)),
                    pl.BlockSpec((1, GDN_HEADS, GDN_DK, GDN_DK), lambda i, j: (i, 0, 0, 0))],
        out_specs=[pl.BlockSpec((1, tt, d), row),
                   pl.BlockSpec((1, GDN_HEADS, GDN_DK, GDN_DK), lambda i, j: (i, 0, 0, 0))],
        out_shape=[jax.ShapeDtypeStruct((b, t, d), F32),
                   jax.ShapeDtypeStruct((b, GDN_HEADS, GDN_DK, GDN_DK), F32)],
        scratch_shapes=[pltpu.VMEM((GDN_HEADS, GDN_DK, GDN_DK), F32),
                        pltpu.VMEM((tt, d), BF16)],
        compiler_params=_cparams("arbitrary", "arbitrary"),
        name="gdn_core",
    )(q, k, v, z, gb, x, mod, og, wout, s0)


def _sb_proj_kernel(x_ref, mod_ref, ng_ref, w_ref, qkg_ref,
                    kf_ref, vf_ref, qb_ref, kb_ref, vb_ref, u_ref, *, nb, tt):
    m = nb * tt
    d = D_MODEL
    h = _norm_mod(x_ref[...], ng_ref[...], mod_ref[:, 1:2, :], mod_ref[:, 0:1, :])
    h = h.reshape(m, d).astype(BF16)
    grp = ((_iota2((MXU_N, MXU_N), 0) // SB_DH) == (_iota2((MXU_N, MXU_N), 1) // SB_DH)).astype(BF16)
    for s in range(3 * d // MXU_N):
        c0 = s * MXU_N
        u_ref[:, c0:c0 + MXU_N] = jnp.dot(h, w_ref[:, c0:c0 + MXU_N], preferred_element_type=F32)
    for s in range(3 * d // MXU_N):
        c0 = s * MXU_N
        which, o0 = divmod(c0, d)
        u = u_ref[:, c0:c0 + MXU_N]
        if which < 2:
            ms = _dot_sel(u * u, grp) * (1.0 / SB_DH)
            u = u * lax.rsqrt(ms + NORM_EPS) * qkg_ref[which:which + 1, o0:o0 + MXU_N]
        u3 = u.reshape(nb, tt, MXU_N)
        if which == 0:
            qb_ref[:, :, o0:o0 + MXU_N] = u3.astype(BF16)
        elif which == 1:
            kf_ref[:, :, o0:o0 + MXU_N] = u3
            kb_ref[:, :, o0:o0 + MXU_N] = u3.astype(BF16)
        else:
            vf_ref[:, :, o0:o0 + MXU_N] = u3
            vb_ref[:, :, o0:o0 + MXU_N] = u3.astype(BF16)


def _sb_proj(x, mod, ng, w, qkg, nb, tt):
    b, t, d = x.shape
    kern = functools.partial(_sb_proj_kernel, nb=nb, tt=tt)
    row = lambda i, j: (i, j, 0)
    const2 = lambda i, j: (0, 0)
    f = jax.ShapeDtypeStruct((b, t, d), F32)
    h = jax.ShapeDtypeStruct((b, t, d), BF16)
    return pl.pallas_call(
        kern,
        grid=(b // nb, t // tt),
        in_specs=[pl.BlockSpec((nb, tt, d), row),
                  pl.BlockSpec((nb, 8, d), lambda i, j: (i, 0, 0)),
                  pl.BlockSpec((1, d), const2),
                  _wspec((d, 3 * d)),
                  pl.BlockSpec((8, d), const2)],
        out_specs=[pl.BlockSpec((nb, tt, d), row)] * 5,
        out_shape=[f, f, h, h, h],
        scratch_shapes=[pltpu.VMEM((nb * tt, 3 * d), F32)],
        compiler_params=_cparams("arbitrary", "arbitrary"),
        name="sb_proj",
    )(x, mod, ng, w, qkg)


def _suffix_selector(bw):
    r = jnp.bitwise_and(_iota2((2 * bw, bw + LANES), 0), bw - 1)
    c = _iota2((2 * bw, bw + LANES), 1)
    return jnp.where(c < bw, (r > c).astype(F32), 1.0).astype(BF16)


def _dot_nt_bf(a, b):
    return lax.dot_general(a, b, (((1,), (1,)), ((), ())), preferred_element_type=F32)


def _sb_tile(qh, kt, vt, carry, acc, mask, sel):
    n = kt.shape[0]
    bw = min(LANES, n)
    z = _dot_nt_bf(qh, kt)
    sp = jnp.maximum(z, 0.0) + jnp.log(1.0 + jnp.exp(-jnp.abs(z)))
    spm = sp if mask is None else jnp.where(mask, sp, 0.0)
    ws = []
    for mb in reversed(range(n // bw)):
        sl = slice(mb * bw, (mb + 1) * bw)
        hi = spm[:, sl].astype(BF16)
        lo = (spm[:, sl] - hi.astype(F32)).astype(BF16)
        r = jnp.dot(jnp.concatenate([hi, lo], axis=1), sel[bw], preferred_element_type=F32)
        w = jnp.exp(z[:, sl] - sp[:, sl] - carry[:, :bw] - r[:, :bw])
        if mask is not None:
            w = jnp.where(mask[:, sl], w, 0.0)
        ws.append(w.astype(BF16))
        carry = carry + r[:, bw:]
    wts = jnp.concatenate(ws[::-1], axis=-1) if len(ws) > 1 else ws[0]
    acc = acc + jnp.dot(wts, vt, preferred_element_type=F32)
    return carry, acc


def _sb_head_queries(q):
    q = q * (SB_DH ** -0.5)
    lane = _iota2(q.shape, 1)
    return [jnp.where((lane // SB_DH) == hd, q, jnp.zeros_like(q)) for hd in range(2)]


def _sb_fast_kernel(q_ref, kd_ref, vd_ref, kp_ref, vp_ref, o_ref, need_ref, st_ref, *, tq, more_before_prev):
    tp = kp_ref.shape[1]
    npair = D_MODEL // LANES
    sel = {bw: _suffix_selector(bw) for bw in {min(LANES, tq), min(LANES, tp)}}
    dmask = _iota2((tq, tq), 1) < _iota2((tq, tq), 0)
    zero = jnp.zeros((tq, LANES), F32)
    pairs = [slice(hp * LANES, (hp + 1) * LANES) for hp in range(npair)]

    for hp in range(npair):
        qhs = _sb_head_queries(q_ref[0, :, pairs[hp]])
        for hd in range(2):
            c, a = _sb_tile(qhs[hd], kd_ref[0, :, pairs[hp]], vd_ref[0, :, pairs[hp]], zero, zero, dmask, sel)
            st_ref[hp, 2 * hd] = c
            st_ref[hp, 2 * hd + 1] = a

    for hp in range(npair):
        qhs = _sb_head_queries(q_ref[0, :, pairs[hp]])
        for hd in range(2):
            c, a = _sb_tile(qhs[hd], kp_ref[0, :, pairs[hp]], vp_ref[0, :, pairs[hp]],
                            st_ref[hp, 2 * hd], st_ref[hp, 2 * hd + 1], None, sel)
            st_ref[hp, 2 * hd] = c
            st_ref[hp, 2 * hd + 1] = a

    lane = _iota2((tq, LANES), 1)
    cmin = None
    for hp in range(npair):
        o_ref[0, :, pairs[hp]] = jnp.where((lane // SB_DH) == 0, st_ref[hp, 1], st_ref[hp, 3]).astype(BF16)
        c = jnp.minimum(st_ref[hp, 0], st_ref[hp, 2])
        cmin = c if cmin is None else jnp.minimum(cmin, c)
    alive = jnp.where(jnp.min(cmin, axis=0, keepdims=True) < SB_DEAD, 1.0, 0.0)
    if not more_before_prev:
        alive = jnp.zeros_like(alive)
    need_ref[0, 0] = jnp.broadcast_to(alive, (8, LANES))


def _sb_fast(qb, kb, vb, prev_k, prev_v, tq, more_before_prev):
    b, t, d = qb.shape
    tp = prev_k.shape[1]
    kern = functools.partial(_sb_fast_kernel, tq=tq, more_before_prev=more_before_prev)
    cur = pl.BlockSpec((1, tq, d), lambda i, j: (i, j, 0))
    prev = pl.BlockSpec((1, tp, d), lambda i, j: (i, 0, 0))
    return pl.pallas_call(
        kern,
        grid=(b, t // tq),
        in_specs=[cur, cur, cur, prev, prev],
        out_specs=[cur, pl.BlockSpec((1, 1, 8, LANES), lambda i, j: (i, j, 0, 0))],
        out_shape=[jax.ShapeDtypeStruct((b, t, d), BF16),
                   jax.ShapeDtypeStruct((b, t // tq, 8, LANES), F32)],
        scratch_shapes=[pltpu.VMEM((d // LANES, 4, tq, LANES), F32)],
        compiler_params=_cparams("arbitrary", "arbitrary"),
        name="sb_fast",
    )(qb, kb, vb, prev_k, prev_v)


def _sb_attention(qb, kb, vb, past_k, past_v, tq):
    b, t, d = qb.shape
    if past_k is None:
        return _sb_attn(qb, kb, vb, None, None, tq)
    assert t == tq
    plen = past_k.shape[1]
    tp = min(256, plen)
    tail = lambda c: c[:, plen - tp:].reshape(b, tp, d).astype(BF16)
    o, need = _sb_fast(qb, kb, vb, tail(past_k), tail(past_v), tq, plen > tp)
    full = lambda: _sb_attn(qb, kb, vb, past_k.reshape(b, plen, d), past_v.reshape(b, plen, d), tq)
    return lax.cond(jnp.max(need) > 0.0, full, lambda: o)


def _sb_attn_kernel(*refs, tq, tk, past_len):
    if past_len:
        q_ref, k_ref, v_ref, pk_ref, pv_ref, o_ref = refs
    else:
        q_ref, k_ref, v_ref, o_ref = refs
        pk_ref = pv_ref = None
    qi = pl.program_id(2)
    lane = _iota2((tq, LANES), 1)
    pairs = [slice(p * LANES, (p + 1) * LANES) for p in range(SB_PAIRS)]
    qhs = [qh for p in range(SB_PAIRS) for qh in _sb_head_queries(q_ref[0, :, pairs[p]])]
    ts = tq
    sel = {bw: _suffix_selector(bw) for bw in {min(LANES, ts), min(LANES, tk)}}

    def both(kt, vt, state, mask):
        new = ()
        for hd in range(2 * SB_PAIRS):
            new += _sb_tile(qhs[hd], kt[:, pairs[hd // 2]], vt[:, pairs[hd // 2]],
                            state[2 * hd], state[2 * hd + 1], mask, sel)
        return new

    r0 = pl.multiple_of(qi * tq, tq)
    zero = jnp.zeros((tq, LANES), F32)
    dmask = _iota2((tq, tq), 1) < _iota2((tq, tq), 0)
    state = both(k_ref[0, pl.ds(r0, tq), :], v_ref[0, pl.ds(r0, tq), :], (zero,) * (4 * SB_PAIRS), dmask)

    def alive(st):
        cmin = st[0]
        for hd in range(1, 2 * SB_PAIRS):
            cmin = jnp.minimum(cmin, st[2 * hd])
        return jnp.min(cmin) < SB_DEAD

    def key_loop(ntiles, size, load, st):
        def cond(c):
            return jnp.logical_and(c[0] < ntiles, c[1])

        def body(c):
            kt, vt = load(pl.multiple_of((ntiles - 1 - c[0]) * size, size))
            nst = both(kt, vt, c[2:], None)
            return (c[0] + 1, alive(nst)) + nst

        return lax.while_loop(cond, body, (jnp.int32(0), alive(st)) + st)[2:]

    state = key_loop(qi * (tq // ts), ts,
                     lambda k0: (k_ref[0, pl.ds(k0, ts), :], v_ref[0, pl.ds(k0, ts), :]), state)
    if past_len:
        state = key_loop(past_len // tk, tk,
                         lambda k0: (pk_ref[0, pl.ds(k0, tk), :].astype(BF16),
                                     pv_ref[0, pl.ds(k0, tk), :].astype(BF16)), state)
    for p in range(SB_PAIRS):
        o_ref[0, :, pairs[p]] = jnp.where((lane // SB_DH) == 0, state[4 * p + 1], state[4 * p + 3]).astype(BF16)


def _sb_attn(qb, kb, vb, past_k, past_v, tq):
    b, t, d = qb.shape
    past_len = 0 if past_k is None else past_k.shape[1]
    tk = tq if not past_len else min(256, past_len)
    kern = functools.partial(_sb_attn_kernel, tq=tq, tk=tk, past_len=past_len)
    width = SB_PAIRS * LANES
    qspec = pl.BlockSpec((1, tq, width), lambda i, h, j: (i, j, h))
    kvspec = pl.BlockSpec((1, t, width), lambda i, h, j: (i, 0, h), pipeline_mode=pl.Buffered(1))
    in_specs = [qspec, kvspec, kvspec]
    args = [qb, kb, vb]
    if past_len:
        pspec = pl.BlockSpec((1, past_len, width), lambda i, h, j: (i, 0, h))
        in_specs += [pspec, pspec]
        args += [past_k, past_v]
    return pl.pallas_call(
        kern,
        grid=(b, d // width, t // tq),
        in_specs=in_specs,
        out_specs=qspec,
        out_shape=jax.ShapeDtypeStruct((b, t, d), BF16),
        compiler_params=_cparams("arbitrary", "arbitrary", "arbitrary"),
        name="sb_attn",
    )(*args)


def _out_proj_kernel(o_ref, x_ref, mod_ref, w_ref, y_ref, *, nb, tt):
    out = jnp.dot(o_ref[...].reshape(nb * tt, -1), w_ref[...], preferred_element_type=F32)
    y_ref[...] = x_ref[...] + mod_ref[:, 2:3, :] * out.reshape(nb, tt, D_MODEL)


def _out_proj(o, x, mod, w, nb, tt):
    b, t, d = x.shape
    kin = o.shape[-1]
    kern = functools.partial(_out_proj_kernel, nb=nb, tt=tt)
    row = lambda i, j: (i, j, 0)
    return pl.pallas_call(
        kern,
        grid=(b // nb, t // tt),
        in_specs=[pl.BlockSpec((nb, tt, kin), row),
                  pl.BlockSpec((nb, tt, d), row),
                  pl.BlockSpec((nb, 8, d), lambda i, j: (i, 0, 0)),
                  _wspec((kin, d))],
        out_specs=pl.BlockSpec((nb, tt, d), row),
        out_shape=jax.ShapeDtypeStruct((b, t, d), F32),
        compiler_params=_cparams("arbitrary", "arbitrary"),
        name="out_proj",
    )(o, x, mod, w)


def _ret_log_gamma(hd):
    return math.log1p(-(2.0 ** (-5.0 - hd)))


def _ret_proj_kernel(x_ref, mod_ref, ng_ref, w_ref, cos_ref, sin_ref,
                     q_ref, qd_ref, k_ref, kd_ref, v_ref, gt_ref, *, nb, tt, chunk):
    m = nb * tt
    d = D_MODEL
    hk = RET_HEADS * RET_DK
    hv = RET_HEADS * RET_DV
    h = _norm_mod(x_ref[...], ng_ref[...], mod_ref[:, 1:2, :], mod_ref[:, 0:1, :])
    h = h.reshape(m, d).astype(BF16)
    cos = cos_ref[...][None]
    sin = sin_ref[...][None]
    tpos = pl.program_id(1) * tt + _iota2((1, tt, 1), 1)
    assert chunk & (chunk - 1) == 0
    idx = jnp.bitwise_and(tpos, chunk - 1).astype(F32)
    half = RET_DK // 2
    for s in range(2 * hk // MXU_N):
        c0 = s * MXU_N
        which, o0 = divmod(c0, hk)
        hd = o0 // RET_DK
        lg = _ret_log_gamma(hd)
        u = jnp.dot(h, w_ref[:, c0:c0 + MXU_N], preferred_element_type=F32).reshape(nb, tt, MXU_N)
        x1 = u[:, :, :half]
        x2 = u[:, :, half:]
        r = jnp.concatenate([x1 * cos - x2 * sin, x1 * sin + x2 * cos], axis=-1)
        if which == 0:
            q_ref[:, :, o0:o0 + MXU_N] = r.astype(BF16)
            qd_ref[:, :, o0:o0 + MXU_N] = (r * jnp.exp(lg * (idx + 1.0))).astype(BF16)
        else:
            r = r * (RET_DK ** -0.5)
            k_ref[:, :, o0:o0 + MXU_N] = r.astype(BF16)
            kd_ref[:, :, o0:o0 + MXU_N] = (r * jnp.exp(lg * (chunk - 1.0 - idx))).astype(BF16)
    for s in range(2 * hv // MXU_N):
        c0 = 2 * hk + s * MXU_N
        u = jnp.dot(h, w_ref[:, c0:c0 + MXU_N], preferred_element_type=F32).reshape(nb, tt, MXU_N)
        o0 = s * MXU_N
        if o0 < hv:
            v_ref[:, :, o0:o0 + MXU_N] = u.astype(BF16)
        else:
            gt_ref[:, :, o0 - hv:o0 - hv + MXU_N] = u


def _ret_proj(x, mod, ng, w, cos, sin, nb, tt, chunk):
    b, t, d = x.shape
    hk = RET_HEADS * RET_DK
    hv = RET_HEADS * RET_DV
    kern = functools.partial(_ret_proj_kernel, nb=nb, tt=tt, chunk=chunk)
    row = lambda i, j: (i, j, 0)
    const2 = lambda i, j: (0, 0)
    qs = jax.ShapeDtypeStruct((b, t, hk), BF16)
    return pl.pallas_call(
        kern,
        grid=(b // nb, t // tt),
        in_specs=[pl.BlockSpec((nb, tt, d), row),
                  pl.BlockSpec((nb, 8, d), lambda i, j: (i, 0, 0)),
                  pl.BlockSpec((1, d), const2),
                  _wspec((d, 2 * hk + 2 * hv)),
                  pl.BlockSpec((tt, RET_DK // 2), lambda i, j: (j, 0)),
                  pl.BlockSpec((tt, RET_DK // 2), lambda i, j: (j, 0))],
        out_specs=[pl.BlockSpec((nb, tt, hk), row)] * 4
                  + [pl.BlockSpec((nb, tt, hv), row)] * 2,
        out_shape=[qs, qs, qs, qs,
                   jax.ShapeDtypeStruct((b, t, hv), BF16),
                   jax.ShapeDtypeStruct((b, t, hv), F32)],
        compiler_params=_cparams("arbitrary", "arbitrary"),
        name="ret_proj",
    )(x, mod, ng, w, cos, sin)


def _ret_core_kernel(q_ref, qd_ref, k_ref, kd_ref, v_ref, gt_ref, x_ref, mod_ref, og_ref, wout_ref, r0_ref,
                     y_ref, rout_ref, r_ref, o_ref, *, chunk):
    @pl.when(pl.program_id(1) == 0)
    def _():
        r_ref[...] = r0_ref[0]

    rel = (_iota2((chunk, chunk), 0) - _iota2((chunk, chunk), 1)).astype(F32)
    for hd in range(RET_HEADS):
        lg = _ret_log_gamma(hd)
        k0 = hd * RET_DK
        v0 = hd * RET_DV
        dmask = jnp.where(rel >= 0, jnp.exp(lg * jnp.maximum(rel, 0.0)), 0.0)
        vh = v_ref[0, :, v0:v0 + RET_DV]
        s = _dot_nt_bf(q_ref[0, :, k0:k0 + RET_DK], k_ref[0, :, k0:k0 + RET_DK]) * dmask
        r = r_ref[hd]
        o = (jnp.dot(s.astype(BF16), vh, preferred_element_type=F32)
             + jnp.dot(qd_ref[0, :, k0:k0 + RET_DK], r.astype(BF16), preferred_element_type=F32))
        r_ref[hd] = r * math.exp(lg * chunk) + lax.dot_general(
            kd_ref[0, :, k0:k0 + RET_DK], vh, (((0,), (0,)), ((), ())), preferred_element_type=F32)
        mu = jnp.mean(o, axis=-1, keepdims=True)
        dlt = o - mu
        var = jnp.mean(dlt * dlt, axis=-1, keepdims=True)
        o = dlt * lax.rsqrt(var + NORM_EPS) * og_ref[:, v0:v0 + RET_DV]
        o_ref[:, v0:v0 + RET_DV] = (o * _silu(gt_ref[0, :, v0:v0 + RET_DV])).astype(BF16)
    out = jnp.dot(o_ref[...], wout_ref[...], preferred_element_type=F32)
    y_ref[0] = x_ref[0] + mod_ref[0, 2:3, :] * out
    rout_ref[0] = r_ref[...]


def _ret_core(q, qd, k, kd, v, gt, x, mod, og, wout, r0, chunk):
    b, t, d = x.shape
    hk = RET_HEADS * RET_DK
    hv = RET_HEADS * RET_DV
    kern = functools.partial(_ret_core_kernel, chunk=chunk)
    row = lambda i, j: (i, j, 0)
    st = pl.BlockSpec((1, RET_HEADS, RET_DK, RET_DV), lambda i, j: (i, 0, 0, 0))
    return pl.pallas_call(
        kern,
        grid=(b, t // chunk),
        in_specs=[pl.BlockSpec((1, chunk, hk), row)] * 4
                 + [pl.BlockSpec((1, chunk, hv), row)] * 2
                 + [pl.BlockSpec((1, chunk, d), row),
                    pl.BlockSpec((1, 8, d), lambda i, j: (i, 0, 0)),
                    pl.BlockSpec((1, hv), lambda i, j: (0, 0)),
                    _wspec((hv, d)),
                    st],
        out_specs=[pl.BlockSpec((1, chunk, d), row), st],
        out_shape=[jax.ShapeDtypeStruct((b, t, d), F32),
                   jax.ShapeDtypeStruct((b, RET_HEADS, RET_DK, RET_DV), F32)],
        scratch_shapes=[pltpu.VMEM((RET_HEADS, RET_DK, RET_DV), F32),
                        pltpu.VMEM((chunk, hv), BF16)],
        compiler_params=_cparams("arbitrary", "arbitrary"),
        name="ret_core",
    )(q, qd, k, kd, v, gt, x, mod, og, wout, r0)


def _run_group(x, mod_all, states, ffn_bufs, pos0, wts):
    b, t, d = x.shape
    if t >= ROW_TILE:
        nb, tt = 1, ROW_TILE
    else:
        nb, tt = b, t
    new_states, new_ffn = [], []
    for i in range(DEPTH):
        kind, j = i % 3, i // 3
        mod = mod_all[i]
        if kind == 0:
            s0, cbuf = states[i]
            gw = wts["gdn"][j]
            q, k, v, z, gb, ncbuf = _gdn_proj(x, mod, wts["norm_mix_g"][i], gw["wqkv"], gw["wz"], gw["wab"],
                                              gw["conv_w"], gw["pv"], cbuf, nb, tt)
            x, s_new = _gdn_core(q, k, v, z, gb, x, mod, gw["norm_g"], gw["wout"], s0,
                                 min(GDN_TILE, t), min(CHUNK, t))
            new_states.append((s_new, ncbuf))
        elif kind == 1:
            pk, pv = states[i]
            sw = wts["sb"][j]
            kf, vf, qb, kb, vb = _sb_proj(x, mod, wts["norm_mix_g"][i], sw["win"], sw["qkg"], nb, tt)
            o = _sb_attention(qb, kb, vb, pk, pv, min(SB_TILE, t))
            x = _out_proj(o, x, mod, sw["wout"], nb, tt)
            new_states.append((kf.reshape(b, t, SB_HEADS, SB_DH), vf.reshape(b, t, SB_HEADS, SB_DH)))
        else:
            (r0,) = states[i]
            rw = wts["ret"][j]
            chunk = min(RET_CHUNK, t)
            half = RET_DK // 2
            inv_freq = RET_ROPE_BASE ** (-jnp.arange(half, dtype=F32) / half)
            ang = (pos0 + jnp.arange(t)).astype(F32)[:, None] * inv_freq[None, :]
            q, qd, k, kd, v, gt = _ret_proj(x, mod, wts["norm_mix_g"][i], rw["win"], jnp.cos(ang), jnp.sin(ang),
                                            nb, tt, chunk)
            x, r_new = _ret_core(q, qd, k, kd, v, gt, x, mod, rw["norm_g"], rw["wout"], r0, chunk)
            new_states.append((r_new,))
        fw = wts["ffn"][i]
        x, fbuf = _ffn(x, mod, wts["norm_ffn_g"][i], fw["win"], fw["conv_w"], fw["conv_b"], fw["wout"],
                       ffn_bufs[i], nb, tt)
        new_ffn.append(fbuf)
    return x, new_states, jnp.stack(new_ffn)


def kernel(x_prompt, x_sample, state_l0_gdn_S, state_l0_gdn_conv, cache_l1_sb_k, cache_l1_sb_v, state_l2_ret, state_l3_gdn_S, state_l3_gdn_conv, state_ffn_conv, c_prompt, c_sample, ada_w, ada_b, norm_mix_g, norm_ffn_g, gdn_w_in, gdn_conv_w, gdn_A_log, gdn_dt_bias, gdn_norm_g, gdn_w_out, sb_w_in, sb_q_norm_g, sb_k_norm_g, sb_w_out, ret_w_in, ret_norm_g, ret_w_out, ffn_w_in, ffn_conv_w, ffn_conv_b, ffn_w_out):
    d = D_MODEL
    bp, tp, _ = x_prompt.shape
    bs, ts, _ = x_sample.shape

    hv = GDN_HEADS * GDN_DK
    wts = {"norm_mix_g": [norm_mix_g[i].reshape(1, d) for i in range(DEPTH)],
           "norm_ffn_g": [norm_ffn_g[i].reshape(1, d) for i in range(DEPTH)],
           "gdn": [], "sb": [], "ret": [], "ffn": []}
    for j in range(gdn_w_in.shape[0]):
        w = gdn_w_in[j]
        wab = jnp.pad(w[:, GDN_QKV + hv:], ((0, 0), (0, LANES - 2 * GDN_HEADS)))
        pv = jnp.zeros((8, LANES), F32)
        pv = pv.at[0, :GDN_HEADS].set(-jnp.exp(gdn_A_log[j].astype(F32)))
        pv = pv.at[1, :GDN_HEADS].set(gdn_dt_bias[j].astype(F32))
        wts["gdn"].append({"wqkv": w[:, :GDN_QKV].astype(BF16),
                           "wz": w[:, GDN_QKV:GDN_QKV + hv].astype(BF16),
                           "wab": wab.astype(BF16),
                           "conv_w": gdn_conv_w[j], "pv": pv,
                           "norm_g": gdn_norm_g[j].reshape(1, GDN_DK),
                           "wout": gdn_w_out[j].astype(BF16)})
    for j in range(sb_w_in.shape[0]):
        qkg = jnp.zeros((8, d), F32)
        qkg = qkg.at[0].set(jnp.tile(sb_q_norm_g[j], SB_HEADS)).at[1].set(jnp.tile(sb_k_norm_g[j], SB_HEADS))
        wts["sb"].append({"win": sb_w_in[j].astype(BF16), "qkg": qkg, "wout": sb_w_out[j].astype(BF16)})
    for j in range(ret_w_in.shape[0]):
        wts["ret"].append({"win": ret_w_in[j].astype(BF16),
                           "norm_g": ret_norm_g[j].reshape(1, RET_HEADS * RET_DV),
                           "wout": ret_w_out[j].astype(BF16)})
    for i in range(DEPTH):
        wts["ffn"].append({"win": ffn_w_in[i].astype(BF16), "conv_w": ffn_conv_w[i],
                           "conv_b": ffn_conv_b[i].reshape(1, 2 * D_FF), "wout": ffn_w_out[i].astype(BF16)})

    nrow = bp + bs
    rows = -(-nrow // 8) * 8
    c_all = jnp.pad(jnp.concatenate([c_prompt, c_sample], axis=0), ((0, rows - nrow), (0, 0)))
    mod = _ada_mod(c_all, ada_w, ada_b)
    mod = jnp.pad(mod.reshape(DEPTH, rows, 6, d), ((0, 0), (0, 0), (0, 2), (0, 0)))
    mod_p, mod_s = mod[:, :bp], mod[:, bp:nrow]

    dt = x_prompt.dtype
    zero_states = [(jnp.zeros((bp, GDN_HEADS, GDN_DK, GDN_DK), dt), jnp.zeros((bp, 3, GDN_QKV), dt)),
                   (None, None),
                   (jnp.zeros((bp, RET_HEADS, RET_DK, RET_DV), dt),),
                   (jnp.zeros((bp, GDN_HEADS, GDN_DK, GDN_DK), dt), jnp.zeros((bp, 3, GDN_QKV), dt))]
    zero_ffn = jnp.zeros((DEPTH, bp, 2, 2 * D_FF), dt)
    y_prompt, p_states, p_ffn_conv = _run_group(x_prompt, mod_p, zero_states, zero_ffn, 0, wts)

    past_len = cache_l1_sb_k.shape[1]
    sample_states = [(state_l0_gdn_S, state_l0_gdn_conv),
                     (cache_l1_sb_k, cache_l1_sb_v),
                     (state_l2_ret,), (state_l3_gdn_S, state_l3_gdn_conv)]
    y_sample, s_states, s_ffn_conv = _run_group(x_sample, mod_s, sample_states, state_ffn_conv, past_len, wts)

    (p_l0_S, p_l0_conv), (p_l1_k, p_l1_v), (p_l2_R,), (p_l3_S, p_l3_conv) = p_states
    (s_l0_S, s_l0_conv), (s_l1_k, s_l1_v), (s_l2_R,), (s_l3_S, s_l3_conv) = s_states
    return (y_prompt, y_sample,
            p_l0_S, p_l0_conv, p_l1_k, p_l1_v, p_l2_R, p_l3_S, p_l3_conv, p_ffn_conv,
            s_l0_S, s_l0_conv, s_l1_k, s_l1_v, s_l2_R, s_l3_S, s_l3_conv, s_ffn_conv)
```

```python
import functools
import math

import jax
import jax.numpy as jnp
from jax import lax
from jax.experimental import pallas as pl
from jax.experimental.pallas import tpu as pltpu

F32 = jnp.float32
BF16 = jnp.bfloat16

D_MODEL = 1024
DEPTH = 4
CHUNK = 64
GDN_HEADS = 8
GDN_DK = 128
GDN_QKV = 3 * GDN_HEADS * GDN_DK
GDN_CONV_TAPS = 4
GDN_AHEAD = 12
CONV_STRIDE = 4
SB_HEADS = 16
SB_DH = 64
RET_HEADS = 4
RET_DK = 256
RET_DV = 512
RET_ROPE_BASE = 10000.0
D_FF = 2816
NORM_EPS = 1e-6

LANES = 128
TILE_ROW0 = 8
MXU_N = 256
VMEM_LIMIT = 56 * 1024 * 1024

ROW_TILE = 512
GDN_TILE = 256
RET_CHUNK = 256
SB_TILE = 256
SB_DEAD = 105.0
SB_PAIRS = 4


def _cparams(*sem):
    return pltpu.CompilerParams(dimension_semantics=sem, vmem_limit_bytes=VMEM_LIMIT)


def _wspec(shape):
    return pl.BlockSpec(shape, lambda i, j: (0,) * len(shape), pipeline_mode=pl.Buffered(1))


def _dot(a, b):
    return jnp.dot(a.astype(BF16), b.astype(BF16), preferred_element_type=F32)


def _dot_nt(a, b):
    return lax.dot_general(a.astype(BF16), b.astype(BF16), (((1,), (1,)), ((), ())),
                           preferred_element_type=F32)


def _dot_tn(a, b):
    return lax.dot_general(a.astype(BF16), b.astype(BF16), (((0,), (0,)), ((), ())),
                           preferred_element_type=F32)


def _split(x):
    hi = x.astype(BF16)
    lo = (x - hi.astype(F32)).astype(BF16)
    return hi, lo


def _dot_sel(a, sel):
    hi, lo = _split(a)
    return (jnp.dot(hi, sel, preferred_element_type=F32)
            + jnp.dot(lo, sel, preferred_element_type=F32))


def _sel_dot(sel, b):
    hi, lo = _split(b)
    return (jnp.dot(sel, hi, preferred_element_type=F32)
            + jnp.dot(sel, lo, preferred_element_type=F32))


def _dot3(a, b):
    ah, al = _split(a)
    bh, bl = _split(b)
    return (jnp.dot(ah, bh, preferred_element_type=F32)
            + jnp.dot(ah, bl, preferred_element_type=F32)
            + jnp.dot(al, bh, preferred_element_type=F32))


def _sigmoid(x):
    return 1.0 / (1.0 + jnp.exp(-x))


def _silu(x):
    return x * _sigmoid(x)


def _softplus(x):
    return jnp.maximum(x, 0.0) + jnp.log(1.0 + jnp.exp(-jnp.abs(x)))


def _norm_mod(x, gain, scale, shift):
    ms = jnp.mean(x * x, axis=-1, keepdims=True)
    y = x * lax.rsqrt(ms + NORM_EPS) * gain
    return y * (1.0 + scale) + shift


def _iota2(shape, dim):
    return lax.broadcasted_iota(jnp.int32, shape, dim)


def _ada_kernel(c_ref, w_ref, b_ref, o_ref):
    o_ref[0] = _dot(_silu(c_ref[...]), w_ref[0]) + b_ref[0]


def _ada_mod(c_all, ada_w, ada_b):
    rows = c_all.shape[0]
    n = ada_w.shape[2]
    tn = 1536
    return pl.pallas_call(
        _ada_kernel,
        grid=(DEPTH, n // tn),
        in_specs=[pl.BlockSpec((rows, D_MODEL), lambda i, j: (0, 0)),
                  pl.BlockSpec((1, D_MODEL, tn), lambda i, j: (i, 0, j)),
                  pl.BlockSpec((1, 1, tn), lambda i, j: (i, 0, j))],
        out_specs=pl.BlockSpec((1, rows, tn), lambda i, j: (i, 0, j)),
        out_shape=jax.ShapeDtypeStruct((DEPTH, rows, n), F32),
        compiler_params=_cparams("arbitrary", "arbitrary"),
        name="ada_mod",
    )(c_all, ada_w, ada_b.reshape(DEPTH, 1, n))


def _slab(sl):
    return slice(sl * LANES, (sl + 1) * LANES)


def _conv_class(ext_ref, sl, a, w_ref, width, n4):
    hw = width - 1
    y = ext_ref[sl, :, pl.ds(TILE_ROW0 + a, n4, stride=CONV_STRIDE), :] * w_ref[hw:hw + 1, _slab(sl)]
    for j in range(hw):
        y = y + ext_ref[sl, :, pl.ds(TILE_ROW0 + a - hw + j, n4, stride=CONV_STRIDE), :] * w_ref[j:j + 1, _slab(sl)]
    return y


FFN_COLS = 256
FFN_CONV_TAPS = 3


def _ffn_kernel(x_ref, mod_ref, ng_ref, win_ref, cw_ref, cb_ref, wout_ref, buf_ref,
                y_ref, nbuf_ref, ext_ref, *, nb, tt):
    m = nb * tt
    hw = FFN_CONV_TAPS - 1
    nchunk = D_FF // FFN_COLS
    tile = slice(TILE_ROW0, TILE_ROW0 + tt)
    carried = slice(TILE_ROW0 - hw, TILE_ROW0)

    @pl.when(pl.program_id(1) == 0)
    def _():
        ext_ref[:, carried, :] = buf_ref[...]

    def conv(cols):
        y = ext_ref[:, tile, cols] * cw_ref[hw:hw + 1, cols]
        for j in range(hw):
            y = y + ext_ref[:, TILE_ROW0 - hw + j:TILE_ROW0 - hw + j + tt, cols] * cw_ref[j:j + 1, cols]
        return y + cb_ref[:, cols]

    x = x_ref[...]
    h = _norm_mod(x, ng_ref[...], mod_ref[:, 4:5, :], mod_ref[:, 3:4, :])
    h = h.reshape(m, D_MODEL).astype(BF16)
    acc = jnp.zeros((m, D_MODEL), F32)

    for c in range(nchunk):
        for col0 in (c * FFN_COLS, D_FF + c * FFN_COLS):
            u = jnp.dot(h, win_ref[:, col0:col0 + FFN_COLS], preferred_element_type=F32)
            ext_ref[:, tile, col0:col0 + FFN_COLS] = u.reshape(nb, tt, FFN_COLS)
    for c in range(nchunk):
        gcols = slice(c * FFN_COLS, (c + 1) * FFN_COLS)
        vcols = slice(D_FF + c * FFN_COLS, D_FF + (c + 1) * FFN_COLS)
        a = (_silu(conv(gcols)) * conv(vcols)).reshape(m, FFN_COLS).astype(BF16)
        acc = acc + jnp.dot(a, wout_ref[gcols, :], preferred_element_type=F32)
    y_ref[...] = x + mod_ref[:, 5:6, :] * acc.reshape(nb, tt, D_MODEL)
    nbuf_ref[...] = ext_ref[:, TILE_ROW0 + tt - hw:TILE_ROW0 + tt, :]
    ext_ref[:, carried, :] = ext_ref[:, TILE_ROW0 + tt - hw:TILE_ROW0 + tt, :]


def _ffn(x, mod, ng, win, cw, cb, wout, buf, nb, tt):
    b, t, d = x.shape
    n2 = 2 * D_FF
    kern = functools.partial(_ffn_kernel, nb=nb, tt=tt)
    return pl.pallas_call(
        kern,
        grid=(b // nb, t // tt),
        in_specs=[pl.BlockSpec((nb, tt, d), lambda i, j: (i, j, 0)),
                  pl.BlockSpec((nb, 8, d), lambda i, j: (i, 0, 0)),
                  pl.BlockSpec((1, d), lambda i, j: (0, 0)),
                  _wspec((d, n2)),
                  pl.BlockSpec((3, n2), lambda i, j: (0, 0)),
                  pl.BlockSpec((1, n2), lambda i, j: (0, 0)),
                  _wspec((D_FF, d)),
                  pl.BlockSpec((nb, 2, n2), lambda i, j: (i, 0, 0))],
        out_specs=[pl.BlockSpec((nb, tt, d), lambda i, j: (i, j, 0)),
                   pl.BlockSpec((nb, 2, n2), lambda i, j: (i, 0, 0))],
        out_shape=[jax.ShapeDtypeStruct((b, t, d), F32),
                   jax.ShapeDtypeStruct((b, 2, n2), F32)],
        scratch_shapes=[pltpu.VMEM((nb, TILE_ROW0 + tt, n2), F32)],
        compiler_params=_cparams("arbitrary", "arbitrary"),
        name="conv_ffn",
    )(x, mod, ng, win, cw, cb, wout, buf)


def _gdn_proj_kernel(x_ref, mod_ref, ng_ref, wqkv_ref, wz_ref, wab_ref, cw_ref, pv_ref, buf_ref,
                     q_ref, k_ref, v_ref, z_ref, gb_ref, nbuf_ref, ext_ref, *, nb, tt):
    m = nb * tt
    d = D_MODEL
    nslab = GDN_QKV // LANES
    hw = GDN_CONV_TAPS - 1
    slab = _slab

    @pl.when(pl.program_id(1) == 0)
    def _():
        for sl in range(nslab):
            ext_ref[sl, :, TILE_ROW0 - hw:TILE_ROW0, :] = buf_ref[:, :, slab(sl)]

    h = _norm_mod(x_ref[...], ng_ref[...], mod_ref[:, 1:2, :], mod_ref[:, 0:1, :])
    h = h.reshape(m, d).astype(BF16)
    nslice = GDN_QKV // MXU_N
    per = MXU_N // LANES

    def project(s):
        c0 = s * MXU_N
        u = jnp.dot(h, wqkv_ref[:, c0:c0 + MXU_N], preferred_element_type=F32).reshape(nb, tt, MXU_N)
        for half in range(per):
            ext_ref[s * per + half, :, TILE_ROW0:TILE_ROW0 + tt, :] = u[:, :, slab(half)]

    for s in range(min(GDN_AHEAD, nslice)):
        project(s)
    ab = jnp.dot(h, wab_ref[...], preferred_element_type=F32)
    lane = _iota2(ab.shape, 1)
    gb = jnp.where(lane < GDN_HEADS,
                   pv_ref[0:1, :] * _softplus(ab + pv_ref[1:2, :]),
                   _sigmoid(ab))
    gb_ref[...] = gb.reshape(nb, tt, LANES)
    outs = (q_ref, k_ref, v_ref)
    n4 = tt // CONV_STRIDE
    for sl in range(nslab):
        if sl % per == 0 and sl // per + GDN_AHEAD < nslice:
            project(sl // per + GDN_AHEAD)
        if sl == nslab - per:
            z_ref[...] = jnp.dot(h, wz_ref[...], preferred_element_type=F32).reshape(nb, tt, d)
        which, hd = divmod(sl, GDN_HEADS)
        for a in range(CONV_STRIDE):
            y = _silu(_conv_class(ext_ref, sl, a, cw_ref, GDN_CONV_TAPS, n4))
            if which < 2:
                r = lax.rsqrt(jnp.sum(y * y, axis=-1, keepdims=True) + NORM_EPS)
                if which == 0:
                    r = r * (GDN_DK ** -0.5)
                y = y * r
            outs[which][:, hd, pl.ds(a, n4, stride=CONV_STRIDE), :] = y
        nbuf_ref[:, :, slab(sl)] = ext_ref[sl, :, TILE_ROW0 + tt - hw:TILE_ROW0 + tt, :]
        ext_ref[sl, :, TILE_ROW0 - hw:TILE_ROW0, :] = ext_ref[sl, :, TILE_ROW0 + tt - hw:TILE_ROW0 + tt, :]


def _gdn_proj(x, mod, ng, wqkv, wz, wab, cw, pv, buf, nb, tt):
    b, t, d = x.shape
    kern = functools.partial(_gdn_proj_kernel, nb=nb, tt=tt)
    row = lambda i, j: (i, j, 0)
    const2 = lambda i, j: (0, 0)
    act = jax.ShapeDtypeStruct((b, t, d), F32)
    heads = jax.ShapeDtypeStruct((b, GDN_HEADS, t, GDN_DK), F32)
    return pl.pallas_call(
        kern,
        grid=(b // nb, t // tt),
        in_specs=[pl.BlockSpec((nb, tt, d), row),
                  pl.BlockSpec((nb, 8, d), lambda i, j: (i, 0, 0)),
                  pl.BlockSpec((1, d), const2),
                  _wspec((d, GDN_QKV)),
                  _wspec((d, d)),
                  pl.BlockSpec((d, LANES), const2),
                  pl.BlockSpec((4, GDN_QKV), const2),
                  pl.BlockSpec((8, LANES), const2),
                  pl.BlockSpec((nb, 3, GDN_QKV), lambda i, j: (i, 0, 0))],
        out_specs=[pl.BlockSpec((nb, GDN_HEADS, tt, GDN_DK), lambda i, j: (i, 0, j, 0))] * 3
                  + [pl.BlockSpec((nb, tt, d), row),
                     pl.BlockSpec((nb, tt, LANES), row),
                     pl.BlockSpec((nb, 3, GDN_QKV), lambda i, j: (i, 0, 0))],
        out_shape=[heads, heads, heads, act,
                   jax.ShapeDtypeStruct((b, t, LANES), F32),
                   jax.ShapeDtypeStruct((b, 3, GDN_QKV), F32)],
        scratch_shapes=[pltpu.VMEM((GDN_QKV // LANES, nb, TILE_ROW0 + tt, LANES), F32)],
        compiler_params=_cparams("arbitrary", "arbitrary"),
        name="gdn_proj",
    )(x, mod, ng, wqkv, wz, wab, cw, pv, buf)


def _tri_inverses(lmats, chunk):
    n = lmats[0].shape[0]
    row = _iota2((n, n), 0)
    col = _iota2((n, n), 1)
    level = 31 - lax.clz(jnp.bitwise_xor(row, col))
    eye = (row == col).astype(F32)
    xs = [eye - jnp.where(level == 0, lm, 0.0) for lm in lmats]
    k = 1
    while (1 << k) < chunk:
        xbs = [x.astype(BF16) for x in xs]
        ps = [_dot(xb, jnp.where(level == k, lm, 0.0)).astype(BF16) for xb, lm in zip(xbs, lmats)]
        xs = [x - _dot(p, xb) for x, p, xb in zip(xs, ps, xbs)]
        k += 1
    return xs


def _gdn_core_kernel(q_ref, k_ref, v_ref, z_ref, gb_ref, x_ref, mod_ref, og_ref, wout_ref, s0_ref,
                     y_ref, sout_ref, s_ref, o_ref, *, tt, chunk):
    @pl.when(pl.program_id(1) == 0)
    def _():
        s_ref[...] = s0_ref[0]

    nck = tt // chunk
    row = _iota2((tt, tt), 0)
    col = _iota2((tt, tt), 1)
    if nck > 1:
        same = (row // chunk) == (col // chunk)
        incl = (row >= col) & same
        strict = (row > col) & same
    else:
        same = None
        incl = row >= col
        strict = row > col
    gb = gb_ref[0]
    gcol = _sel_dot(incl.astype(BF16), gb)
    if nck > 1:
        glast = _sel_dot(same.astype(BF16), gb)
    else:
        glast = jnp.broadcast_to(jnp.sum(gb, axis=0, keepdims=True), gb.shape)
    grow = gcol.T
    eg = jnp.exp(gcol)
    ekd = jnp.exp(glast - gcol)
    egl = jnp.exp(glast)

    heads = range(GDN_HEADS)
    cols = [slice(hd * GDN_DK, (hd + 1) * GDN_DK) for hd in heads]
    kbs, lmats, qks = [], [], []
    for hd in heads:
        kh = k_ref[0, hd]
        diff = gcol[:, hd:hd + 1] - grow[hd:hd + 1, :]
        decay = jnp.exp(jnp.where(incl, diff, -jnp.inf))
        kb = kh * gb[:, GDN_HEADS + hd:GDN_HEADS + hd + 1]
        kbs.append(kb)
        lmats.append(jnp.where(strict, _dot_nt(kb, kh) * decay, 0.0))
        qks.append((_dot_nt(q_ref[0, hd], kh) * decay).astype(BF16))
    tinvs = [t.astype(BF16) for t in _tri_inverses(lmats, chunk)]
    us, ws = [], []
    for hd in heads:
        beta = gb[:, GDN_HEADS + hd:GDN_HEADS + hd + 1]
        uw = _dot(tinvs[hd], jnp.concatenate([v_ref[0, hd] * beta, kbs[hd] * eg[:, hd:hd + 1]], axis=1))
        us.append(uw[:, :GDN_DK])
        ws.append(uw[:, GDN_DK:].astype(BF16))
    ss = [s_ref[hd] for hd in heads]
    vnews = [[] for _ in heads]
    ocross = [[] for _ in heads]
    for c in range(nck):
        rows = slice(c * chunk, (c + 1) * chunk)
        for hd in heads:
            qd = (q_ref[0, hd, rows, :] * eg[rows, hd:hd + 1]).astype(BF16)
            ws_qs = _dot(jnp.concatenate([ws[hd][rows], qd], axis=0), ss[hd])
            vn = us[hd][rows] - ws_qs[:chunk]
            ocross[hd].append(ws_qs[chunk:])
            kd = k_ref[0, hd, rows, :] * ekd[rows, hd:hd + 1]
            ss[hd] = ss[hd] * egl[c * chunk:c * chunk + 1, hd:hd + 1] + _dot_tn(kd, vn)
            vnews[hd].append(vn.astype(BF16))
    for hd in heads:
        s_ref[hd] = ss[hd]
        vnew = jnp.concatenate(vnews[hd], axis=0) if nck > 1 else vnews[hd][0]
        oc = jnp.concatenate(ocross[hd], axis=0) if nck > 1 else ocross[hd][0]
        o = oc + _dot(qks[hd], vnew)
        o = o * lax.rsqrt(jnp.mean(o * o, axis=-1, keepdims=True) + NORM_EPS) * og_ref[...]
        o_ref[:, cols[hd]] = (o * _silu(z_ref[0, :, cols[hd]])).astype(BF16)

    out = jnp.dot(o_ref[...], wout_ref[...], preferred_element_type=F32)
    y_ref[0] = x_ref[0] + mod_ref[0, 2:3, :] * out
    sout_ref[0] = s_ref[...]


def _gdn_core(q, k, v, z, gb, x, mod, og, wout, s0, tt, chunk):
    b, t, d = x.shape
    kern = functools.partial(_gdn_core_kernel, tt=tt, chunk=chunk)
    row = lambda i, j: (i, j, 0)
    return pl.pallas_call(
        kern,
        grid=(b, t // tt),
        in_specs=[pl.BlockSpec((1, GDN_HEADS, tt, GDN_DK), lambda i, j: (i, 0, j, 0))] * 3
                 + [pl.BlockSpec((1, tt, d), row),
                    pl.BlockSpec((1, tt, LANES), row),
                    pl.BlockSpec((1, tt, d), row),
                    pl.BlockSpec((1, 8, d), lambda i, j: (i, 0, 0)),
                    pl.BlockSpec((1, GDN_DK), lambda i, j: (0, 0)),
                    _wspec((d, d)),
                    pl.BlockSpec((1, GDN_HEADS, GDN_DK, GDN_DK), lambda i, j: (i, 0, 0, 0))],
        out_specs=[pl.BlockSpec((1, tt, d), row),
                   pl.BlockSpec((1, GDN_HEADS, GDN_DK, GDN_DK), lambda i, j: (i, 0, 0, 0))],
        out_shape=[jax.ShapeDtypeStruct((b, t, d), F32),
                   jax.ShapeDtypeStruct((b, GDN_HEADS, GDN_DK, GDN_DK), F32)],
        scratch_shapes=[pltpu.VMEM((GDN_HEADS, GDN_DK, GDN_DK), F32),
                        pltpu.VMEM((tt, d), BF16)],
        compiler_params=_cparams("arbitrary", "arbitrary"),
        name="gdn_core",
    )(q, k, v, z, gb, x, mod, og, wout, s0)


def _sb_proj_kernel(x_ref, mod_ref, ng_ref, w_ref, qkg_ref,
                    kf_ref, vf_ref, qb_ref, kb_ref, vb_ref, u_ref, *, nb, tt):
    m = nb * tt
    d = D_MODEL
    h = _norm_mod(x_ref[...], ng_ref[...], mod_ref[:, 1:2, :], mod_ref[:, 0:1, :])
    h = h.reshape(m, d).astype(BF16)
    grp = ((_iota2((MXU_N, MXU_N), 0) // SB_DH) == (_iota2((MXU_N, MXU_N), 1) // SB_DH)).astype(BF16)
    for s in range(3 * d // MXU_N):
        c0 = s * MXU_N
        u_ref[:, c0:c0 + MXU_N] = jnp.dot(h, w_ref[:, c0:c0 + MXU_N], preferred_element_type=F32)
    for s in range(3 * d // MXU_N):
        c0 = s * MXU_N
        which, o0 = divmod(c0, d)
        u = u_ref[:, c0:c0 + MXU_N]
        if which < 2:
            ms = _dot_sel(u * u, grp) * (1.0 / SB_DH)
            u = u * lax.rsqrt(ms + NORM_EPS) * qkg_ref[which:which + 1, o0:o0 + MXU_N]
        u3 = u.reshape(nb, tt, MXU_N)
        if which == 0:
            qb_ref[:, :, o0:o0 + MXU_N] = u3.astype(BF16)
        elif which == 1:
            kf_ref[:, :, o0:o0 + MXU_N] = u3
            kb_ref[:, :, o0:o0 + MXU_N] = u3.astype(BF16)
        else:
            vf_ref[:, :, o0:o0 + MXU_N] = u3
            vb_ref[:, :, o0:o0 + MXU_N] = u3.astype(BF16)


def _sb_proj(x, mod, ng, w, qkg, nb, tt):
    b, t, d = x.shape
    kern = functools.partial(_sb_proj_kernel, nb=nb, tt=tt)
    row = lambda i, j: (i, j, 0)
    const2 = lambda i, j: (0, 0)
    f = jax.ShapeDtypeStruct((b, t, d), F32)
    h = jax.ShapeDtypeStruct((b, t, d), BF16)
    return pl.pallas_call(
        kern,
        grid=(b // nb, t // tt),
        in_specs=[pl.BlockSpec((nb, tt, d), row),
                  pl.BlockSpec((nb, 8, d), lambda i, j: (i, 0, 0)),
                  pl.BlockSpec((1, d), const2),
                  _wspec((d, 3 * d)),
                  pl.BlockSpec((8, d), const2)],
        out_specs=[pl.BlockSpec((nb, tt, d), row)] * 5,
        out_shape=[f, f, h, h, h],
        scratch_shapes=[pltpu.VMEM((nb * tt, 3 * d), F32)],
        compiler_params=_cparams("arbitrary", "arbitrary"),
        name="sb_proj",
    )(x, mod, ng, w, qkg)


def _suffix_selector(bw):
    r = jnp.bitwise_and(_iota2((2 * bw, bw + LANES), 0), bw - 1)
    c = _iota2((2 * bw, bw + LANES), 1)
    return jnp.where(c < bw, (r > c).astype(F32), 1.0).astype(BF16)


def _dot_nt_bf(a, b):
    return lax.dot_general(a, b, (((1,), (1,)), ((), ())), preferred_element_type=F32)


def _sb_tiles(qhs, kts, vts, carries, accs, mask, sel):
    heads = range(len(qhs))
    tq = qhs[0].shape[0]
    n = kts[0].shape[0]
    bw = min(LANES, n)
    zs = [_dot_nt_bf(qhs[hd], kts[hd]) for hd in heads]
    sps = [jnp.maximum(z, 0.0) + jnp.log(1.0 + jnp.exp(-jnp.abs(z))) for z in zs]
    spms = sps if mask is None else [jnp.where(mask, sp, 0.0) for sp in sps]
    carries = list(carries)
    accs = list(accs)
    ws = [[] for _ in heads]
    for mb in reversed(range(n // bw)):
        sl = slice(mb * bw, (mb + 1) * bw)
        his = [spm[:, sl].astype(BF16) for spm in spms]
        los = [(spm[:, sl] - hi.astype(F32)).astype(BF16) for spm, hi in zip(spms, his)]
        lhs = jnp.concatenate([jnp.concatenate([hi, lo], axis=1) for hi, lo in zip(his, los)], axis=0)
        r_all = jnp.dot(lhs, sel[bw], preferred_element_type=F32)
        for hd in heads:
            r = r_all[hd * tq:(hd + 1) * tq]
            w = jnp.exp(zs[hd][:, sl] - sps[hd][:, sl] - carries[hd][:, :bw] - r[:, :bw])
            if mask is not None:
                w = jnp.where(mask[:, sl], w, 0.0)
            ws[hd].append(w.astype(BF16))
            carries[hd] = carries[hd] + r[:, bw:]
    for hd in heads:
        wts = jnp.concatenate(ws[hd][::-1], axis=-1) if len(ws[hd]) > 1 else ws[hd][0]
        accs[hd] = accs[hd] + jnp.dot(wts, vts[hd], preferred_element_type=F32)
    return carries, accs


def _sb_head_queries(q):
    q = q * (SB_DH ** -0.5)
    lane = _iota2(q.shape, 1)
    return [jnp.where((lane // SB_DH) == hd, q, jnp.zeros_like(q)) for hd in range(2)]


def _sb_fast_kernel(q_ref, kd_ref, vd_ref, kp_ref, vp_ref, o_ref, need_ref, *, tq, more_before_prev):
    tp = kp_ref.shape[1]
    npair = D_MODEL // LANES
    sel = {bw: _suffix_selector(bw) for bw in {min(LANES, tq), min(LANES, tp)}}
    dmask = _iota2((tq, tq), 1) < _iota2((tq, tq), 0)
    zero = jnp.zeros((tq, LANES), F32)
    pairs = [slice(hp * LANES, (hp + 1) * LANES) for hp in range(npair)]
    qhs = [qh for hp in range(npair) for qh in _sb_head_queries(q_ref[0, :, pairs[hp]])]
    heads = range(2 * npair)

    def tiles(k_ref, v_ref, carries, accs, mask):
        return _sb_tiles(qhs, [k_ref[0, :, pairs[hd // 2]].astype(BF16) for hd in heads],
                         [v_ref[0, :, pairs[hd // 2]].astype(BF16) for hd in heads], carries, accs, mask, sel)

    carries, accs = tiles(kd_ref, vd_ref, [zero for _ in heads], [zero for _ in heads], dmask)
    carries, accs = tiles(kp_ref, vp_ref, carries, accs, None)

    lane = _iota2((tq, LANES), 1)
    cmin = None
    for hp in range(npair):
        o_ref[0, :, pairs[hp]] = jnp.where((lane // SB_DH) == 0, accs[2 * hp], accs[2 * hp + 1]).astype(BF16)
        c = jnp.minimum(carries[2 * hp], carries[2 * hp + 1])
        cmin = c if cmin is None else jnp.minimum(cmin, c)
    alive = jnp.where(jnp.min(cmin, axis=0, keepdims=True) < SB_DEAD, 1.0, 0.0)
    if not more_before_prev:
        alive = jnp.zeros_like(alive)
    need_ref[0, 0] = jnp.broadcast_to(alive, (8, LANES))


def _sb_fast(qb, kb, vb, prev_k, prev_v, tq, more_before_prev):
    b, t, d = qb.shape
    tp = prev_k.shape[1]
    kern = functools.partial(_sb_fast_kernel, tq=tq, more_before_prev=more_before_prev)
    cur = pl.BlockSpec((1, tq, d), lambda i, j: (i, j, 0))
    prev = pl.BlockSpec((1, tp, d), lambda i, j: (i, 0, 0))
    return pl.pallas_call(
        kern,
        grid=(b, t // tq),
        in_specs=[cur, cur, cur, prev, prev],
        out_specs=[cur, pl.BlockSpec((1, 1, 8, LANES), lambda i, j: (i, j, 0, 0))],
        out_shape=[jax.ShapeDtypeStruct((b, t, d), BF16),
                   jax.ShapeDtypeStruct((b, t // tq, 8, LANES), F32)],
        compiler_params=_cparams("arbitrary", "arbitrary"),
        name="sb_fast",
    )(qb, kb, vb, prev_k, prev_v)


def _sb_attention(qb, kb, vb, past_k, past_v, tq):
    b, t, d = qb.shape
    if past_k is None:
        return _sb_attn(qb, kb, vb, None, None, tq)
    assert t == tq
    plen = past_k.shape[1]
    tp = min(256, plen)
    tail = lambda c: c[:, plen - tp:].reshape(b, tp, d)
    o, need = _sb_fast(qb, kb, vb, tail(past_k), tail(past_v), tq, plen > tp)
    full = lambda: _sb_attn(qb, kb, vb, past_k.reshape(b, plen, d), past_v.reshape(b, plen, d), tq)
    return lax.cond(jnp.max(need) > 0.0, full, lambda: o)


def _sb_attn_kernel(*refs, tq, tk, past_len):
    if past_len:
        q_ref, k_ref, v_ref, pk_ref, pv_ref, o_ref = refs
    else:
        q_ref, k_ref, v_ref, o_ref = refs
        pk_ref = pv_ref = None
    qi = pl.program_id(2)
    lane = _iota2((tq, LANES), 1)
    pairs = [slice(p * LANES, (p + 1) * LANES) for p in range(SB_PAIRS)]
    qhs = [qh for p in range(SB_PAIRS) for qh in _sb_head_queries(q_ref[0, :, pairs[p]])]
    ts = tq
    sel = {bw: _suffix_selector(bw) for bw in {min(LANES, ts), min(LANES, tk)}}

    def both(kt, vt, state, mask):
        nh = 2 * SB_PAIRS
        carries, accs = _sb_tiles(qhs, [kt[:, pairs[hd // 2]] for hd in range(nh)],
                                  [vt[:, pairs[hd // 2]] for hd in range(nh)],
                                  state[0::2], state[1::2], mask, sel)
        return tuple(x for ca in zip(carries, accs) for x in ca)

    r0 = pl.multiple_of(qi * tq, tq)
    zero = jnp.zeros((tq, LANES), F32)
    dmask = _iota2((tq, tq), 1) < _iota2((tq, tq), 0)
    state = both(k_ref[0, pl.ds(r0, tq), :], v_ref[0, pl.ds(r0, tq), :], (zero,) * (4 * SB_PAIRS), dmask)

    def alive(st):
        cmin = st[0]
        for hd in range(1, 2 * SB_PAIRS):
            cmin = jnp.minimum(cmin, st[2 * hd])
        return jnp.min(cmin) < SB_DEAD

    def key_loop(ntiles, size, load, st):
        def cond(c):
            return jnp.logical_and(c[0] < ntiles, c[1])

        def body(c):
            kt, vt = load(pl.multiple_of((ntiles - 1 - c[0]) * size, size))
            nst = both(kt, vt, c[2:], None)
            return (c[0] + 1, alive(nst)) + nst

        return lax.while_loop(cond, body, (jnp.int32(0), alive(st)) + st)[2:]

    state = key_loop(qi * (tq // ts), ts,
                     lambda k0: (k_ref[0, pl.ds(k0, ts), :], v_ref[0, pl.ds(k0, ts), :]), state)
    if past_len:
        state = key_loop(past_len // tk, tk,
                         lambda k0: (pk_ref[0, pl.ds(k0, tk), :].astype(BF16),
                                     pv_ref[0, pl.ds(k0, tk), :].astype(BF16)), state)
    for p in range(SB_PAIRS):
        o_ref[0, :, pairs[p]] = jnp.where((lane // SB_DH) == 0, state[4 * p + 1], state[4 * p + 3]).astype(BF16)


def _sb_attn(qb, kb, vb, past_k, past_v, tq):
    b, t, d = qb.shape
    past_len = 0 if past_k is None else past_k.shape[1]
    tk = tq if not past_len else min(256, past_len)
    kern = functools.partial(_sb_attn_kernel, tq=tq, tk=tk, past_len=past_len)
    width = SB_PAIRS * LANES
    qspec = pl.BlockSpec((1, tq, width), lambda i, h, j: (i, j, h))
    kvspec = pl.BlockSpec((1, t, width), lambda i, h, j: (i, 0, h), pipeline_mode=pl.Buffered(1))
    in_specs = [qspec, kvspec, kvspec]
    args = [qb, kb, vb]
    if past_len:
        pspec = pl.BlockSpec((1, past_len, width), lambda i, h, j: (i, 0, h))
        in_specs += [pspec, pspec]
        args += [past_k, past_v]
    return pl.pallas_call(
        kern,
        grid=(b, d // width, t // tq),
        in_specs=in_specs,
        out_specs=qspec,
        out_shape=jax.ShapeDtypeStruct((b, t, d), BF16),
        compiler_params=_cparams("arbitrary", "arbitrary", "arbitrary"),
        name="sb_attn",
    )(*args)


def _out_proj_kernel(o_ref, x_ref, mod_ref, w_ref, y_ref, *, nb, tt):
    out = jnp.dot(o_ref[...].reshape(nb * tt, -1), w_ref[...], preferred_element_type=F32)
    y_ref[...] = x_ref[...] + mod_ref[:, 2:3, :] * out.reshape(nb, tt, D_MODEL)


def _out_proj(o, x, mod, w, nb, tt):
    b, t, d = x.shape
    kin = o.shape[-1]
    kern = functools.partial(_out_proj_kernel, nb=nb, tt=tt)
    row = lambda i, j: (i, j, 0)
    return pl.pallas_call(
        kern,
        grid=(b // nb, t // tt),
        in_specs=[pl.BlockSpec((nb, tt, kin), row),
                  pl.BlockSpec((nb, tt, d), row),
                  pl.BlockSpec((nb, 8, d), lambda i, j: (i, 0, 0)),
                  _wspec((kin, d))],
        out_specs=pl.BlockSpec((nb, tt, d), row),
        out_shape=jax.ShapeDtypeStruct((b, t, d), F32),
        compiler_params=_cparams("arbitrary", "arbitrary"),
        name="out_proj",
    )(o, x, mod, w)


def _ret_log_gamma(hd):
    return math.log1p(-(2.0 ** (-5.0 - hd)))


def _ret_proj_kernel(x_ref, mod_ref, ng_ref, w_ref, cos_ref, sin_ref,
                     q_ref, qd_ref, k_ref, kd_ref, v_ref, gt_ref, *, nb, tt, chunk):
    m = nb * tt
    d = D_MODEL
    hk = RET_HEADS * RET_DK
    hv = RET_HEADS * RET_DV
    h = _norm_mod(x_ref[...], ng_ref[...], mod_ref[:, 1:2, :], mod_ref[:, 0:1, :])
    h = h.reshape(m, d).astype(BF16)
    cos = cos_ref[...][None]
    sin = sin_ref[...][None]
    tpos = pl.program_id(1) * tt + _iota2((1, tt, 1), 1)
    assert chunk & (chunk - 1) == 0
    idx = jnp.bitwise_and(tpos, chunk - 1).astype(F32)
    half = RET_DK // 2
    for s in range(2 * hk // MXU_N):
        c0 = s * MXU_N
        which, o0 = divmod(c0, hk)
        hd = o0 // RET_DK
        lg = _ret_log_gamma(hd)
        u = jnp.dot(h, w_ref[:, c0:c0 + MXU_N], preferred_element_type=F32).reshape(nb, tt, MXU_N)
        x1 = u[:, :, :half]
        x2 = u[:, :, half:]
        r = jnp.concatenate([x1 * cos - x2 * sin, x1 * sin + x2 * cos], axis=-1)
        if which == 0:
            q_ref[:, :, o0:o0 + MXU_N] = r.astype(BF16)
            qd_ref[:, :, o0:o0 + MXU_N] = (r * jnp.exp(lg * (idx + 1.0))).astype(BF16)
        else:
            r = r * (RET_DK ** -0.5)
            k_ref[:, :, o0:o0 + MXU_N] = r.astype(BF16)
            kd_ref[:, :, o0:o0 + MXU_N] = (r * jnp.exp(lg * (chunk - 1.0 - idx))).astype(BF16)
    for s in range(2 * hv // MXU_N):
        c0 = 2 * hk + s * MXU_N
        u = jnp.dot(h, w_ref[:, c0:c0 + MXU_N], preferred_element_type=F32).reshape(nb, tt, MXU_N)
        o0 = s * MXU_N
        if o0 < hv:
            v_ref[:, :, o0:o0 + MXU_N] = u.astype(BF16)
        else:
            gt_ref[:, :, o0 - hv:o0 - hv + MXU_N] = u


def _ret_proj(x, mod, ng, w, cos, sin, nb, tt, chunk):
    b, t, d = x.shape
    hk = RET_HEADS * RET_DK
    hv = RET_HEADS * RET_DV
    kern = functools.partial(_ret_proj_kernel, nb=nb, tt=tt, chunk=chunk)
    row = lambda i, j: (i, j, 0)
    const2 = lambda i, j: (0, 0)
    qs = jax.ShapeDtypeStruct((b, t, hk), BF16)
    return pl.pallas_call(
        kern,
        grid=(b // nb, t // tt),
        in_specs=[pl.BlockSpec((nb, tt, d), row),
                  pl.BlockSpec((nb, 8, d), lambda i, j: (i, 0, 0)),
                  pl.BlockSpec((1, d), const2),
                  _wspec((d, 2 * hk + 2 * hv)),
                  pl.BlockSpec((tt, RET_DK // 2), lambda i, j: (j, 0)),
                  pl.BlockSpec((tt, RET_DK // 2), lambda i, j: (j, 0))],
        out_specs=[pl.BlockSpec((nb, tt, hk), row)] * 4
                  + [pl.BlockSpec((nb, tt, hv), row)] * 2,
        out_shape=[qs, qs, qs, qs,
                   jax.ShapeDtypeStruct((b, t, hv), BF16),
                   jax.ShapeDtypeStruct((b, t, hv), F32)],
        compiler_params=_cparams("arbitrary", "arbitrary"),
        name="ret_proj",
    )(x, mod, ng, w, cos, sin)


def _ret_core_kernel(q_ref, qd_ref, k_ref, kd_ref, v_ref, gt_ref, x_ref, mod_ref, og_ref, wout_ref, r0_ref,
                     y_ref, rout_ref, r_ref, o_ref, *, chunk):
    @pl.when(pl.program_id(1) == 0)
    def _():
        r_ref[...] = r0_ref[0]

    rel = (_iota2((chunk, chunk), 0) - _iota2((chunk, chunk), 1)).astype(F32)
    for hd in range(RET_HEADS):
        lg = _ret_log_gamma(hd)
        k0 = hd * RET_DK
        v0 = hd * RET_DV
        dmask = jnp.where(rel >= 0, jnp.exp(lg * jnp.maximum(rel, 0.0)), 0.0)
        vh = v_ref[0, :, v0:v0 + RET_DV]
        s = _dot_nt_bf(q_ref[0, :, k0:k0 + RET_DK], k_ref[0, :, k0:k0 + RET_DK]) * dmask
        r = r_ref[hd]
        o = (jnp.dot(s.astype(BF16), vh, preferred_element_type=F32)
             + jnp.dot(qd_ref[0, :, k0:k0 + RET_DK], r.astype(BF16), preferred_element_type=F32))
        r_ref[hd] = r * math.exp(lg * chunk) + lax.dot_general(
            kd_ref[0, :, k0:k0 + RET_DK], vh, (((0,), (0,)), ((), ())), preferred_element_type=F32)
        mu = jnp.mean(o, axis=-1, keepdims=True)
        dlt = o - mu
        var = jnp.mean(dlt * dlt, axis=-1, keepdims=True)
        o = dlt * lax.rsqrt(var + NORM_EPS) * og_ref[:, v0:v0 + RET_DV]
        o_ref[:, v0:v0 + RET_DV] = (o * _silu(gt_ref[0, :, v0:v0 + RET_DV])).astype(BF16)
    out = jnp.dot(o_ref[...], wout_ref[...], preferred_element_type=F32)
    y_ref[0] = x_ref[0] + mod_ref[0, 2:3, :] * out
    rout_ref[0] = r_ref[...]


def _ret_core(q, qd, k, kd, v, gt, x, mod, og, wout, r0, chunk):
    b, t, d = x.shape
    hk = RET_HEADS * RET_DK
    hv = RET_HEADS * RET_DV
    kern = functools.partial(_ret_core_kernel, chunk=chunk)
    row = lambda i, j: (i, j, 0)
    st = pl.BlockSpec((1, RET_HEADS, RET_DK, RET_DV), lambda i, j: (i, 0, 0, 0))
    return pl.pallas_call(
        kern,
        grid=(b, t // chunk),
        in_specs=[pl.BlockSpec((1, chunk, hk), row)] * 4
                 + [pl.BlockSpec((1, chunk, hv), row)] * 2
                 + [pl.BlockSpec((1, chunk, d), row),
                    pl.BlockSpec((1, 8, d), lambda i, j: (i, 0, 0)),
                    pl.BlockSpec((1, hv), lambda i, j: (0, 0)),
                    _wspec((hv, d)),
                    st],
        out_specs=[pl.BlockSpec((1, chunk, d), row), st],
        out_shape=[jax.ShapeDtypeStruct((b, t, d), F32),
                   jax.ShapeDtypeStruct((b, RET_HEADS, RET_DK, RET_DV), F32)],
        scratch_shapes=[pltpu.VMEM((RET_HEADS, RET_DK, RET_DV), F32),
                        pltpu.VMEM((chunk, hv), BF16)],
        compiler_params=_cparams("arbitrary", "arbitrary"),
        name="ret_core",
    )(q, qd, k, kd, v, gt, x, mod, og, wout, r0)


def _run_group(x, mod_all, states, ffn_bufs, pos0, wts):
    b, t, d = x.shape
    if t >= ROW_TILE:
        nb, tt = 1, ROW_TILE
    else:
        nb, tt = b, t
    new_states, new_ffn = [], []
    for i in range(DEPTH):
        kind, j = i % 3, i // 3
        mod = mod_all[i]
        if kind == 0:
            s0, cbuf = states[i]
            gw = wts["gdn"][j]
            q, k, v, z, gb, ncbuf = _gdn_proj(x, mod, wts["norm_mix_g"][i], gw["wqkv"], gw["wz"], gw["wab"],
                                              gw["conv_w"], gw["pv"], cbuf, nb, tt)
            x, s_new = _gdn_core(q, k, v, z, gb, x, mod, gw["norm_g"], gw["wout"], s0,
                                 min(GDN_TILE, t), min(CHUNK, t))
            new_states.append((s_new, ncbuf))
        elif kind == 1:
            pk, pv = states[i]
            sw = wts["sb"][j]
            kf, vf, qb, kb, vb = _sb_proj(x, mod, wts["norm_mix_g"][i], sw["win"], sw["qkg"], nb, tt)
            o = _sb_attention(qb, kb, vb, pk, pv, min(SB_TILE, t))
            x = _out_proj(o, x, mod, sw["wout"], nb, tt)
            new_states.append((kf.reshape(b, t, SB_HEADS, SB_DH), vf.reshape(b, t, SB_HEADS, SB_DH)))
        else:
            (r0,) = states[i]
            rw = wts["ret"][j]
            chunk = min(RET_CHUNK, t)
            half = RET_DK // 2
            inv_freq = RET_ROPE_BASE ** (-jnp.arange(half, dtype=F32) / half)
            ang = (pos0 + jnp.arange(t)).astype(F32)[:, None] * inv_freq[None, :]
            q, qd, k, kd, v, gt = _ret_proj(x, mod, wts["norm_mix_g"][i], rw["win"], jnp.cos(ang), jnp.sin(ang),
                                            nb, tt, chunk)
            x, r_new = _ret_core(q, qd, k, kd, v, gt, x, mod, rw["norm_g"], rw["wout"], r0, chunk)
            new_states.append((r_new,))
        fw = wts["ffn"][i]
        x, fbuf = _ffn(x, mod, wts["norm_ffn_g"][i], fw["win"], fw["conv_w"], fw["conv_b"], fw["wout"],
                       ffn_bufs[i], nb, tt)
        new_ffn.append(fbuf)
    return x, new_states, jnp.stack(new_ffn)


def kernel(x_prompt, x_sample, state_l0_gdn_S, state_l0_gdn_conv, cache_l1_sb_k, cache_l1_sb_v, state_l2_ret, state_l3_gdn_S, state_l3_gdn_conv, state_ffn_conv, c_prompt, c_sample, ada_w, ada_b, norm_mix_g, norm_ffn_g, gdn_w_in, gdn_conv_w, gdn_A_log, gdn_dt_bias, gdn_norm_g, gdn_w_out, sb_w_in, sb_q_norm_g, sb_k_norm_g, sb_w_out, ret_w_in, ret_norm_g, ret_w_out, ffn_w_in, ffn_conv_w, ffn_conv_b, ffn_w_out):
    d = D_MODEL
    bp, tp, _ = x_prompt.shape
    bs, ts, _ = x_sample.shape

    hv = GDN_HEADS * GDN_DK
    wts = {"norm_mix_g": [norm_mix_g[i].reshape(1, d) for i in range(DEPTH)],
           "norm_ffn_g": [norm_ffn_g[i].reshape(1, d) for i in range(DEPTH)],
           "gdn": [], "sb": [], "ret": [], "ffn": []}
    for j in range(gdn_w_in.shape[0]):
        w = gdn_w_in[j]
        wab = jnp.pad(w[:, GDN_QKV + hv:], ((0, 0), (0, LANES - 2 * GDN_HEADS)))
        pv = jnp.zeros((8, LANES), F32)
        pv = pv.at[0, :GDN_HEADS].set(-jnp.exp(gdn_A_log[j].astype(F32)))
        pv = pv.at[1, :GDN_HEADS].set(gdn_dt_bias[j].astype(F32))
        wts["gdn"].append({"wqkv": w[:, :GDN_QKV].astype(BF16),
                           "wz": w[:, GDN_QKV:GDN_QKV + hv].astype(BF16),
                           "wab": wab.astype(BF16),
                           "conv_w": gdn_conv_w[j], "pv": pv,
                           "norm_g": gdn_norm_g[j].reshape(1, GDN_DK),
                           "wout": gdn_w_out[j].astype(BF16)})
    for j in range(sb_w_in.shape[0]):
        qkg = jnp.zeros((8, d), F32)
        qkg = qkg.at[0].set(jnp.tile(sb_q_norm_g[j], SB_HEADS)).at[1].set(jnp.tile(sb_k_norm_g[j], SB_HEADS))
        wts["sb"].append({"win": sb_w_in[j].astype(BF16), "qkg": qkg, "wout": sb_w_out[j].astype(BF16)})
    for j in range(ret_w_in.shape[0]):
        wts["ret"].append({"win": ret_w_in[j].astype(BF16),
                           "norm_g": ret_norm_g[j].reshape(1, RET_HEADS * RET_DV),
                           "wout": ret_w_out[j].astype(BF16)})
    for i in range(DEPTH):
        wts["ffn"].append({"win": ffn_w_in[i].astype(BF16), "conv_w": ffn_conv_w[i],
                           "conv_b": ffn_conv_b[i].reshape(1, 2 * D_FF), "wout": ffn_w_out[i].astype(BF16)})

    nrow = bp + bs
    rows = -(-nrow // 8) * 8
    c_all = jnp.pad(jnp.concatenate([c_prompt, c_sample], axis=0), ((0, rows - nrow), (0, 0)))
    mod = _ada_mod(c_all, ada_w, ada_b)
    mod = jnp.pad(mod.reshape(DEPTH, rows, 6, d), ((0, 0), (0, 0), (0, 2), (0, 0)))
    mod_p, mod_s = mod[:, :bp], mod[:, bp:nrow]

    dt = x_prompt.dtype
    zero_states = [(jnp.zeros((bp, GDN_HEADS, GDN_DK, GDN_DK), dt), jnp.zeros((bp, 3, GDN_QKV), dt)),
                   (None, None),
                   (jnp.zeros((bp, RET_HEADS, RET_DK, RET_DV), dt),),
                   (jnp.zeros((bp, GDN_HEADS, GDN_DK, GDN_DK), dt), jnp.zeros((bp, 3, GDN_QKV), dt))]
    zero_ffn = jnp.zeros((DEPTH, bp, 2, 2 * D_FF), dt)
    y_prompt, p_states, p_ffn_conv = _run_group(x_prompt, mod_p, zero_states, zero_ffn, 0, wts)

    past_len = cache_l1_sb_k.shape[1]
    sample_states = [(state_l0_gdn_S, state_l0_gdn_conv),
                     (cache_l1_sb_k, cache_l1_sb_v),
                     (state_l2_ret,), (state_l3_gdn_S, state_l3_gdn_conv)]
    y_sample, s_states, s_ffn_conv = _run_group(x_sample, mod_s, sample_states, state_ffn_conv, past_len, wts)

    (p_l0_S, p_l0_conv), (p_l1_k, p_l1_v), (p_l2_R,), (p_l3_S, p_l3_conv) = p_states
    (s_l0_S, s_l0_conv), (s_l1_k, s_l1_v), (s_l2_R,), (s_l3_S, s_l3_conv) = s_states
    return (y_prompt, y_sample,
            p_l0_S, p_l0_conv, p_l1_k, p_l1_v, p_l2_R, p_l3_S, p_l3_conv, p_ffn_conv,
            s_l0_S, s_l0_conv, s_l1_k, s_l1_v, s_l2_R, s_l3_S, s_l3_conv, s_ffn_conv)
```

```python
import functools
import math

import jax
import jax.numpy as jnp
from jax import lax
from jax.experimental import pallas as pl
from jax.experimental.pallas import tpu as pltpu

F32 = jnp.float32
BF16 = jnp.bfloat16

D_MODEL = 1024
DEPTH = 4
CHUNK = 64
GDN_HEADS = 8
GDN_DK = 128
GDN_QKV = 3 * GDN_HEADS * GDN_DK
GDN_CONV_TAPS = 4
GDN_AHEAD = 12
CONV_STRIDE = 4
SB_HEADS = 16
SB_DH = 64
RET_HEADS = 4
RET_DK = 256
RET_DV = 512
RET_ROPE_BASE = 10000.0
D_FF = 2816
NORM_EPS = 1e-6

LANES = 128
TILE_ROW0 = 8
MXU_N = 256
VMEM_LIMIT = 56 * 1024 * 1024

ROW_TILE = 512
GDN_TILE = 128
RET_CHUNK = 256
SB_TILE = 256
SB_DEAD = 105.0
SB_PAIRS = 4


def _cparams(*sem):
    return pltpu.CompilerParams(dimension_semantics=sem, vmem_limit_bytes=VMEM_LIMIT)


def _wspec(shape):
    return pl.BlockSpec(shape, lambda i, j: (0,) * len(shape), pipeline_mode=pl.Buffered(1))


def _dot(a, b):
    return jnp.dot(a.astype(BF16), b.astype(BF16), preferred_element_type=F32)


def _dot_nt(a, b):
    return lax.dot_general(a.astype(BF16), b.astype(BF16), (((1,), (1,)), ((), ())),
                           preferred_element_type=F32)


def _dot_tn(a, b):
    return lax.dot_general(a.astype(BF16), b.astype(BF16), (((0,), (0,)), ((), ())),
                           preferred_element_type=F32)


def _split(x):
    hi = x.astype(BF16)
    lo = (x - hi.astype(F32)).astype(BF16)
    return hi, lo


def _dot_sel(a, sel):
    hi, lo = _split(a)
    return (jnp.dot(hi, sel, preferred_element_type=F32)
            + jnp.dot(lo, sel, preferred_element_type=F32))


def _sel_dot(sel, b):
    hi, lo = _split(b)
    return (jnp.dot(sel, hi, preferred_element_type=F32)
            + jnp.dot(sel, lo, preferred_element_type=F32))


def _dot3(a, b):
    ah, al = _split(a)
    bh, bl = _split(b)
    return (jnp.dot(ah, bh, preferred_element_type=F32)
            + jnp.dot(ah, bl, preferred_element_type=F32)
            + jnp.dot(al, bh, preferred_element_type=F32))


def _sigmoid(x):
    return 1.0 / (1.0 + jnp.exp(-x))


def _silu(x):
    return x * _sigmoid(x)


def _softplus(x):
    return jnp.maximum(x, 0.0) + jnp.log(1.0 + jnp.exp(-jnp.abs(x)))


def _norm_mod(x, gain, scale, shift):
    ms = jnp.mean(x * x, axis=-1, keepdims=True)
    y = x * lax.rsqrt(ms + NORM_EPS) * gain
    return y * (1.0 + scale) + shift


def _iota2(shape, dim):
    return lax.broadcasted_iota(jnp.int32, shape, dim)


def _ada_kernel(c_ref, w_ref, b_ref, o_ref):
    o_ref[0] = _dot(_silu(c_ref[...]), w_ref[0]) + b_ref[0]


def _ada_mod(c_all, ada_w, ada_b):
    rows = c_all.shape[0]
    n = ada_w.shape[2]
    tn = 1536
    return pl.pallas_call(
        _ada_kernel,
        grid=(DEPTH, n // tn),
        in_specs=[pl.BlockSpec((rows, D_MODEL), lambda i, j: (0, 0)),
                  pl.BlockSpec((1, D_MODEL, tn), lambda i, j: (i, 0, j)),
                  pl.BlockSpec((1, 1, tn), lambda i, j: (i, 0, j))],
        out_specs=pl.BlockSpec((1, rows, tn), lambda i, j: (i, 0, j)),
        out_shape=jax.ShapeDtypeStruct((DEPTH, rows, n), F32),
        compiler_params=_cparams("arbitrary", "arbitrary"),
        name="ada_mod",
    )(c_all, ada_w, ada_b.reshape(DEPTH, 1, n))


def _slab(sl):
    return slice(sl * LANES, (sl + 1) * LANES)


def _conv_class(ext_ref, sl, a, w_ref, width, n4):
    hw = width - 1
    y = ext_ref[sl, :, pl.ds(TILE_ROW0 + a, n4, stride=CONV_STRIDE), :] * w_ref[hw:hw + 1, _slab(sl)]
    for j in range(hw):
        y = y + ext_ref[sl, :, pl.ds(TILE_ROW0 + a - hw + j, n4, stride=CONV_STRIDE), :] * w_ref[j:j + 1, _slab(sl)]
    return y


FFN_COLS = 256
FFN_CONV_TAPS = 3


def _ffn_kernel(x_ref, mod_ref, ng_ref, win_ref, cw_ref, cb_ref, wout_ref, buf_ref,
                y_ref, nbuf_ref, ext_ref, *, nb, tt):
    m = nb * tt
    hw = FFN_CONV_TAPS - 1
    nchunk = D_FF // FFN_COLS
    tile = slice(TILE_ROW0, TILE_ROW0 + tt)
    carried = slice(TILE_ROW0 - hw, TILE_ROW0)

    @pl.when(pl.program_id(1) == 0)
    def _():
        ext_ref[:, carried, :] = buf_ref[...]

    def conv(cols):
        y = ext_ref[:, tile, cols] * cw_ref[hw:hw + 1, cols]
        for j in range(hw):
            y = y + ext_ref[:, TILE_ROW0 - hw + j:TILE_ROW0 - hw + j + tt, cols] * cw_ref[j:j + 1, cols]
        return y + cb_ref[:, cols]

    x = x_ref[...]
    h = _norm_mod(x, ng_ref[...], mod_ref[:, 4:5, :], mod_ref[:, 3:4, :])
    h = h.reshape(m, D_MODEL).astype(BF16)
    acc = jnp.zeros((m, D_MODEL), F32)

    for c in range(nchunk):
        for col0 in (c * FFN_COLS, D_FF + c * FFN_COLS):
            u = jnp.dot(h, win_ref[:, col0:col0 + FFN_COLS], preferred_element_type=F32)
            ext_ref[:, tile, col0:col0 + FFN_COLS] = u.reshape(nb, tt, FFN_COLS)
    for c in range(nchunk):
        gcols = slice(c * FFN_COLS, (c + 1) * FFN_COLS)
        vcols = slice(D_FF + c * FFN_COLS, D_FF + (c + 1) * FFN_COLS)
        a = (_silu(conv(gcols)) * conv(vcols)).reshape(m, FFN_COLS).astype(BF16)
        acc = acc + jnp.dot(a, wout_ref[gcols, :], preferred_element_type=F32)
    y_ref[...] = x + mod_ref[:, 5:6, :] * acc.reshape(nb, tt, D_MODEL)
    nbuf_ref[...] = ext_ref[:, TILE_ROW0 + tt - hw:TILE_ROW0 + tt, :]
    ext_ref[:, carried, :] = ext_ref[:, TILE_ROW0 + tt - hw:TILE_ROW0 + tt, :]


def _ffn(x, mod, ng, win, cw, cb, wout, buf, nb, tt):
    b, t, d = x.shape
    n2 = 2 * D_FF
    kern = functools.partial(_ffn_kernel, nb=nb, tt=tt)
    return pl.pallas_call(
        kern,
        grid=(b // nb, t // tt),
        in_specs=[pl.BlockSpec((nb, tt, d), lambda i, j: (i, j, 0)),
                  pl.BlockSpec((nb, 8, d), lambda i, j: (i, 0, 0)),
                  pl.BlockSpec((1, d), lambda i, j: (0, 0)),
                  _wspec((d, n2)),
                  pl.BlockSpec((3, n2), lambda i, j: (0, 0)),
                  pl.BlockSpec((1, n2), lambda i, j: (0, 0)),
                  _wspec((D_FF, d)),
                  pl.BlockSpec((nb, 2, n2), lambda i, j: (i, 0, 0))],
        out_specs=[pl.BlockSpec((nb, tt, d), lambda i, j: (i, j, 0)),
                   pl.BlockSpec((nb, 2, n2), lambda i, j: (i, 0, 0))],
        out_shape=[jax.ShapeDtypeStruct((b, t, d), F32),
                   jax.ShapeDtypeStruct((b, 2, n2), F32)],
        scratch_shapes=[pltpu.VMEM((nb, TILE_ROW0 + tt, n2), F32)],
        compiler_params=_cparams("arbitrary", "arbitrary"),
        name="conv_ffn",
    )(x, mod, ng, win, cw, cb, wout, buf)


def _gdn_proj_kernel(x_ref, mod_ref, ng_ref, wqkv_ref, wz_ref, wab_ref, cw_ref, pv_ref, buf_ref,
                     q_ref, k_ref, v_ref, z_ref, gb_ref, nbuf_ref, ext_ref, *, nb, tt):
    m = nb * tt
    d = D_MODEL
    nslab = GDN_QKV // LANES
    hw = GDN_CONV_TAPS - 1
    slab = _slab

    @pl.when(pl.program_id(1) == 0)
    def _():
        for sl in range(nslab):
            ext_ref[sl, :, TILE_ROW0 - hw:TILE_ROW0, :] = buf_ref[:, :, slab(sl)]

    h = _norm_mod(x_ref[...], ng_ref[...], mod_ref[:, 1:2, :], mod_ref[:, 0:1, :])
    h = h.reshape(m, d).astype(BF16)
    nslice = GDN_QKV // MXU_N
    per = MXU_N // LANES

    def project(s):
        c0 = s * MXU_N
        u = jnp.dot(h, wqkv_ref[:, c0:c0 + MXU_N], preferred_element_type=F32).reshape(nb, tt, MXU_N)
        for half in range(per):
            ext_ref[s * per + half, :, TILE_ROW0:TILE_ROW0 + tt, :] = u[:, :, slab(half)]

    for s in range(min(GDN_AHEAD, nslice)):
        project(s)
    ab = jnp.dot(h, wab_ref[...], preferred_element_type=F32)
    lane = _iota2(ab.shape, 1)
    gb = jnp.where(lane < GDN_HEADS,
                   pv_ref[0:1, :] * _softplus(ab + pv_ref[1:2, :]),
                   _sigmoid(ab))
    gb_ref[...] = gb.reshape(nb, tt, LANES)
    outs = (q_ref, k_ref, v_ref)
    n4 = tt // CONV_STRIDE
    for sl in range(nslab):
        if sl % per == 0 and sl // per + GDN_AHEAD < nslice:
            project(sl // per + GDN_AHEAD)
        if sl == nslab - per:
            z_ref[...] = jnp.dot(h, wz_ref[...], preferred_element_type=F32).reshape(nb, tt, d)
        which, hd = divmod(sl, GDN_HEADS)
        for a in range(CONV_STRIDE):
            y = _silu(_conv_class(ext_ref, sl, a, cw_ref, GDN_CONV_TAPS, n4))
            if which < 2:
                r = lax.rsqrt(jnp.sum(y * y, axis=-1, keepdims=True) + NORM_EPS)
                if which == 0:
                    r = r * (GDN_DK ** -0.5)
                y = y * r
            outs[which][:, hd, pl.ds(a, n4, stride=CONV_STRIDE), :] = y
        nbuf_ref[:, :, slab(sl)] = ext_ref[sl, :, TILE_ROW0 + tt - hw:TILE_ROW0 + tt, :]
        ext_ref[sl, :, TILE_ROW0 - hw:TILE_ROW0, :] = ext_ref[sl, :, TILE_ROW0 + tt - hw:TILE_ROW0 + tt, :]


def _gdn_proj(x, mod, ng, wqkv, wz, wab, cw, pv, buf, nb, tt):
    b, t, d = x.shape
    kern = functools.partial(_gdn_proj_kernel, nb=nb, tt=tt)
    row = lambda i, j: (i, j, 0)
    const2 = lambda i, j: (0, 0)
    act = jax.ShapeDtypeStruct((b, t, d), F32)
    heads = jax.ShapeDtypeStruct((b, GDN_HEADS, t, GDN_DK), F32)
    return pl.pallas_call(
        kern,
        grid=(b // nb, t // tt),
        in_specs=[pl.BlockSpec((nb, tt, d), row),
                  pl.BlockSpec((nb, 8, d), lambda i, j: (i, 0, 0)),
                  pl.BlockSpec((1, d), const2),
                  _wspec((d, GDN_QKV)),
                  _wspec((d, d)),
                  pl.BlockSpec((d, LANES), const2),
                  pl.BlockSpec((4, GDN_QKV), const2),
                  pl.BlockSpec((8, LANES), const2),
                  pl.BlockSpec((nb, 3, GDN_QKV), lambda i, j: (i, 0, 0))],
        out_specs=[pl.BlockSpec((nb, GDN_HEADS, tt, GDN_DK), lambda i, j: (i, 0, j, 0))] * 3
                  + [pl.BlockSpec((nb, tt, d), row),
                     pl.BlockSpec((nb, tt, LANES), row),
                     pl.BlockSpec((nb, 3, GDN_QKV), lambda i, j: (i, 0, 0))],
        out_shape=[heads, heads, heads, act,
                   jax.ShapeDtypeStruct((b, t, LANES), F32),
                   jax.ShapeDtypeStruct((b, 3, GDN_QKV), F32)],
        scratch_shapes=[pltpu.VMEM((GDN_QKV // LANES, nb, TILE_ROW0 + tt, LANES), F32)],
        compiler_params=_cparams("arbitrary", "arbitrary"),
        name="gdn_proj",
    )(x, mod, ng, wqkv, wz, wab, cw, pv, buf)


def _tri_inverses(lmats, chunk):
    n = lmats[0].shape[0]
    row = _iota2((n, n), 0)
    col = _iota2((n, n), 1)
    level = 31 - lax.clz(jnp.bitwise_xor(row, col))
    eye = (row == col).astype(F32)
    xs = [eye - jnp.where(level == 0, lm, 0.0) for lm in lmats]
    k = 1
    while (1 << k) < chunk:
        xbs = [x.astype(BF16) for x in xs]
        ps = [_dot(xb, jnp.where(level == k, lm, 0.0)).astype(BF16) for xb, lm in zip(xbs, lmats)]
        xs = [x - _dot(p, xb) for x, p, xb in zip(xs, ps, xbs)]
        k += 1
    return xs


def _gdn_core_kernel(q_ref, k_ref, v_ref, z_ref, gb_ref, x_ref, mod_ref, og_ref, wout_ref, s0_ref,
                     y_ref, sout_ref, s_ref, o_ref, *, tt, chunk):
    @pl.when(pl.program_id(1) == 0)
    def _():
        s_ref[...] = s0_ref[0]

    nck = tt // chunk
    row = _iota2((tt, tt), 0)
    col = _iota2((tt, tt), 1)
    if nck > 1:
        same = (row // chunk) == (col // chunk)
        incl = (row >= col) & same
        strict = (row > col) & same
    else:
        same = None
        incl = row >= col
        strict = row > col
    gb = gb_ref[0]
    gcol = _sel_dot(incl.astype(BF16), gb)
    if nck > 1:
        glast = _sel_dot(same.astype(BF16), gb)
    else:
        glast = jnp.broadcast_to(jnp.sum(gb, axis=0, keepdims=True), gb.shape)
    grow = gcol.T
    eg = jnp.exp(gcol)
    ekd = jnp.exp(glast - gcol)
    egl = jnp.exp(glast)

    heads = range(GDN_HEADS)
    cols = [slice(hd * GDN_DK, (hd + 1) * GDN_DK) for hd in heads]
    kbs, lmats, qks = [], [], []
    for hd in heads:
        kh = k_ref[0, hd]
        diff = gcol[:, hd:hd + 1] - grow[hd:hd + 1, :]
        decay = jnp.exp(jnp.where(incl, diff, -jnp.inf))
        kb = kh * gb[:, GDN_HEADS + hd:GDN_HEADS + hd + 1]
        kbs.append(kb)
        lmats.append(jnp.where(strict, _dot_nt(kb, kh) * decay, 0.0))
        qks.append((_dot_nt(q_ref[0, hd], kh) * decay).astype(BF16))
    tinvs = [t.astype(BF16) for t in _tri_inverses(lmats, chunk)]
    us, ws = [], []
    for hd in heads:
        beta = gb[:, GDN_HEADS + hd:GDN_HEADS + hd + 1]
        uw = _dot(tinvs[hd], jnp.concatenate([v_ref[0, hd] * beta, kbs[hd] * eg[:, hd:hd + 1]], axis=1))
        us.append(uw[:, :GDN_DK])
        ws.append(uw[:, GDN_DK:].astype(BF16))
    ss = [s_ref[hd] for hd in heads]
    vnews = [[] for _ in heads]
    ocross = [[] for _ in heads]
    for c in range(nck):
        rows = slice(c * chunk, (c + 1) * chunk)
        for hd in heads:
            qd = (q_ref[0, hd, rows, :] * eg[rows, hd:hd + 1]).astype(BF16)
            ws_qs = _dot(jnp.concatenate([ws[hd][rows], qd], axis=0), ss[hd])
            vn = us[hd][rows] - ws_qs[:chunk]
            ocross[hd].append(ws_qs[chunk:])
            kd = k_ref[0, hd, rows, :] * ekd[rows, hd:hd + 1]
            ss[hd] = ss[hd] * egl[c * chunk:c * chunk + 1, hd:hd + 1] + _dot_tn(kd, vn)
            vnews[hd].append(vn.astype(BF16))
    for hd in heads:
        s_ref[hd] = ss[hd]
        vnew = jnp.concatenate(vnews[hd], axis=0) if nck > 1 else vnews[hd][0]
        oc = jnp.concatenate(ocross[hd], axis=0) if nck > 1 else ocross[hd][0]
        o = oc + _dot(qks[hd], vnew)
        o = o * lax.rsqrt(jnp.mean(o * o, axis=-1, keepdims=True) + NORM_EPS) * og_ref[...]
        o_ref[:, cols[hd]] = (o * _silu(z_ref[0, :, cols[hd]])).astype(BF16)

    out = jnp.dot(o_ref[...], wout_ref[...], preferred_element_type=F32)
    y_ref[0] = x_ref[0] + mod_ref[0, 2:3, :] * out
    sout_ref[0] = s_ref[...]


def _gdn_core(q, k, v, z, gb, x, mod, og, wout, s0, tt, chunk):
    b, t, d = x.shape
    kern = functools.partial(_gdn_core_kernel, tt=tt, chunk=chunk)
    row = lambda i, j: (i, j, 0)
    return pl.pallas_call(
        kern,
        grid=(b, t // tt),
        in_specs=[pl.BlockSpec((1, GDN_HEADS, tt, GDN_DK), lambda i, j: (i, 0, j, 0))] * 3
                 + [pl.BlockSpec((1, tt, d), row),
                    pl.BlockSpec((1, tt, LANES), row),
                    pl.BlockSpec((1, tt, d), row),
                    pl.BlockSpec((1, 8, d), lambda i, j: (i, 0, 0)),
                    pl.BlockSpec((1, GDN_DK), lambda i, j: (0, 0)),
                    _wspec((d, d)),
                    pl.BlockSpec((1, GDN_HEADS, GDN_DK, GDN_DK), lambda i, j: (i, 0, 0, 0))],
        out_specs=[pl.BlockSpec((1, tt, d), row),
                   pl.BlockSpec((1, GDN_HEADS, GDN_DK, GDN_DK), lambda i, j: (i, 0, 0, 0))],
        out_shape=[jax.ShapeDtypeStruct((b, t, d), F32),
                   jax.ShapeDtypeStruct((b, GDN_HEADS, GDN_DK, GDN_DK), F32)],
        scratch_shapes=[pltpu.VMEM((GDN_HEADS, GDN_DK, GDN_DK), F32),
                        pltpu.VMEM((tt, d), BF16)],
        compiler_params=_cparams("arbitrary", "arbitrary"),
        name="gdn_core",
    )(q, k, v, z, gb, x, mod, og, wout, s0)


def _sb_proj_kernel(x_ref, mod_ref, ng_ref, w_ref, qkg_ref,
                    kf_ref, vf_ref, qb_ref, kb_ref, vb_ref, u_ref, *, nb, tt):
    m = nb * tt
    d = D_MODEL
    h = _norm_mod(x_ref[...], ng_ref[...], mod_ref[:, 1:2, :], mod_ref[:, 0:1, :])
    h = h.reshape(m, d).astype(BF16)
    grp = ((_iota2((MXU_N, MXU_N), 0) // SB_DH) == (_iota2((MXU_N, MXU_N), 1) // SB_DH)).astype(BF16)
    for s in range(3 * d // MXU_N):
        c0 = s * MXU_N
        u_ref[:, c0:c0 + MXU_N] = jnp.dot(h, w_ref[:, c0:c0 + MXU_N], preferred_element_type=F32)
    for s in range(3 * d // MXU_N):
        c0 = s * MXU_N
        which, o0 = divmod(c0, d)
        u = u_ref[:, c0:c0 + MXU_N]
        if which < 2:
            ms = _dot_sel(u * u, grp) * (1.0 / SB_DH)
            u = u * lax.rsqrt(ms + NORM_EPS) * qkg_ref[which:which + 1, o0:o0 + MXU_N]
        u3 = u.reshape(nb, tt, MXU_N)
        if which == 0:
            qb_ref[:, :, o0:o0 + MXU_N] = u3.astype(BF16)
        elif which == 1:
            kf_ref[:, :, o0:o0 + MXU_N] = u3
            kb_ref[:, :, o0:o0 + MXU_N] = u3.astype(BF16)
        else:
            vf_ref[:, :, o0:o0 + MXU_N] = u3
            vb_ref[:, :, o0:o0 + MXU_N] = u3.astype(BF16)


def _sb_proj(x, mod, ng, w, qkg, nb, tt):
    b, t, d = x.shape
    kern = functools.partial(_sb_proj_kernel, nb=nb, tt=tt)
    row = lambda i, j: (i, j, 0)
    const2 = lambda i, j: (0, 0)
    f = jax.ShapeDtypeStruct((b, t, d), F32)
    h = jax.ShapeDtypeStruct((b, t, d), BF16)
    return pl.pallas_call(
        kern,
        grid=(b // nb, t // tt),
        in_specs=[pl.BlockSpec((nb, tt, d), row),
                  pl.BlockSpec((nb, 8, d), lambda i, j: (i, 0, 0)),
                  pl.BlockSpec((1, d), const2),
                  _wspec((d, 3 * d)),
                  pl.BlockSpec((8, d), const2)],
        out_specs=[pl.BlockSpec((nb, tt, d), row)] * 5,
        out_shape=[f, f, h, h, h],
        scratch_shapes=[pltpu.VMEM((nb * tt, 3 * d), F32)],
        compiler_params=_cparams("arbitrary", "arbitrary"),
        name="sb_proj",
    )(x, mod, ng, w, qkg)


def _suffix_selector(bw):
    r = jnp.bitwise_and(_iota2((2 * bw, bw + LANES), 0), bw - 1)
    c = _iota2((2 * bw, bw + LANES), 1)
    return jnp.where(c < bw, (r > c).astype(F32), 1.0).astype(BF16)


def _dot_nt_bf(a, b):
    return lax.dot_general(a, b, (((1,), (1,)), ((), ())), preferred_element_type=F32)


def _sb_tiles(qhs, kts, vts, carries, accs, mask, sel):
    heads = range(len(qhs))
    tq = qhs[0].shape[0]
    n = kts[0].shape[0]
    bw = min(LANES, n)
    zs = [_dot_nt_bf(qhs[hd], kts[hd]) for hd in heads]
    sps = [jnp.maximum(z, 0.0) + jnp.log(1.0 + jnp.exp(-jnp.abs(z))) for z in zs]
    spms = sps if mask is None else [jnp.where(mask, sp, 0.0) for sp in sps]
    carries = list(carries)
    accs = list(accs)
    ws = [[] for _ in heads]
    for mb in reversed(range(n // bw)):
        sl = slice(mb * bw, (mb + 1) * bw)
        his = [spm[:, sl].astype(BF16) for spm in spms]
        los = [(spm[:, sl] - hi.astype(F32)).astype(BF16) for spm, hi in zip(spms, his)]
        lhs = jnp.concatenate([jnp.concatenate([hi, lo], axis=1) for hi, lo in zip(his, los)], axis=0)
        r_all = jnp.dot(lhs, sel[bw], preferred_element_type=F32)
        for hd in heads:
            r = r_all[hd * tq:(hd + 1) * tq]
            w = jnp.exp(zs[hd][:, sl] - sps[hd][:, sl] - carries[hd][:, :bw] - r[:, :bw])
            if mask is not None:
                w = jnp.where(mask[:, sl], w, 0.0)
            ws[hd].append(w.astype(BF16))
            carries[hd] = carries[hd] + r[:, bw:]
    for hd in heads:
        wts = jnp.concatenate(ws[hd][::-1], axis=-1) if len(ws[hd]) > 1 else ws[hd][0]
        accs[hd] = accs[hd] + jnp.dot(wts, vts[hd], preferred_element_type=F32)
    return carries, accs


def _sb_head_queries(q):
    q = q * (SB_DH ** -0.5)
    lane = _iota2(q.shape, 1)
    return [jnp.where((lane // SB_DH) == hd, q, jnp.zeros_like(q)) for hd in range(2)]


def _sb_fast_kernel(q_ref, kd_ref, vd_ref, kp_ref, vp_ref, o_ref, need_ref, *, tq, more_before_prev):
    tp = kp_ref.shape[1]
    npair = D_MODEL // LANES
    sel = {bw: _suffix_selector(bw) for bw in {min(LANES, tq), min(LANES, tp)}}
    dmask = _iota2((tq, tq), 1) < _iota2((tq, tq), 0)
    zero = jnp.zeros((tq, LANES), F32)
    pairs = [slice(hp * LANES, (hp + 1) * LANES) for hp in range(npair)]
    qhs = [qh for hp in range(npair) for qh in _sb_head_queries(q_ref[0, :, pairs[hp]])]
    heads = range(2 * npair)

    def tiles(k_ref, v_ref, carries, accs, mask):
        return _sb_tiles(qhs, [k_ref[0, :, pairs[hd // 2]].astype(BF16) for hd in heads],
                         [v_ref[0, :, pairs[hd // 2]].astype(BF16) for hd in heads], carries, accs, mask, sel)

    carries, accs = tiles(kd_ref, vd_ref, [zero for _ in heads], [zero for _ in heads], dmask)
    carries, accs = tiles(kp_ref, vp_ref, carries, accs, None)

    lane = _iota2((tq, LANES), 1)
    cmin = None
    for hp in range(npair):
        o_ref[0, :, pairs[hp]] = jnp.where((lane // SB_DH) == 0, accs[2 * hp], accs[2 * hp + 1]).astype(BF16)
        c = jnp.minimum(carries[2 * hp], carries[2 * hp + 1])
        cmin = c if cmin is None else jnp.minimum(cmin, c)
    alive = jnp.where(jnp.min(cmin, axis=0, keepdims=True) < SB_DEAD, 1.0, 0.0)
    if not more_before_prev:
        alive = jnp.zeros_like(alive)
    need_ref[0, 0] = jnp.broadcast_to(alive, (8, LANES))


def _sb_fast(qb, kb, vb, prev_k, prev_v, tq, more_before_prev):
    b, t, d = qb.shape
    tp = prev_k.shape[1]
    kern = functools.partial(_sb_fast_kernel, tq=tq, more_before_prev=more_before_prev)
    cur = pl.BlockSpec((1, tq, d), lambda i, j: (i, j, 0))
    prev = pl.BlockSpec((1, tp, d), lambda i, j: (i, 0, 0))
    return pl.pallas_call(
        kern,
        grid=(b, t // tq),
        in_specs=[cur, cur, cur, prev, prev],
        out_specs=[cur, pl.BlockSpec((1, 1, 8, LANES), lambda i, j: (i, j, 0, 0))],
        out_shape=[jax.ShapeDtypeStruct((b, t, d), BF16),
                   jax.ShapeDtypeStruct((b, t // tq, 8, LANES), F32)],
        compiler_params=_cparams("arbitrary", "arbitrary"),
        name="sb_fast",
    )(qb, kb, vb, prev_k, prev_v)


def _sb_attention(qb, kb, vb, past_k, past_v, tq):
    b, t, d = qb.shape
    if past_k is None:
        return _sb_attn(qb, kb, vb, None, None, tq)
    assert t == tq
    plen = past_k.shape[1]
    tp = min(256, plen)
    tail = lambda c: c[:, plen - tp:].reshape(b, tp, d)
    o, need = _sb_fast(qb, kb, vb, tail(past_k), tail(past_v), tq, plen > tp)
    full = lambda: _sb_attn(qb, kb, vb, past_k.reshape(b, plen, d), past_v.reshape(b, plen, d), tq)
    return lax.cond(jnp.max(need) > 0.0, full, lambda: o)


def _sb_attn_kernel(*refs, tq, tk, past_len):
    if past_len:
        q_ref, k_ref, v_ref, pk_ref, pv_ref, o_ref = refs
    else:
        q_ref, k_ref, v_ref, o_ref = refs
        pk_ref = pv_ref = None
    qi = pl.program_id(2)
    lane = _iota2((tq, LANES), 1)
    pairs = [slice(p * LANES, (p + 1) * LANES) for p in range(SB_PAIRS)]
    qhs = [qh for p in range(SB_PAIRS) for qh in _sb_head_queries(q_ref[0, :, pairs[p]])]
    ts = tq
    sel = {bw: _suffix_selector(bw) for bw in {min(LANES, ts), min(LANES, tk)}}

    def both(kt, vt, state, mask):
        nh = 2 * SB_PAIRS
        carries, accs = _sb_tiles(qhs, [kt[:, pairs[hd // 2]] for hd in range(nh)],
                                  [vt[:, pairs[hd // 2]] for hd in range(nh)],
                                  state[0::2], state[1::2], mask, sel)
        return tuple(x for ca in zip(carries, accs) for x in ca)

    r0 = pl.multiple_of(qi * tq, tq)
    zero = jnp.zeros((tq, LANES), F32)
    dmask = _iota2((tq, tq), 1) < _iota2((tq, tq), 0)
    state = both(k_ref[0, pl.ds(r0, tq), :], v_ref[0, pl.ds(r0, tq), :], (zero,) * (4 * SB_PAIRS), dmask)

    def alive(st):
        cmin = st[0]
        for hd in range(1, 2 * SB_PAIRS):
            cmin = jnp.minimum(cmin, st[2 * hd])
        return jnp.min(cmin) < SB_DEAD

    def key_loop(ntiles, size, load, st):
        def cond(c):
            return jnp.logical_and(c[0] < ntiles, c[1])

        def body(c):
            kt, vt = load(pl.multiple_of((ntiles - 1 - c[0]) * size, size))
            nst = both(kt, vt, c[2:], None)
            return (c[0] + 1, alive(nst)) + nst

        return lax.while_loop(cond, body, (jnp.int32(0), alive(st)) + st)[2:]

    state = key_loop(qi * (tq // ts), ts,
                     lambda k0: (k_ref[0, pl.ds(k0, ts), :], v_ref[0, pl.ds(k0, ts), :]), state)
    if past_len:
        state = key_loop(past_len // tk, tk,
                         lambda k0: (pk_ref[0, pl.ds(k0, tk), :].astype(BF16),
                                     pv_ref[0, pl.ds(k0, tk), :].astype(BF16)), state)
    for p in range(SB_PAIRS):
        o_ref[0, :, pairs[p]] = jnp.where((lane // SB_DH) == 0, state[4 * p + 1], state[4 * p + 3]).astype(BF16)


def _sb_attn(qb, kb, vb, past_k, past_v, tq):
    b, t, d = qb.shape
    past_len = 0 if past_k is None else past_k.shape[1]
    tk = tq if not past_len else min(256, past_len)
    kern = functools.partial(_sb_attn_kernel, tq=tq, tk=tk, past_len=past_len)
    width = SB_PAIRS * LANES
    qspec = pl.BlockSpec((1, tq, width), lambda i, h, j: (i, j, h))
    kvspec = pl.BlockSpec((1, t, width), lambda i, h, j: (i, 0, h), pipeline_mode=pl.Buffered(1))
    in_specs = [qspec, kvspec, kvspec]
    args = [qb, kb, vb]
    if past_len:
        pspec = pl.BlockSpec((1, past_len, width), lambda i, h, j: (i, 0, h))
        in_specs += [pspec, pspec]
        args += [past_k, past_v]
    return pl.pallas_call(
        kern,
        grid=(b, d // width, t // tq),
        in_specs=in_specs,
        out_specs=qspec,
        out_shape=jax.ShapeDtypeStruct((b, t, d), BF16),
        compiler_params=_cparams("arbitrary", "arbitrary", "arbitrary"),
        name="sb_attn",
    )(*args)


def _out_proj_kernel(o_ref, x_ref, mod_ref, w_ref, y_ref, *, nb, tt):
    out = jnp.dot(o_ref[...].reshape(nb * tt, -1), w_ref[...], preferred_element_type=F32)
    y_ref[...] = x_ref[...] + mod_ref[:, 2:3, :] * out.reshape(nb, tt, D_MODEL)


def _out_proj(o, x, mod, w, nb, tt):
    b, t, d = x.shape
    kin = o.shape[-1]
    kern = functools.partial(_out_proj_kernel, nb=nb, tt=tt)
    row = lambda i, j: (i, j, 0)
    return pl.pallas_call(
        kern,
        grid=(b // nb, t // tt),
        in_specs=[pl.BlockSpec((nb, tt, kin), row),
                  pl.BlockSpec((nb, tt, d), row),
                  pl.BlockSpec((nb, 8, d), lambda i, j: (i, 0, 0)),
                  _wspec((kin, d))],
        out_specs=pl.BlockSpec((nb, tt, d), row),
        out_shape=jax.ShapeDtypeStruct((b, t, d), F32),
        compiler_params=_cparams("arbitrary", "arbitrary"),
        name="out_proj",
    )(o, x, mod, w)


def _ret_log_gamma(hd):
    return math.log1p(-(2.0 ** (-5.0 - hd)))


def _ret_proj_kernel(x_ref, mod_ref, ng_ref, w_ref, cos_ref, sin_ref,
                     q_ref, qd_ref, k_ref, kd_ref, v_ref, gt_ref, *, nb, tt, chunk):
    m = nb * tt
    d = D_MODEL
    hk = RET_HEADS * RET_DK
    hv = RET_HEADS * RET_DV
    h = _norm_mod(x_ref[...], ng_ref[...], mod_ref[:, 1:2, :], mod_ref[:, 0:1, :])
    h = h.reshape(m, d).astype(BF16)
    cos = cos_ref[...][None]
    sin = sin_ref[...][None]
    tpos = pl.program_id(1) * tt + _iota2((1, tt, 1), 1)
    assert chunk & (chunk - 1) == 0
    idx = jnp.bitwise_and(tpos, chunk - 1).astype(F32)
    half = RET_DK // 2
    for s in range(2 * hk // MXU_N):
        c0 = s * MXU_N
        which, o0 = divmod(c0, hk)
        hd = o0 // RET_DK
        lg = _ret_log_gamma(hd)
        u = jnp.dot(h, w_ref[:, c0:c0 + MXU_N], preferred_element_type=F32).reshape(nb, tt, MXU_N)
        x1 = u[:, :, :half]
        x2 = u[:, :, half:]
        r = jnp.concatenate([x1 * cos - x2 * sin, x1 * sin + x2 * cos], axis=-1)
        if which == 0:
            q_ref[:, :, o0:o0 + MXU_N] = r.astype(BF16)
            qd_ref[:, :, o0:o0 + MXU_N] = (r * jnp.exp(lg * (idx + 1.0))).astype(BF16)
        else:
            r = r * (RET_DK ** -0.5)
            k_ref[:, :, o0:o0 + MXU_N] = r.astype(BF16)
            kd_ref[:, :, o0:o0 + MXU_N] = (r * jnp.exp(lg * (chunk - 1.0 - idx))).astype(BF16)
    for s in range(2 * hv // MXU_N):
        c0 = 2 * hk + s * MXU_N
        u = jnp.dot(h, w_ref[:, c0:c0 + MXU_N], preferred_element_type=F32).reshape(nb, tt, MXU_N)
        o0 = s * MXU_N
        if o0 < hv:
            v_ref[:, :, o0:o0 + MXU_N] = u.astype(BF16)
        else:
            gt_ref[:, :, o0 - hv:o0 - hv + MXU_N] = u


def _ret_proj(x, mod, ng, w, cos, sin, nb, tt, chunk):
    b, t, d = x.shape
    hk = RET_HEADS * RET_DK
    hv = RET_HEADS * RET_DV
    kern = functools.partial(_ret_proj_kernel, nb=nb, tt=tt, chunk=chunk)
    row = lambda i, j: (i, j, 0)
    const2 = lambda i, j: (0, 0)
    qs = jax.ShapeDtypeStruct((b, t, hk), BF16)
    return pl.pallas_call(
        kern,
        grid=(b // nb, t // tt),
        in_specs=[pl.BlockSpec((nb, tt, d), row),
                  pl.BlockSpec((nb, 8, d), lambda i, j: (i, 0, 0)),
                  pl.BlockSpec((1, d), const2),
                  _wspec((d, 2 * hk + 2 * hv)),
                  pl.BlockSpec((tt, RET_DK // 2), lambda i, j: (j, 0)),
                  pl.BlockSpec((tt, RET_DK // 2), lambda i, j: (j, 0))],
        out_specs=[pl.BlockSpec((nb, tt, hk), row)] * 4
                  + [pl.BlockSpec((nb, tt, hv), row)] * 2,
        out_shape=[qs, qs, qs, qs,
                   jax.ShapeDtypeStruct((b, t, hv), BF16),
                   jax.ShapeDtypeStruct((b, t, hv), F32)],
        compiler_params=_cparams("arbitrary", "arbitrary"),
        name="ret_proj",
    )(x, mod, ng, w, cos, sin)


def _ret_core_kernel(q_ref, qd_ref, k_ref, kd_ref, v_ref, gt_ref, x_ref, mod_ref, og_ref, wout_ref, r0_ref,
                     y_ref, rout_ref, r_ref, o_ref, *, chunk):
    @pl.when(pl.program_id(1) == 0)
    def _():
        r_ref[...] = r0_ref[0]

    rel = (_iota2((chunk, chunk), 0) - _iota2((chunk, chunk), 1)).astype(F32)
    for hd in range(RET_HEADS):
        lg = _ret_log_gamma(hd)
        k0 = hd * RET_DK
        v0 = hd * RET_DV
        dmask = jnp.where(rel >= 0, jnp.exp(lg * jnp.maximum(rel, 0.0)), 0.0)
        vh = v_ref[0, :, v0:v0 + RET_DV]
        s = _dot_nt_bf(q_ref[0, :, k0:k0 + RET_DK], k_ref[0, :, k0:k0 + RET_DK]) * dmask
        r = r_ref[hd]
        o = (jnp.dot(s.astype(BF16), vh, preferred_element_type=F32)
             + jnp.dot(qd_ref[0, :, k0:k0 + RET_DK], r.astype(BF16), preferred_element_type=F32))
        r_ref[hd] = r * math.exp(lg * chunk) + lax.dot_general(
            kd_ref[0, :, k0:k0 + RET_DK], vh, (((0,), (0,)), ((), ())), preferred_element_type=F32)
        mu = jnp.mean(o, axis=-1, keepdims=True)
        dlt = o - mu
        var = jnp.mean(dlt * dlt, axis=-1, keepdims=True)
        o = dlt * lax.rsqrt(var + NORM_EPS) * og_ref[:, v0:v0 + RET_DV]
        o_ref[:, v0:v0 + RET_DV] = (o * _silu(gt_ref[0, :, v0:v0 + RET_DV])).astype(BF16)
    out = jnp.dot(o_ref[...], wout_ref[...], preferred_element_type=F32)
    y_ref[0] = x_ref[0] + mod_ref[0, 2:3, :] * out
    rout_ref[0] = r_ref[...]


def _ret_core(q, qd, k, kd, v, gt, x, mod, og, wout, r0, chunk):
    b, t, d = x.shape
    hk = RET_HEADS * RET_DK
    hv = RET_HEADS * RET_DV
    kern = functools.partial(_ret_core_kernel, chunk=chunk)
    row = lambda i, j: (i, j, 0)
    st = pl.BlockSpec((1, RET_HEADS, RET_DK, RET_DV), lambda i, j: (i, 0, 0, 0))
    return pl.pallas_call(
        kern,
        grid=(b, t // chunk),
        in_specs=[pl.BlockSpec((1, chunk, hk), row)] * 4
                 + [pl.BlockSpec((1, chunk, hv), row)] * 2
                 + [pl.BlockSpec((1, chunk, d), row),
                    pl.BlockSpec((1, 8, d), lambda i, j: (i, 0, 0)),
                    pl.BlockSpec((1, hv), lambda i, j: (0, 0)),
                    _wspec((hv, d)),
                    st],
        out_specs=[pl.BlockSpec((1, chunk, d), row), st],
        out_shape=[jax.ShapeDtypeStruct((b, t, d), F32),
                   jax.ShapeDtypeStruct((b, RET_HEADS, RET_DK, RET_DV), F32)],
        scratch_shapes=[pltpu.VMEM((RET_HEADS, RET_DK, RET_DV), F32),
                        pltpu.VMEM((chunk, hv), BF16)],
        compiler_params=_cparams("arbitrary", "arbitrary"),
        name="ret_core",
    )(q, qd, k, kd, v, gt, x, mod, og, wout, r0)


def _run_group(x, mod_all, states, ffn_bufs, pos0, wts):
    b, t, d = x.shape
    if t >= ROW_TILE:
        nb, tt = 1, ROW_TILE
    else:
        nb, tt = b, t
    new_states, new_ffn = [], []
    for i in range(DEPTH):
        kind, j = i % 3, i // 3
        mod = mod_all[i]
        if kind == 0:
            s0, cbuf = states[i]
            gw = wts["gdn"][j]
            q, k, v, z, gb, ncbuf = _gdn_proj(x, mod, wts["norm_mix_g"][i], gw["wqkv"], gw["wz"], gw["wab"],
                                              gw["conv_w"], gw["pv"], cbuf, nb, tt)
            x, s_new = _gdn_core(q, k, v, z, gb, x, mod, gw["norm_g"], gw["wout"], s0,
                                 min(GDN_TILE, t), min(CHUNK, t))
            new_states.append((s_new, ncbuf))
        elif kind == 1:
            pk, pv = states[i]
            sw = wts["sb"][j]
            kf, vf, qb, kb, vb = _sb_proj(x, mod, wts["norm_mix_g"][i], sw["win"], sw["qkg"], nb, tt)
            o = _sb_attention(qb, kb, vb, pk, pv, min(SB_TILE, t))
            x = _out_proj(o, x, mod, sw["wout"], nb, tt)
            new_states.append((kf.reshape(b, t, SB_HEADS, SB_DH), vf.reshape(b, t, SB_HEADS, SB_DH)))
        else:
            (r0,) = states[i]
            rw = wts["ret"][j]
            chunk = min(RET_CHUNK, t)
            half = RET_DK // 2
            inv_freq = RET_ROPE_BASE ** (-jnp.arange(half, dtype=F32) / half)
            ang = (pos0 + jnp.arange(t)).astype(F32)[:, None] * inv_freq[None, :]
            q, qd, k, kd, v, gt = _ret_proj(x, mod, wts["norm_mix_g"][i], rw["win"], jnp.cos(ang), jnp.sin(ang),
                                            nb, tt, chunk)
            x, r_new = _ret_core(q, qd, k, kd, v, gt, x, mod, rw["norm_g"], rw["wout"], r0, chunk)
            new_states.append((r_new,))
        fw = wts["ffn"][i]
        x, fbuf = _ffn(x, mod, wts["norm_ffn_g"][i], fw["win"], fw["conv_w"], fw["conv_b"], fw["wout"],
                       ffn_bufs[i], nb, tt)
        new_ffn.append(fbuf)
    return x, new_states, jnp.stack(new_ffn)


def kernel(x_prompt, x_sample, state_l0_gdn_S, state_l0_gdn_conv, cache_l1_sb_k, cache_l1_sb_v, state_l2_ret, state_l3_gdn_S, state_l3_gdn_conv, state_ffn_conv, c_prompt, c_sample, ada_w, ada_b, norm_mix_g, norm_ffn_g, gdn_w_in, gdn_conv_w, gdn_A_log, gdn_dt_bias, gdn_norm_g, gdn_w_out, sb_w_in, sb_q_norm_g, sb_k_norm_g, sb_w_out, ret_w_in, ret_norm_g, ret_w_out, ffn_w_in, ffn_conv_w, ffn_conv_b, ffn_w_out):
    d = D_MODEL
    bp, tp, _ = x_prompt.shape
    bs, ts, _ = x_sample.shape

    hv = GDN_HEADS * GDN_DK
    wts = {"norm_mix_g": [norm_mix_g[i].reshape(1, d) for i in range(DEPTH)],
           "norm_ffn_g": [norm_ffn_g[i].reshape(1, d) for i in range(DEPTH)],
           "gdn": [], "sb": [], "ret": [], "ffn": []}
    for j in range(gdn_w_in.shape[0]):
        w = gdn_w_in[j]
        wab = jnp.pad(w[:, GDN_QKV + hv:], ((0, 0), (0, LANES - 2 * GDN_HEADS)))
        pv = jnp.zeros((8, LANES), F32)
        pv = pv.at[0, :GDN_HEADS].set(-jnp.exp(gdn_A_log[j].astype(F32)))
        pv = pv.at[1, :GDN_HEADS].set(gdn_dt_bias[j].astype(F32))
        wts["gdn"].append({"wqkv": w[:, :GDN_QKV].astype(BF16),
                           "wz": w[:, GDN_QKV:GDN_QKV + hv].astype(BF16),
                           "wab": wab.astype(BF16),
                           "conv_w": gdn_conv_w[j], "pv": pv,
                           "norm_g": gdn_norm_g[j].reshape(1, GDN_DK),
                           "wout": gdn_w_out[j].astype(BF16)})
    for j in range(sb_w_in.shape[0]):
        qkg = jnp.zeros((8, d), F32)
        qkg = qkg.at[0].set(jnp.tile(sb_q_norm_g[j], SB_HEADS)).at[1].set(jnp.tile(sb_k_norm_g[j], SB_HEADS))
        wts["sb"].append({"win": sb_w_in[j].astype(BF16), "qkg": qkg, "wout": sb_w_out[j].astype(BF16)})
    for j in range(ret_w_in.shape[0]):
        wts["ret"].append({"win": ret_w_in[j].astype(BF16),
                           "norm_g": ret_norm_g[j].reshape(1, RET_HEADS * RET_DV),
                           "wout": ret_w_out[j].astype(BF16)})
    for i in range(DEPTH):
        wts["ffn"].append({"win": ffn_w_in[i].astype(BF16), "conv_w": ffn_conv_w[i],
                           "conv_b": ffn_conv_b[i].reshape(1, 2 * D_FF), "wout": ffn_w_out[i].astype(BF16)})

    nrow = bp + bs
    rows = -(-nrow // 8) * 8
    c_all = jnp.pad(jnp.concatenate([c_prompt, c_sample], axis=0), ((0, rows - nrow), (0, 0)))
    mod = _ada_mod(c_all, ada_w, ada_b)
    mod = jnp.pad(mod.reshape(DEPTH, rows, 6, d), ((0, 0), (0, 0), (0, 2), (0, 0)))
    mod_p, mod_s = mod[:, :bp], mod[:, bp:nrow]

    dt = x_prompt.dtype
    zero_states = [(jnp.zeros((bp, GDN_HEADS, GDN_DK, GDN_DK), dt), jnp.zeros((bp, 3, GDN_QKV), dt)),
                   (None, None),
                   (jnp.zeros((bp, RET_HEADS, RET_DK, RET_DV), dt),),
                   (jnp.zeros((bp, GDN_HEADS, GDN_DK, GDN_DK), dt), jnp.zeros((bp, 3, GDN_QKV), dt))]
    zero_ffn = jnp.zeros((DEPTH, bp, 2, 2 * D_FF), dt)
    y_prompt, p_states, p_ffn_conv = _run_group(x_prompt, mod_p, zero_states, zero_ffn, 0, wts)

    past_len = cache_l1_sb_k.shape[1]
    sample_states = [(state_l0_gdn_S, state_l0_gdn_conv),
                     (cache_l1_sb_k, cache_l1_sb_v),
                     (state_l2_ret,), (state_l3_gdn_S, state_l3_gdn_conv)]
    y_sample, s_states, s_ffn_conv = _run_group(x_sample, mod_s, sample_states, state_ffn_conv, past_len, wts)

    (p_l0_S, p_l0_conv), (p_l1_k, p_l1_v), (p_l2_R,), (p_l3_S, p_l3_conv) = p_states
    (s_l0_S, s_l0_conv), (s_l1_k, s_l1_v), (s_l2_R,), (s_l3_S, s_l3_conv) = s_states
    return (y_prompt, y_sample,
            p_l0_S, p_l0_conv, p_l1_k, p_l1_v, p_l2_R, p_l3_S, p_l3_conv, p_ffn_conv,
            s_l0_S, s_l0_conv, s_l1_k, s_l1_v, s_l2_R, s_l3_S, s_l3_conv, s_ffn_conv)
```

```python
import functools
import math

import jax
import jax.numpy as jnp
from jax import lax
from jax.experimental import pallas as pl
from jax.experimental.pallas import tpu as pltpu

F32 = jnp.float32
BF16 = jnp.bfloat16

D_MODEL = 1024
DEPTH = 4
CHUNK = 64
GDN_HEADS = 8
GDN_DK = 128
GDN_QKV = 3 * GDN_HEADS * GDN_DK
GDN_CONV_TAPS = 4
GDN_AHEAD = 12
CONV_STRIDE = 4
SB_HEADS = 16
SB_DH = 64
RET_HEADS = 4
RET_DK = 256
RET_DV = 512
RET_ROPE_BASE = 10000.0
D_FF = 2816
NORM_EPS = 1e-6

LANES = 128
TILE_ROW0 = 8
MXU_N = 256
VMEM_LIMIT = 56 * 1024 * 1024

ROW_TILE = 512
GDN_TILE = 128
RET_CHUNK = 256
SB_TILE = 256
SB_DEAD = 105.0
SB_PAIRS = 4


def _cparams(*sem):
    return pltpu.CompilerParams(dimension_semantics=sem, vmem_limit_bytes=VMEM_LIMIT)


def _wspec(shape):
    return pl.BlockSpec(shape, lambda i, j: (0,) * len(shape), pipeline_mode=pl.Buffered(1))


def _dot(a, b):
    return jnp.dot(a.astype(BF16), b.astype(BF16), preferred_element_type=F32)


def _dot_nt(a, b):
    return lax.dot_general(a.astype(BF16), b.astype(BF16), (((1,), (1,)), ((), ())),
                           preferred_element_type=F32)


def _dot_tn(a, b):
    return lax.dot_general(a.astype(BF16), b.astype(BF16), (((0,), (0,)), ((), ())),
                           preferred_element_type=F32)


def _split(x):
    hi = x.astype(BF16)
    lo = (x - hi.astype(F32)).astype(BF16)
    return hi, lo


def _dot_sel(a, sel):
    hi, lo = _split(a)
    return (jnp.dot(hi, sel, preferred_element_type=F32)
            + jnp.dot(lo, sel, preferred_element_type=F32))


def _sel_dot(sel, b):
    hi, lo = _split(b)
    return (jnp.dot(sel, hi, preferred_element_type=F32)
            + jnp.dot(sel, lo, preferred_element_type=F32))


def _sigmoid(x):
    return 1.0 / (1.0 + jnp.exp(-x))


def _silu(x):
    return x * _sigmoid(x)


def _softplus(x):
    return jnp.maximum(x, 0.0) + jnp.log(1.0 + jnp.exp(-jnp.abs(x)))


def _norm_mod(x, gain, scale, shift):
    ms = jnp.mean(x * x, axis=-1, keepdims=True)
    y = x * lax.rsqrt(ms + NORM_EPS) * gain
    return y * (1.0 + scale) + shift


def _iota2(shape, dim):
    return lax.broadcasted_iota(jnp.int32, shape, dim)


def _ada_kernel(c_ref, w_ref, b_ref, o_ref):
    o_ref[0] = _dot(_silu(c_ref[...]), w_ref[0]) + b_ref[0]


def _ada_mod(c_all, ada_w, ada_b):
    rows = c_all.shape[0]
    n = ada_w.shape[2]
    tn = 1536
    return pl.pallas_call(
        _ada_kernel,
        grid=(DEPTH, n // tn),
        in_specs=[pl.BlockSpec((rows, D_MODEL), lambda i, j: (0, 0)),
                  pl.BlockSpec((1, D_MODEL, tn), lambda i, j: (i, 0, j)),
                  pl.BlockSpec((1, 1, tn), lambda i, j: (i, 0, j))],
        out_specs=pl.BlockSpec((1, rows, tn), lambda i, j: (i, 0, j)),
        out_shape=jax.ShapeDtypeStruct((DEPTH, rows, n), F32),
        compiler_params=_cparams("arbitrary", "arbitrary"),
        name="ada_mod",
    )(c_all, ada_w, ada_b.reshape(DEPTH, 1, n))


def _slab(sl):
    return slice(sl * LANES, (sl + 1) * LANES)


def _conv_class(ext_ref, sl, a, w_ref, width, n4):
    hw = width - 1
    y = ext_ref[sl, :, pl.ds(TILE_ROW0 + a, n4, stride=CONV_STRIDE), :] * w_ref[hw:hw + 1, _slab(sl)]
    for j in range(hw):
        y = y + ext_ref[sl, :, pl.ds(TILE_ROW0 + a - hw + j, n4, stride=CONV_STRIDE), :] * w_ref[j:j + 1, _slab(sl)]
    return y


FFN_COLS = 256
FFN_CONV_TAPS = 3


def _ffn_kernel(x_ref, mod_ref, ng_ref, win_ref, cw_ref, cb_ref, wout_ref, buf_ref,
                y_ref, nbuf_ref, ext_ref, *, nb, tt):
    m = nb * tt
    hw = FFN_CONV_TAPS - 1
    nchunk = D_FF // FFN_COLS
    tile = slice(TILE_ROW0, TILE_ROW0 + tt)
    carried = slice(TILE_ROW0 - hw, TILE_ROW0)

    @pl.when(pl.program_id(1) == 0)
    def _():
        ext_ref[:, carried, :] = buf_ref[...]

    def conv(cols):
        y = ext_ref[:, tile, cols] * cw_ref[hw:hw + 1, cols]
        for j in range(hw):
            y = y + ext_ref[:, TILE_ROW0 - hw + j:TILE_ROW0 - hw + j + tt, cols] * cw_ref[j:j + 1, cols]
        return y + cb_ref[:, cols]

    x = x_ref[...]
    h = _norm_mod(x, ng_ref[...], mod_ref[:, 4:5, :], mod_ref[:, 3:4, :])
    h = h.reshape(m, D_MODEL).astype(BF16)
    acc = jnp.zeros((m, D_MODEL), F32)

    for c in range(nchunk):
        for col0 in (c * FFN_COLS, D_FF + c * FFN_COLS):
            u = jnp.dot(h, win_ref[:, col0:col0 + FFN_COLS], preferred_element_type=F32)
            ext_ref[:, tile, col0:col0 + FFN_COLS] = u.reshape(nb, tt, FFN_COLS)
    for c in range(nchunk):
        gcols = slice(c * FFN_COLS, (c + 1) * FFN_COLS)
        vcols = slice(D_FF + c * FFN_COLS, D_FF + (c + 1) * FFN_COLS)
        a = (_silu(conv(gcols)) * conv(vcols)).reshape(m, FFN_COLS).astype(BF16)
        acc = acc + jnp.dot(a, wout_ref[gcols, :], preferred_element_type=F32)
    y_ref[...] = x + mod_ref[:, 5:6, :] * acc.reshape(nb, tt, D_MODEL)
    nbuf_ref[...] = ext_ref[:, TILE_ROW0 + tt - hw:TILE_ROW0 + tt, :]
    ext_ref[:, carried, :] = ext_ref[:, TILE_ROW0 + tt - hw:TILE_ROW0 + tt, :]


def _ffn(x, mod, ng, win, cw, cb, wout, buf, nb, tt):
    b, t, d = x.shape
    n2 = 2 * D_FF
    kern = functools.partial(_ffn_kernel, nb=nb, tt=tt)
    return pl.pallas_call(
        kern,
        grid=(b // nb, t // tt),
        in_specs=[pl.BlockSpec((nb, tt, d), lambda i, j: (i, j, 0)),
                  pl.BlockSpec((nb, 8, d), lambda i, j: (i, 0, 0)),
                  pl.BlockSpec((1, d), lambda i, j: (0, 0)),
                  _wspec((d, n2)),
                  pl.BlockSpec((3, n2), lambda i, j: (0, 0)),
                  pl.BlockSpec((1, n2), lambda i, j: (0, 0)),
                  _wspec((D_FF, d)),
                  pl.BlockSpec((nb, 2, n2), lambda i, j: (i, 0, 0))],
        out_specs=[pl.BlockSpec((nb, tt, d), lambda i, j: (i, j, 0)),
                   pl.BlockSpec((nb, 2, n2), lambda i, j: (i, 0, 0))],
        out_shape=[jax.ShapeDtypeStruct((b, t, d), F32),
                   jax.ShapeDtypeStruct((b, 2, n2), F32)],
        scratch_shapes=[pltpu.VMEM((nb, TILE_ROW0 + tt, n2), F32)],
        compiler_params=_cparams("arbitrary", "arbitrary"),
        name="conv_ffn",
    )(x, mod, ng, win, cw, cb, wout, buf)


def _gdn_proj_kernel(x_ref, mod_ref, ng_ref, wqkv_ref, wz_ref, wab_ref, cw_ref, pv_ref, buf_ref,
                     q_ref, k_ref, v_ref, z_ref, gb_ref, nbuf_ref, ext_ref, *, nb, tt):
    m = nb * tt
    d = D_MODEL
    nslab = GDN_QKV // LANES
    hw = GDN_CONV_TAPS - 1
    slab = _slab

    @pl.when(pl.program_id(1) == 0)
    def _():
        for sl in range(nslab):
            ext_ref[sl, :, TILE_ROW0 - hw:TILE_ROW0, :] = buf_ref[:, :, slab(sl)]

    h = _norm_mod(x_ref[...], ng_ref[...], mod_ref[:, 1:2, :], mod_ref[:, 0:1, :])
    h = h.reshape(m, d).astype(BF16)
    nslice = GDN_QKV // MXU_N
    per = MXU_N // LANES

    def project(s):
        c0 = s * MXU_N
        u = jnp.dot(h, wqkv_ref[:, c0:c0 + MXU_N], preferred_element_type=F32).reshape(nb, tt, MXU_N)
        for half in range(per):
            ext_ref[s * per + half, :, TILE_ROW0:TILE_ROW0 + tt, :] = u[:, :, slab(half)]

    for s in range(min(GDN_AHEAD, nslice)):
        project(s)
    ab = jnp.dot(h, wab_ref[...], preferred_element_type=F32)
    lane = _iota2(ab.shape, 1)
    gb = jnp.where(lane < GDN_HEADS,
                   pv_ref[0:1, :] * _softplus(ab + pv_ref[1:2, :]),
                   _sigmoid(ab))
    gb_ref[...] = gb.reshape(nb, tt, LANES)
    outs = (q_ref, k_ref, v_ref)
    n4 = tt // CONV_STRIDE
    for sl in range(nslab):
        if sl % per == 0 and sl // per + GDN_AHEAD < nslice:
            project(sl // per + GDN_AHEAD)
        if sl == nslab - per:
            z_ref[...] = jnp.dot(h, wz_ref[...], preferred_element_type=F32).reshape(nb, tt, d)
        which, hd = divmod(sl, GDN_HEADS)
        for a in range(CONV_STRIDE):
            y = _silu(_conv_class(ext_ref, sl, a, cw_ref, GDN_CONV_TAPS, n4))
            if which < 2:
                r = lax.rsqrt(jnp.sum(y * y, axis=-1, keepdims=True) + NORM_EPS)
                if which == 0:
                    r = r * (GDN_DK ** -0.5)
                y = y * r
            outs[which][:, hd, pl.ds(a, n4, stride=CONV_STRIDE), :] = y
        nbuf_ref[:, :, slab(sl)] = ext_ref[sl, :, TILE_ROW0 + tt - hw:TILE_ROW0 + tt, :]
        ext_ref[sl, :, TILE_ROW0 - hw:TILE_ROW0, :] = ext_ref[sl, :, TILE_ROW0 + tt - hw:TILE_ROW0 + tt, :]


def _gdn_proj(x, mod, ng, wqkv, wz, wab, cw, pv, buf, nb, tt):
    b, t, d = x.shape
    kern = functools.partial(_gdn_proj_kernel, nb=nb, tt=tt)
    row = lambda i, j: (i, j, 0)
    const2 = lambda i, j: (0, 0)
    act = jax.ShapeDtypeStruct((b, t, d), F32)
    heads = jax.ShapeDtypeStruct((b, GDN_HEADS, t, GDN_DK), F32)
    return pl.pallas_call(
        kern,
        grid=(b // nb, t // tt),
        in_specs=[pl.BlockSpec((nb, tt, d), row),
                  pl.BlockSpec((nb, 8, d), lambda i, j: (i, 0, 0)),
                  pl.BlockSpec((1, d), const2),
                  _wspec((d, GDN_QKV)),
                  _wspec((d, d)),
                  pl.BlockSpec((d, LANES), const2),
                  pl.BlockSpec((4, GDN_QKV), const2),
                  pl.BlockSpec((8, LANES), const2),
                  pl.BlockSpec((nb, 3, GDN_QKV), lambda i, j: (i, 0, 0))],
        out_specs=[pl.BlockSpec((nb, GDN_HEADS, tt, GDN_DK), lambda i, j: (i, 0, j, 0))] * 3
                  + [pl.BlockSpec((nb, tt, d), row),
                     pl.BlockSpec((nb, tt, LANES), row),
                     pl.BlockSpec((nb, 3, GDN_QKV), lambda i, j: (i, 0, 0))],
        out_shape=[heads, heads, heads, act,
                   jax.ShapeDtypeStruct((b, t, LANES), F32),
                   jax.ShapeDtypeStruct((b, 3, GDN_QKV), F32)],
        scratch_shapes=[pltpu.VMEM((GDN_QKV // LANES, nb, TILE_ROW0 + tt, LANES), F32)],
        compiler_params=_cparams("arbitrary", "arbitrary"),
        name="gdn_proj",
    )(x, mod, ng, wqkv, wz, wab, cw, pv, buf)


def _tri_inverses(lmats, chunk):
    n = lmats[0].shape[0]
    row = _iota2((n, n), 0)
    col = _iota2((n, n), 1)
    level = 31 - lax.clz(jnp.bitwise_xor(row, col))
    eye = (row == col).astype(F32)
    xs = [eye - jnp.where(level == 0, lm, 0.0) for lm in lmats]
    k = 1
    while (1 << k) < chunk:
        xbs = [x.astype(BF16) for x in xs]
        ps = [_dot(xb, jnp.where(level == k, lm, 0.0)).astype(BF16) for xb, lm in zip(xbs, lmats)]
        xs = [x - _dot(p, xb) for x, p, xb in zip(xs, ps, xbs)]
        k += 1
    return xs


def _gdn_core_kernel(q_ref, k_ref, v_ref, z_ref, gb_ref, x_ref, mod_ref, og_ref, wout_ref, s0_ref,
                     y_ref, sout_ref, s_ref, o_ref, *, tt, chunk):
    @pl.when(pl.program_id(1) == 0)
    def _():
        s_ref[...] = s0_ref[0]

    nck = tt // chunk
    row = _iota2((tt, tt), 0)
    col = _iota2((tt, tt), 1)
    if nck > 1:
        same = (row // chunk) == (col // chunk)
        incl = (row >= col) & same
        strict = (row > col) & same
    else:
        same = None
        incl = row >= col
        strict = row > col
    gb = gb_ref[0]
    gcol = _sel_dot(incl.astype(BF16), gb)
    if nck > 1:
        glast = _sel_dot(same.astype(BF16), gb)
    else:
        glast = jnp.broadcast_to(jnp.sum(gb, axis=0, keepdims=True), gb.shape)
    grow = gcol.T
    eg = jnp.exp(gcol)
    ekd = jnp.exp(glast - gcol)
    egl = jnp.exp(glast)

    heads = range(GDN_HEADS)
    cols = [slice(hd * GDN_DK, (hd + 1) * GDN_DK) for hd in heads]
    kbs, lmats, qks = [], [], []
    for hd in heads:
        kh = k_ref[0, hd]
        diff = gcol[:, hd:hd + 1] - grow[hd:hd + 1, :]
        decay = jnp.exp(jnp.where(incl, diff, -jnp.inf))
        kb = kh * gb[:, GDN_HEADS + hd:GDN_HEADS + hd + 1]
        kbs.append(kb)
        lmats.append(jnp.where(strict, _dot_nt(kb, kh) * decay, 0.0))
        qks.append((_dot_nt(q_ref[0, hd], kh) * decay).astype(BF16))
    tinvs = [t.astype(BF16) for t in _tri_inverses(lmats, chunk)]
    us, ws = [], []
    for hd in heads:
        beta = gb[:, GDN_HEADS + hd:GDN_HEADS + hd + 1]
        uw = _dot(tinvs[hd], jnp.concatenate([v_ref[0, hd] * beta, kbs[hd] * eg[:, hd:hd + 1]], axis=1))
        us.append(uw[:, :GDN_DK])
        ws.append(uw[:, GDN_DK:].astype(BF16))
    ss = [s_ref[hd] for hd in heads]
    vnews = [[] for _ in heads]
    ocross = [[] for _ in heads]
    for c in range(nck):
        rows = slice(c * chunk, (c + 1) * chunk)
        for hd in heads:
            qd = (q_ref[0, hd, rows, :] * eg[rows, hd:hd + 1]).astype(BF16)
            ws_qs = _dot(jnp.concatenate([ws[hd][rows], qd], axis=0), ss[hd])
            vn = us[hd][rows] - ws_qs[:chunk]
            ocross[hd].append(ws_qs[chunk:])
            kd = k_ref[0, hd, rows, :] * ekd[rows, hd:hd + 1]
            ss[hd] = ss[hd] * egl[c * chunk:c * chunk + 1, hd:hd + 1] + _dot_tn(kd, vn)
            vnews[hd].append(vn.astype(BF16))
    for hd in heads:
        s_ref[hd] = ss[hd]
        vnew = jnp.concatenate(vnews[hd], axis=0) if nck > 1 else vnews[hd][0]
        oc = jnp.concatenate(ocross[hd], axis=0) if nck > 1 else ocross[hd][0]
        o = oc + _dot(qks[hd], vnew)
        o = o * lax.rsqrt(jnp.mean(o * o, axis=-1, keepdims=True) + NORM_EPS) * og_ref[...]
        o_ref[:, cols[hd]] = (o * _silu(z_ref[0, :, cols[hd]])).astype(BF16)

    out = jnp.dot(o_ref[...], wout_ref[...], preferred_element_type=F32)
    y_ref[0] = x_ref[0] + mod_ref[0, 2:3, :] * out
    sout_ref[0] = s_ref[...]


def _gdn_core(q, k, v, z, gb, x, mod, og, wout, s0, tt, chunk):
    b, t, d = x.shape
    kern = functools.partial(_gdn_core_kernel, tt=tt, chunk=chunk)
    row = lambda i, j: (i, j, 0)
    return pl.pallas_call(
        kern,
        grid=(b, t // tt),
        in_specs=[pl.BlockSpec((1, GDN_HEADS, tt, GDN_DK), lambda i, j: (i, 0, j, 0))] * 3
                 + [pl.BlockSpec((1, tt, d), row),
                    pl.BlockSpec((1, tt, LANES), row),
                    pl.BlockSpec((1, tt, d), row),
                    pl.BlockSpec((1, 8, d), lambda i, j: (i, 0, 0)),
                    pl.BlockSpec((1, GDN_DK), lambda i, j: (0, 0)),
                    _wspec((d, d)),
                    pl.BlockSpec((1, GDN_HEADS, GDN_DK, GDN_DK), lambda i, j: (i, 0, 0, 0))],
        out_specs=[pl.BlockSpec((1, tt, d), row),
                   pl.BlockSpec((1, GDN_HEADS, GDN_DK, GDN_DK), lambda i, j: (i, 0, 0, 0))],
        out_shape=[jax.ShapeDtypeStruct((b, t, d), F32),
                   jax.ShapeDtypeStruct((b, GDN_HEADS, GDN_DK, GDN_DK), F32)],
        scratch_shapes=[pltpu.VMEM((GDN_HEADS, GDN_DK, GDN_DK), F32),
                        pltpu.VMEM((tt, d), BF16)],
        compiler_params=_cparams("arbitrary", "arbitrary"),
        name="gdn_core",
    )(q, k, v, z, gb, x, mod, og, wout, s0)


def _sb_proj_kernel(x_ref, mod_ref, ng_ref, w_ref, qkg_ref,
                    kf_ref, vf_ref, qb_ref, kb_ref, vb_ref, u_ref, *, nb, tt):
    m = nb * tt
    d = D_MODEL
    h = _norm_mod(x_ref[...], ng_ref[...], mod_ref[:, 1:2, :], mod_ref[:, 0:1, :])
    h = h.reshape(m, d).astype(BF16)
    grp = ((_iota2((MXU_N, MXU_N), 0) // SB_DH) == (_iota2((MXU_N, MXU_N), 1) // SB_DH)).astype(BF16)
    for s in range(3 * d // MXU_N):
        c0 = s * MXU_N
        u_ref[:, c0:c0 + MXU_N] = jnp.dot(h, w_ref[:, c0:c0 + MXU_N], preferred_element_type=F32)
    for s in range(3 * d // MXU_N):
        c0 = s * MXU_N
        which, o0 = divmod(c0, d)
        u = u_ref[:, c0:c0 + MXU_N]
        if which < 2:
            ms = _dot_sel(u * u, grp) * (1.0 / SB_DH)
            u = u * lax.rsqrt(ms + NORM_EPS) * qkg_ref[which:which + 1, o0:o0 + MXU_N]
        u3 = u.reshape(nb, tt, MXU_N)
        if which == 0:
            qb_ref[:, :, o0:o0 + MXU_N] = u3.astype(BF16)
        elif which == 1:
            kf_ref[:, :, o0:o0 + MXU_N] = u3
            kb_ref[:, :, o0:o0 + MXU_N] = u3.astype(BF16)
        else:
            vf_ref[:, :, o0:o0 + MXU_N] = u3
            vb_ref[:, :, o0:o0 + MXU_N] = u3.astype(BF16)


def _sb_proj(x, mod, ng, w, qkg, nb, tt):
    b, t, d = x.shape
    kern = functools.partial(_sb_proj_kernel, nb=nb, tt=tt)
    row = lambda i, j: (i, j, 0)
    const2 = lambda i, j: (0, 0)
    f = jax.ShapeDtypeStruct((b, t, d), F32)
    h = jax.ShapeDtypeStruct((b, t, d), BF16)
    return pl.pallas_call(
        kern,
        grid=(b // nb, t // tt),
        in_specs=[pl.BlockSpec((nb, tt, d), row),
                  pl.BlockSpec((nb, 8, d), lambda i, j: (i, 0, 0)),
                  pl.BlockSpec((1, d), const2),
                  _wspec((d, 3 * d)),
                  pl.BlockSpec((8, d), const2)],
        out_specs=[pl.BlockSpec((nb, tt, d), row)] * 5,
        out_shape=[f, f, h, h, h],
        scratch_shapes=[pltpu.VMEM((nb * tt, 3 * d), F32)],
        compiler_params=_cparams("arbitrary", "arbitrary"),
        name="sb_proj",
    )(x, mod, ng, w, qkg)


def _suffix_selector(bw):
    r = jnp.bitwise_and(_iota2((2 * bw, bw + LANES), 0), bw - 1)
    c = _iota2((2 * bw, bw + LANES), 1)
    return jnp.where(c < bw, (r > c).astype(F32), 1.0).astype(BF16)


def _dot_nt_bf(a, b):
    return lax.dot_general(a, b, (((1,), (1,)), ((), ())), preferred_element_type=F32)


def _sb_tiles(qhs, kts, vts, carries, accs, mask, sel):
    heads = range(len(qhs))
    tq = qhs[0].shape[0]
    n = kts[0].shape[0]
    bw = min(LANES, n)
    zs = [_dot_nt_bf(qhs[hd], kts[hd]) for hd in heads]
    sps = [jnp.maximum(z, 0.0) + jnp.log(1.0 + jnp.exp(-jnp.abs(z))) for z in zs]
    spms = sps if mask is None else [jnp.where(mask, sp, 0.0) for sp in sps]
    carries = list(carries)
    accs = list(accs)
    ws = [[] for _ in heads]
    for mb in reversed(range(n // bw)):
        sl = slice(mb * bw, (mb + 1) * bw)
        his = [spm[:, sl].astype(BF16) for spm in spms]
        los = [(spm[:, sl] - hi.astype(F32)).astype(BF16) for spm, hi in zip(spms, his)]
        lhs = jnp.concatenate([jnp.concatenate([hi, lo], axis=1) for hi, lo in zip(his, los)], axis=0)
        r_all = jnp.dot(lhs, sel[bw], preferred_element_type=F32)
        for hd in heads:
            r = r_all[hd * tq:(hd + 1) * tq]
            w = jnp.exp(zs[hd][:, sl] - sps[hd][:, sl] - carries[hd][:, :bw] - r[:, :bw])
            if mask is not None:
                w = jnp.where(mask[:, sl], w, 0.0)
            ws[hd].append(w.astype(BF16))
            carries[hd] = carries[hd] + r[:, bw:]
    for hd in heads:
        wts = jnp.concatenate(ws[hd][::-1], axis=-1) if len(ws[hd]) > 1 else ws[hd][0]
        accs[hd] = accs[hd] + jnp.dot(wts, vts[hd], preferred_element_type=F32)
    return carries, accs


def _sb_head_queries(q):
    q = q * (SB_DH ** -0.5)
    lane = _iota2(q.shape, 1)
    return [jnp.where((lane // SB_DH) == hd, q, jnp.zeros_like(q)) for hd in range(2)]


def _sb_fast_kernel(q_ref, kd_ref, vd_ref, kp_ref, vp_ref, o_ref, need_ref, *, tq, more_before_prev):
    tp = kp_ref.shape[1]
    npair = D_MODEL // LANES
    sel = {bw: _suffix_selector(bw) for bw in {min(LANES, tq), min(LANES, tp)}}
    dmask = _iota2((tq, tq), 1) < _iota2((tq, tq), 0)
    zero = jnp.zeros((tq, LANES), F32)
    pairs = [slice(hp * LANES, (hp + 1) * LANES) for hp in range(npair)]
    qhs = [qh for hp in range(npair) for qh in _sb_head_queries(q_ref[0, :, pairs[hp]])]
    heads = range(2 * npair)

    def tiles(k_ref, v_ref, carries, accs, mask):
        return _sb_tiles(qhs, [k_ref[0, :, pairs[hd // 2]].astype(BF16) for hd in heads],
                         [v_ref[0, :, pairs[hd // 2]].astype(BF16) for hd in heads], carries, accs, mask, sel)

    carries, accs = tiles(kd_ref, vd_ref, [zero for _ in heads], [zero for _ in heads], dmask)
    carries, accs = tiles(kp_ref, vp_ref, carries, accs, None)

    lane = _iota2((tq, LANES), 1)
    cmin = None
    for hp in range(npair):
        o_ref[0, :, pairs[hp]] = jnp.where((lane // SB_DH) == 0, accs[2 * hp], accs[2 * hp + 1]).astype(BF16)
        c = jnp.minimum(carries[2 * hp], carries[2 * hp + 1])
        cmin = c if cmin is None else jnp.minimum(cmin, c)
    alive = jnp.where(jnp.min(cmin, axis=0, keepdims=True) < SB_DEAD, 1.0, 0.0)
    if not more_before_prev:
        alive = jnp.zeros_like(alive)
    need_ref[0, 0] = jnp.broadcast_to(alive, (8, LANES))


def _sb_fast(qb, kb, vb, prev_k, prev_v, tq, more_before_prev):
    b, t, d = qb.shape
    tp = prev_k.shape[1]
    kern = functools.partial(_sb_fast_kernel, tq=tq, more_before_prev=more_before_prev)
    cur = pl.BlockSpec((1, tq, d), lambda i, j: (i, j, 0))
    prev = pl.BlockSpec((1, tp, d), lambda i, j: (i, 0, 0))
    return pl.pallas_call(
        kern,
        grid=(b, t // tq),
        in_specs=[cur, cur, cur, prev, prev],
        out_specs=[cur, pl.BlockSpec((1, 1, 8, LANES), lambda i, j: (i, j, 0, 0))],
        out_shape=[jax.ShapeDtypeStruct((b, t, d), BF16),
                   jax.ShapeDtypeStruct((b, t // tq, 8, LANES), F32)],
        compiler_params=_cparams("arbitrary", "arbitrary"),
        name="sb_fast",
    )(qb, kb, vb, prev_k, prev_v)


def _sb_attention(qb, kb, vb, past_k, past_v, tq):
    b, t, d = qb.shape
    if past_k is None:
        return _sb_attn(qb, kb, vb, None, None, tq)
    assert t == tq
    plen = past_k.shape[1]
    tp = min(256, plen)
    tail = lambda c: c[:, plen - tp:].reshape(b, tp, d)
    o, need = _sb_fast(qb, kb, vb, tail(past_k), tail(past_v), tq, plen > tp)
    full = lambda: _sb_attn(qb, kb, vb, past_k.reshape(b, plen, d), past_v.reshape(b, plen, d), tq)
    return lax.cond(jnp.max(need) > 0.0, full, lambda: o)


def _sb_attn_kernel(*refs, tq, tk, past_len):
    if past_len:
        q_ref, k_ref, v_ref, pk_ref, pv_ref, o_ref = refs
    else:
        q_ref, k_ref, v_ref, o_ref = refs
        pk_ref = pv_ref = None
    qi = pl.program_id(2)
    lane = _iota2((tq, LANES), 1)
    pairs = [slice(p * LANES, (p + 1) * LANES) for p in range(SB_PAIRS)]
    qhs = [qh for p in range(SB_PAIRS) for qh in _sb_head_queries(q_ref[0, :, pairs[p]])]
    ts = tq
    sel = {bw: _suffix_selector(bw) for bw in {min(LANES, ts), min(LANES, tk)}}

    def both(kt, vt, state, mask):
        nh = 2 * SB_PAIRS
        carries, accs = _sb_tiles(qhs, [kt[:, pairs[hd // 2]] for hd in range(nh)],
                                  [vt[:, pairs[hd // 2]] for hd in range(nh)],
                                  state[0::2], state[1::2], mask, sel)
        return tuple(x for ca in zip(carries, accs) for x in ca)

    r0 = pl.multiple_of(qi * tq, tq)
    zero = jnp.zeros((tq, LANES), F32)
    dmask = _iota2((tq, tq), 1) < _iota2((tq, tq), 0)
    state = both(k_ref[0, pl.ds(r0, tq), :], v_ref[0, pl.ds(r0, tq), :], (zero,) * (4 * SB_PAIRS), dmask)

    def alive(st):
        cmin = st[0]
        for hd in range(1, 2 * SB_PAIRS):
            cmin = jnp.minimum(cmin, st[2 * hd])
        return jnp.min(cmin) < SB_DEAD

    def key_loop(ntiles, size, load, st):
        def cond(c):
            return jnp.logical_and(c[0] < ntiles, c[1])

        def body(c):
            kt, vt = load(pl.multiple_of((ntiles - 1 - c[0]) * size, size))
            nst = both(kt, vt, c[2:], None)
            return (c[0] + 1, alive(nst)) + nst

        return lax.while_loop(cond, body, (jnp.int32(0), alive(st)) + st)[2:]

    state = key_loop(qi * (tq // ts), ts,
                     lambda k0: (k_ref[0, pl.ds(k0, ts), :], v_ref[0, pl.ds(k0, ts), :]), state)
    if past_len:
        state = key_loop(past_len // tk, tk,
                         lambda k0: (pk_ref[0, pl.ds(k0, tk), :].astype(BF16),
                                     pv_ref[0, pl.ds(k0, tk), :].astype(BF16)), state)
    for p in range(SB_PAIRS):
        o_ref[0, :, pairs[p]] = jnp.where((lane // SB_DH) == 0, state[4 * p + 1], state[4 * p + 3]).astype(BF16)


def _sb_attn(qb, kb, vb, past_k, past_v, tq):
    b, t, d = qb.shape
    past_len = 0 if past_k is None else past_k.shape[1]
    tk = tq if not past_len else min(256, past_len)
    kern = functools.partial(_sb_attn_kernel, tq=tq, tk=tk, past_len=past_len)
    width = SB_PAIRS * LANES
    qspec = pl.BlockSpec((1, tq, width), lambda i, h, j: (i, j, h))
    kvspec = pl.BlockSpec((1, t, width), lambda i, h, j: (i, 0, h), pipeline_mode=pl.Buffered(1))
    in_specs = [qspec, kvspec, kvspec]
    args = [qb, kb, vb]
    if past_len:
        pspec = pl.BlockSpec((1, past_len, width), lambda i, h, j: (i, 0, h))
        in_specs += [pspec, pspec]
        args += [past_k, past_v]
    return pl.pallas_call(
        kern,
        grid=(b, d // width, t // tq),
        in_specs=in_specs,
        out_specs=qspec,
        out_shape=jax.ShapeDtypeStruct((b, t, d), BF16),
        compiler_params=_cparams("arbitrary", "arbitrary", "arbitrary"),
        name="sb_attn",
    )(*args)


def _out_proj_kernel(o_ref, x_ref, mod_ref, w_ref, y_ref, *, nb, tt):
    out = jnp.dot(o_ref[...].reshape(nb * tt, -1), w_ref[...], preferred_element_type=F32)
    y_ref[...] = x_ref[...] + mod_ref[:, 2:3, :] * out.reshape(nb, tt, D_MODEL)


def _out_proj(o, x, mod, w, nb, tt):
    b, t, d = x.shape
    kin = o.shape[-1]
    kern = functools.partial(_out_proj_kernel, nb=nb, tt=tt)
    row = lambda i, j: (i, j, 0)
    return pl.pallas_call(
        kern,
        grid=(b // nb, t // tt),
        in_specs=[pl.BlockSpec((nb, tt, kin), row),
                  pl.BlockSpec((nb, tt, d), row),
                  pl.BlockSpec((nb, 8, d), lambda i, j: (i, 0, 0)),
                  _wspec((kin, d))],
        out_specs=pl.BlockSpec((nb, tt, d), row),
        out_shape=jax.ShapeDtypeStruct((b, t, d), F32),
        compiler_params=_cparams("arbitrary", "arbitrary"),
        name="out_proj",
    )(o, x, mod, w)


def _ret_log_gamma(hd):
    return math.log1p(-(2.0 ** (-5.0 - hd)))


def _ret_proj_kernel(x_ref, mod_ref, ng_ref, w_ref, cos_ref, sin_ref,
                     q_ref, qd_ref, k_ref, kd_ref, v_ref, gt_ref, *, nb, tt, chunk):
    m = nb * tt
    d = D_MODEL
    hk = RET_HEADS * RET_DK
    hv = RET_HEADS * RET_DV
    h = _norm_mod(x_ref[...], ng_ref[...], mod_ref[:, 1:2, :], mod_ref[:, 0:1, :])
    h = h.reshape(m, d).astype(BF16)
    cos = cos_ref[...][None]
    sin = sin_ref[...][None]
    tpos = pl.program_id(1) * tt + _iota2((1, tt, 1), 1)
    assert chunk & (chunk - 1) == 0
    idx = jnp.bitwise_and(tpos, chunk - 1).astype(F32)
    half = RET_DK // 2
    for s in range(2 * hk // MXU_N):
        c0 = s * MXU_N
        which, o0 = divmod(c0, hk)
        hd = o0 // RET_DK
        lg = _ret_log_gamma(hd)
        u = jnp.dot(h, w_ref[:, c0:c0 + MXU_N], preferred_element_type=F32).reshape(nb, tt, MXU_N)
        x1 = u[:, :, :half]
        x2 = u[:, :, half:]
        r = jnp.concatenate([x1 * cos - x2 * sin, x1 * sin + x2 * cos], axis=-1)
        if which == 0:
            q_ref[:, :, o0:o0 + MXU_N] = r.astype(BF16)
            qd_ref[:, :, o0:o0 + MXU_N] = (r * jnp.exp(lg * (idx + 1.0))).astype(BF16)
        else:
            r = r * (RET_DK ** -0.5)
            k_ref[:, :, o0:o0 + MXU_N] = r.astype(BF16)
            kd_ref[:, :, o0:o0 + MXU_N] = (r * jnp.exp(lg * (chunk - 1.0 - idx))).astype(BF16)
    for s in range(2 * hv // MXU_N):
        c0 = 2 * hk + s * MXU_N
        u = jnp.dot(h, w_ref[:, c0:c0 + MXU_N], preferred_element_type=F32).reshape(nb, tt, MXU_N)
        o0 = s * MXU_N
        if o0 < hv:
            v_ref[:, :, o0:o0 + MXU_N] = u.astype(BF16)
        else:
            gt_ref[:, :, o0 - hv:o0 - hv + MXU_N] = u


def _ret_proj(x, mod, ng, w, cos, sin, nb, tt, chunk):
    b, t, d = x.shape
    hk = RET_HEADS * RET_DK
    hv = RET_HEADS * RET_DV
    kern = functools.partial(_ret_proj_kernel, nb=nb, tt=tt, chunk=chunk)
    row = lambda i, j: (i, j, 0)
    const2 = lambda i, j: (0, 0)
    qs = jax.ShapeDtypeStruct((b, t, hk), BF16)
    return pl.pallas_call(
        kern,
        grid=(b // nb, t // tt),
        in_specs=[pl.BlockSpec((nb, tt, d), row),
                  pl.BlockSpec((nb, 8, d), lambda i, j: (i, 0, 0)),
                  pl.BlockSpec((1, d), const2),
                  _wspec((d, 2 * hk + 2 * hv)),
                  pl.BlockSpec((tt, RET_DK // 2), lambda i, j: (j, 0)),
                  pl.BlockSpec((tt, RET_DK // 2), lambda i, j: (j, 0))],
        out_specs=[pl.BlockSpec((nb, tt, hk), row)] * 4
                  + [pl.BlockSpec((nb, tt, hv), row)] * 2,
        out_shape=[qs, qs, qs, qs,
                   jax.ShapeDtypeStruct((b, t, hv), BF16),
                   jax.ShapeDtypeStruct((b, t, hv), F32)],
        compiler_params=_cparams("arbitrary", "arbitrary"),
        name="ret_proj",
    )(x, mod, ng, w, cos, sin)


def _ret_core_kernel(q_ref, qd_ref, k_ref, kd_ref, v_ref, gt_ref, x_ref, mod_ref, og_ref, wout_ref, r0_ref,
                     y_ref, rout_ref, r_ref, o_ref, *, chunk):
    @pl.when(pl.program_id(1) == 0)
    def _():
        r_ref[...] = r0_ref[0]

    rel = (_iota2((chunk, chunk), 0) - _iota2((chunk, chunk), 1)).astype(F32)
    for hd in range(RET_HEADS):
        lg = _ret_log_gamma(hd)
        k0 = hd * RET_DK
        v0 = hd * RET_DV
        dmask = jnp.where(rel >= 0, jnp.exp(lg * jnp.maximum(rel, 0.0)), 0.0)
        vh = v_ref[0, :, v0:v0 + RET_DV]
        s = _dot_nt_bf(q_ref[0, :, k0:k0 + RET_DK], k_ref[0, :, k0:k0 + RET_DK]) * dmask
        r = r_ref[hd]
        o = (jnp.dot(s.astype(BF16), vh, preferred_element_type=F32)
             + jnp.dot(qd_ref[0, :, k0:k0 + RET_DK], r.astype(BF16), preferred_element_type=F32))
        r_ref[hd] = r * math.exp(lg * chunk) + lax.dot_general(
            kd_ref[0, :, k0:k0 + RET_DK], vh, (((0,), (0,)), ((), ())), preferred_element_type=F32)
        mu = jnp.mean(o, axis=-1, keepdims=True)
        dlt = o - mu
        var = jnp.mean(dlt * dlt, axis=-1, keepdims=True)
        o = dlt * lax.rsqrt(var + NORM_EPS) * og_ref[:, v0:v0 + RET_DV]
        o_ref[:, v0:v0 + RET_DV] = (o * _silu(gt_ref[0, :, v0:v0 + RET_DV])).astype(BF16)
    out = jnp.dot(o_ref[...], wout_ref[...], preferred_element_type=F32)
    y_ref[0] = x_ref[0] + mod_ref[0, 2:3, :] * out
    rout_ref[0] = r_ref[...]


def _ret_core(q, qd, k, kd, v, gt, x, mod, og, wout, r0, chunk):
    b, t, d = x.shape
    hk = RET_HEADS * RET_DK
    hv = RET_HEADS * RET_DV
    kern = functools.partial(_ret_core_kernel, chunk=chunk)
    row = lambda i, j: (i, j, 0)
    st = pl.BlockSpec((1, RET_HEADS, RET_DK, RET_DV), lambda i, j: (i, 0, 0, 0))
    return pl.pallas_call(
        kern,
        grid=(b, t // chunk),
        in_specs=[pl.BlockSpec((1, chunk, hk), row)] * 4
                 + [pl.BlockSpec((1, chunk, hv), row)] * 2
                 + [pl.BlockSpec((1, chunk, d), row),
                    pl.BlockSpec((1, 8, d), lambda i, j: (i, 0, 0)),
                    pl.BlockSpec((1, hv), lambda i, j: (0, 0)),
                    _wspec((hv, d)),
                    st],
        out_specs=[pl.BlockSpec((1, chunk, d), row), st],
        out_shape=[jax.ShapeDtypeStruct((b, t, d), F32),
                   jax.ShapeDtypeStruct((b, RET_HEADS, RET_DK, RET_DV), F32)],
        scratch_shapes=[pltpu.VMEM((RET_HEADS, RET_DK, RET_DV), F32),
                        pltpu.VMEM((chunk, hv), BF16)],
        compiler_params=_cparams("arbitrary", "arbitrary"),
        name="ret_core",
    )(q, qd, k, kd, v, gt, x, mod, og, wout, r0)


def _run_group(x, mod_all, states, ffn_bufs, pos0, wts):
    b, t, d = x.shape
    if t >= ROW_TILE:
        nb, tt = 1, ROW_TILE
    else:
        nb, tt = b, t
    new_states, new_ffn = [], []
    for i in range(DEPTH):
        kind, j = i % 3, i // 3
        mod = mod_all[i]
        if kind == 0:
            s0, cbuf = states[i]
            gw = wts["gdn"][j]
            q, k, v, z, gb, ncbuf = _gdn_proj(x, mod, wts["norm_mix_g"][i], gw["wqkv"], gw["wz"], gw["wab"],
                                              gw["conv_w"], gw["pv"], cbuf, nb, tt)
            x, s_new = _gdn_core(q, k, v, z, gb, x, mod, gw["norm_g"], gw["wout"], s0,
                                 min(GDN_TILE, t), min(CHUNK, t))
            new_states.append((s_new, ncbuf))
        elif kind == 1:
            pk, pv = states[i]
            sw = wts["sb"][j]
            kf, vf, qb, kb, vb = _sb_proj(x, mod, wts["norm_mix_g"][i], sw["win"], sw["qkg"], nb, tt)
            o = _sb_attention(qb, kb, vb, pk, pv, min(SB_TILE, t))
            x = _out_proj(o, x, mod, sw["wout"], nb, tt)
            new_states.append((kf.reshape(b, t, SB_HEADS, SB_DH), vf.reshape(b, t, SB_HEADS, SB_DH)))
        else:
            (r0,) = states[i]
            rw = wts["ret"][j]
            chunk = min(RET_CHUNK, t)
            half = RET_DK // 2
            inv_freq = RET_ROPE_BASE ** (-jnp.arange(half, dtype=F32) / half)
            ang = (pos0 + jnp.arange(t)).astype(F32)[:, None] * inv_freq[None, :]
            q, qd, k, kd, v, gt = _ret_proj(x, mod, wts["norm_mix_g"][i], rw["win"], jnp.cos(ang), jnp.sin(ang),
                                            nb, tt, chunk)
            x, r_new = _ret_core(q, qd, k, kd, v, gt, x, mod, rw["norm_g"], rw["wout"], r0, chunk)
            new_states.append((r_new,))
        fw = wts["ffn"][i]
        x, fbuf = _ffn(x, mod, wts["norm_ffn_g"][i], fw["win"], fw["conv_w"], fw["conv_b"], fw["wout"],
                       ffn_bufs[i], nb, tt)
        new_ffn.append(fbuf)
    return x, new_states, jnp.stack(new_ffn)


def kernel(x_prompt, x_sample, state_l0_gdn_S, state_l0_gdn_conv, cache_l1_sb_k, cache_l1_sb_v, state_l2_ret, state_l3_gdn_S, state_l3_gdn_conv, state_ffn_conv, c_prompt, c_sample, ada_w, ada_b, norm_mix_g, norm_ffn_g, gdn_w_in, gdn_conv_w, gdn_A_log, gdn_dt_bias, gdn_norm_g, gdn_w_out, sb_w_in, sb_q_norm_g, sb_k_norm_g, sb_w_out, ret_w_in, ret_norm_g, ret_w_out, ffn_w_in, ffn_conv_w, ffn_conv_b, ffn_w_out):
    d = D_MODEL
    bp, tp, _ = x_prompt.shape
    bs, ts, _ = x_sample.shape

    hv = GDN_HEADS * GDN_DK
    wts = {"norm_mix_g": [norm_mix_g[i].reshape(1, d) for i in range(DEPTH)],
           "norm_ffn_g": [norm_ffn_g[i].reshape(1, d) for i in range(DEPTH)],
           "gdn": [], "sb": [], "ret": [], "ffn": []}
    for j in range(gdn_w_in.shape[0]):
        w = gdn_w_in[j]
        wab = jnp.pad(w[:, GDN_QKV + hv:], ((0, 0), (0, LANES - 2 * GDN_HEADS)))
        pv = jnp.zeros((8, LANES), F32)
        pv = pv.at[0, :GDN_HEADS].set(-jnp.exp(gdn_A_log[j].astype(F32)))
        pv = pv.at[1, :GDN_HEADS].set(gdn_dt_bias[j].astype(F32))
        wts["gdn"].append({"wqkv": w[:, :GDN_QKV].astype(BF16),
                           "wz": w[:, GDN_QKV:GDN_QKV + hv].astype(BF16),
                           "wab": wab.astype(BF16),
                           "conv_w": gdn_conv_w[j], "pv": pv,
                           "norm_g": gdn_norm_g[j].reshape(1, GDN_DK),
                           "wout": gdn_w_out[j].astype(BF16)})
    for j in range(sb_w_in.shape[0]):
        qkg = jnp.zeros((8, d), F32)
        qkg = qkg.at[0].set(jnp.tile(sb_q_norm_g[j], SB_HEADS)).at[1].set(jnp.tile(sb_k_norm_g[j], SB_HEADS))
        wts["sb"].append({"win": sb_w_in[j].astype(BF16), "qkg": qkg, "wout": sb_w_out[j].astype(BF16)})
    for j in range(ret_w_in.shape[0]):
        wts["ret"].append({"win": ret_w_in[j].astype(BF16),
                           "norm_g": ret_norm_g[j].reshape(1, RET_HEADS * RET_DV),
                           "wout": ret_w_out[j].astype(BF16)})
    for i in range(DEPTH):
        wts["ffn"].append({"win": ffn_w_in[i].astype(BF16), "conv_w": ffn_conv_w[i],
                           "conv_b": ffn_conv_b[i].reshape(1, 2 * D_FF), "wout": ffn_w_out[i].astype(BF16)})

    nrow = bp + bs
    rows = -(-nrow // 8) * 8
    c_all = jnp.pad(jnp.concatenate([c_prompt, c_sample], axis=0), ((0, rows - nrow), (0, 0)))
    mod = _ada_mod(c_all, ada_w, ada_b)
    mod = jnp.pad(mod.reshape(DEPTH, rows, 6, d), ((0, 0), (0, 0), (0, 2), (0, 0)))
    mod_p, mod_s = mod[:, :bp], mod[:, bp:nrow]

    dt = x_prompt.dtype
    zero_states = [(jnp.zeros((bp, GDN_HEADS, GDN_DK, GDN_DK), dt), jnp.zeros((bp, 3, GDN_QKV), dt)),
                   (None, None),
                   (jnp.zeros((bp, RET_HEADS, RET_DK, RET_DV), dt),),
                   (jnp.zeros((bp, GDN_HEADS, GDN_DK, GDN_DK), dt), jnp.zeros((bp, 3, GDN_QKV), dt))]
    zero_ffn = jnp.zeros((DEPTH, bp, 2, 2 * D_FF), dt)
    y_prompt, p_states, p_ffn_conv = _run_group(x_prompt, mod_p, zero_states, zero_ffn, 0, wts)

    past_len = cache_l1_sb_k.shape[1]
    sample_states = [(state_l0_gdn_S, state_l0_gdn_conv),
                     (cache_l1_sb_k, cache_l1_sb_v),
                     (state_l2_ret,), (state_l3_gdn_S, state_l3_gdn_conv)]
    y_sample, s_states, s_ffn_conv = _run_group(x_sample, mod_s, sample_states, state_ffn_conv, past_len, wts)

    (p_l0_S, p_l0_conv), (p_l1_k, p_l1_v), (p_l2_R,), (p_l3_S, p_l3_conv) = p_states
    (s_l0_S, s_l0_conv), (s_l1_k, s_l1_v), (s_l2_R,), (s_l3_S, s_l3_conv) = s_states
    return (y_prompt, y_sample,
            p_l0_S, p_l0_conv, p_l1_k, p_l1_v, p_l2_R, p_l3_S, p_l3_conv, p_ffn_conv,
            s_l0_S, s_l0_conv, s_l1_k, s_l1_v, s_l2_R, s_l3_S, s_l3_conv, s_ffn_conv)
```
